```python
import math
import jax, jax.numpy as jnp
from jax import lax
import numpy as np

D_MODEL = 2048
BATCH = 1
SEQ = 16384
DEPTH = 2

GRID_W = 64
CTX_LEN = 256

D_FOURIER = D_MODEL // 4
N_FOURIER_GROUPS = 4
FOURIER_GROUP = D_FOURIER // N_FOURIER_GROUPS
D_CONV = D_MODEL // 4
CONV_WIDTH = 31
CONV_PAD = CONV_WIDTH // 2
N_ATT_HEADS = 8
ATT_QK_DIM = 64
ATT_V_DIM = 2 * ATT_QK_DIM
D_ATT = N_ATT_HEADS * ATT_V_DIM
D_MIX = D_FOURIER + D_CONV + D_ATT
QK_COLS = N_ATT_HEADS * 2 * ATT_QK_DIM
ATT_SCALE = 1.0 / math.sqrt(ATT_QK_DIM)
Q_BLOCK = 128
ROPE_BASE = 10000.0
ROPE_PAIRS_PER_AXIS = ATT_QK_DIM // 4

OFF_CONV = D_FOURIER
OFF_Q = OFF_CONV + 2 * D_CONV
OFF_K = OFF_Q + QK_COLS
OFF_V = OFF_K + QK_COLS
D_IN_PROJ = OFF_V + D_ATT

N_GROUPS = 4
EXPERTS_PER_GROUP = 8
N_EXPERTS = N_GROUPS * EXPERTS_PER_GROUP
TOP_K = 2
D_EXPERT = D_MODEL // 4
MOE_BLOCK = 128

N_MOD = 6
LN_EPS = 1e-6
ALPHA = (2.0 * DEPTH) ** 0.25
BETA = (8.0 * DEPTH) ** -0.25

kernel_name = "hybrid_fourier_conv_diffattn_hmoe_dit"


def _layernorm(x):
    xf = x.astype(jnp.float32)
    mu = xf.mean(-1, keepdims=True)
    var = jnp.square(xf - mu).mean(-1, keepdims=True)
    return ((xf - mu) * lax.rsqrt(var + LN_EPS)).astype(x.dtype)


def _ln_affine(x, g, b):
    return _layernorm(x) * g + b


def _modulate(h, shift, scale):
    return h * (1.0 + scale[:, None, :]) + shift[:, None, :]


def _axial_rope_tables(n_tokens, dtype):
    rows = n_tokens // GRID_W
    row = jnp.repeat(jnp.arange(rows), GRID_W).astype(jnp.float32)
    col = jnp.tile(jnp.arange(GRID_W), rows).astype(jnp.float32)
    inv = 1.0 / (ROPE_BASE ** (jnp.arange(ROPE_PAIRS_PER_AXIS, dtype=jnp.float32) / ROPE_PAIRS_PER_AXIS))
    ang = jnp.concatenate([row[:, None] * inv, col[:, None] * inv], -1)
    return jnp.cos(ang).astype(dtype), jnp.sin(ang).astype(dtype)


def _apply_rope(x, cos, sin):
    x1, x2 = jnp.split(x, 2, axis=-1)
    return jnp.concatenate([x1 * cos - x2 * sin, x1 * sin + x2 * cos], -1)


def _qk_heads(p):
    B, L, _ = p.shape
    return p.reshape(B, L, N_ATT_HEADS, 2, ATT_QK_DIM).transpose(0, 2, 3, 1, 4)


def _v_heads(p):
    B, L, _ = p.shape
    return p.reshape(B, L, N_ATT_HEADS, ATT_V_DIM).transpose(0, 2, 1, 3)


def _diff_attend(q, k, v, lam):
    s = jnp.einsum('bhmqd,bhmkd->bhmqk', q, k).astype(jnp.float32) * ATT_SCALE
    p = jax.nn.softmax(s, axis=-1)
    a = p[:, :, 0] - lam * p[:, :, 1]
    return jnp.einsum('bhqk,bhkd->bhqd', a.astype(v.dtype), v)


def _diff_attention_blocks(q, k, v, lam):
    B, H, M, S, d = q.shape
    nb = S // Q_BLOCK
    qb = q.reshape(B, H, M, nb, Q_BLOCK, d).transpose(3, 0, 1, 2, 4, 5)
    ob = lax.map(lambda qi: _diff_attend(qi, k, v, lam), qb)
    return ob.transpose(1, 2, 0, 3, 4).reshape(B, H, S, ATT_V_DIM)


def _diff_head_out(o, g, lam_init):
    B, H, L, Dv = o.shape
    of = o.astype(jnp.float32)
    of = of * lax.rsqrt(jnp.mean(of * of, -1, keepdims=True) + LN_EPS) * g * (1.0 - lam_init)
    return of.astype(o.dtype).transpose(0, 2, 1, 3).reshape(B, L, H * Dv)


def _fourier_mix(u, w):
    B, L, _ = u.shape
    ug = u.astype(jnp.float32).reshape(B, L, N_FOURIER_GROUPS, FOURIER_GROUP)
    y = jnp.fft.fft2(ug, axes=(1, 3), norm='ortho').real
    return y.astype(u.dtype).reshape(B, L, D_FOURIER) @ w


def _conformer_conv(a, gt, conv_w, conv_b, ln_g, ln_b, w_pw):
    u = a * jax.nn.sigmoid(gt)
    y = lax.conv_general_dilated(u, conv_w[:, None, :], window_strides=(1,),
                                 padding=((CONV_PAD, CONV_PAD),),
                                 dimension_numbers=('NWC', 'WIO', 'NWC'),
                                 feature_group_count=D_CONV) + conv_b
    y = _ln_affine(y, ln_g, ln_b)
    return jax.nn.silu(y) @ w_pw


def _split_groups(pr):
    return (pr[..., :OFF_CONV], pr[..., OFF_CONV:OFF_CONV + D_CONV], pr[..., OFF_CONV + D_CONV:OFF_Q],
            pr[..., OFF_Q:OFF_K], pr[..., OFF_K:OFF_V], pr[..., OFF_V:])


def _grouped_experts(h, expert, weight, w_gate, w_up, w_down):
    N, D = h.shape
    K = expert.shape[1]
    A = N * K
    n_blocks = -(-A // MOE_BLOCK) + N_EXPERTS
    cap = n_blocks * MOE_BLOCK
    flat_e = expert.reshape(A)
    order = jnp.argsort(flat_e)
    e_sorted = flat_e[order]
    tok_sorted = (order // K).astype(jnp.int32)
    w_sorted = weight.reshape(A)[order]
    counts = jnp.bincount(flat_e, length=N_EXPERTS)
    padded = (counts + MOE_BLOCK - 1) // MOE_BLOCK * MOE_BLOCK
    pad_end = jnp.cumsum(padded)
    pad_start = pad_end - padded
    start = jnp.cumsum(counts) - counts
    dest = pad_start[e_sorted] + jnp.arange(A) - start[e_sorted]
    row_tok = jnp.full((cap,), N, jnp.int32).at[dest].set(tok_sorted)
    block_e = jnp.minimum(jnp.searchsorted(pad_end, jnp.arange(n_blocks) * MOE_BLOCK, side='right'),
                          N_EXPERTS - 1)
    h_pad = jnp.concatenate([h, jnp.zeros((1, D), h.dtype)], 0)
    xb = h_pad[row_tok].reshape(n_blocks, MOE_BLOCK, D)

    def expert_block(args):
        xblk, e = args
        return (jax.nn.silu(xblk @ w_gate[e]) * (xblk @ w_up[e])) @ w_down[e]

    yb = lax.map(expert_block, (xb, block_e)).reshape(cap, D)
    return jnp.zeros((N, D), h.dtype).at[tok_sorted].add(yb[dest] * w_sorted[:, None])


def _hier_moe(h, w_rg, b_rg, w_re, b_re, w_gate, w_up, w_down):
    N = h.shape[0]
    p_g = jax.nn.softmax((h @ w_rg).astype(jnp.float32) + b_rg, axis=-1)
    wg, gidx = lax.top_k(p_g, 1)
    logit_e = ((h @ w_re).astype(jnp.float32) + b_re).reshape(N, N_GROUPS, EXPERTS_PER_GROUP)
    logit_in = logit_e[jnp.arange(N), gidx[:, 0]]
    we, eidx = lax.top_k(jax.nn.softmax(logit_in, axis=-1), TOP_K)
    we = we / we.sum(-1, keepdims=True)
    weights = (wg * we).astype(h.dtype)
    expert = gidx * EXPERTS_PER_GROUP + eidx
    return _grouped_experts(h, expert, weights, w_gate, w_up, w_down)


def setup_inputs(seed: int = 0) -> dict:
    key = jax.random.key(seed)
    ks = iter(jax.random.split(key, 40))

    def nrm(shape, scale):
        return jax.random.normal(next(ks), shape, jnp.float32) * scale

    L = DEPTH
    w_in = nrm((L, D_MODEL, D_IN_PROJ), D_MODEL ** -0.5)
    w_in = w_in.at[:, :, OFF_V:].multiply(BETA)
    return {
        "x": nrm((BATCH, SEQ, D_MODEL), 1.0),
        "c": nrm((BATCH, D_MODEL), 1.0),
        "ctx": nrm((BATCH, CTX_LEN, D_MODEL), 1.0),
        "c_ctx": nrm((D_MODEL,), 1.0),
        "w_mod": nrm((L, D_MODEL, N_MOD * D_MODEL), 0.5 * D_MODEL ** -0.5),
        "b_mod": nrm((L, N_MOD * D_MODEL), 0.01),
        "w_in": w_in,
        "b_in": nrm((L, D_IN_PROJ), 0.01),
        "w_fourier": nrm((L, D_FOURIER, D_FOURIER), BETA * D_FOURIER ** -0.5),
        "conv_w": nrm((L, CONV_WIDTH, D_CONV), CONV_WIDTH ** -0.5),
        "conv_b": nrm((L, D_CONV), 0.01),
        "conv_ln_g": 1.0 + nrm((L, D_CONV), 0.01),
        "conv_ln_b": nrm((L, D_CONV), 0.01),
        "w_pw": nrm((L, D_CONV, D_CONV), BETA * D_CONV ** -0.5),
        "lam_q1": nrm((L, ATT_QK_DIM), 0.1),
        "lam_k1": nrm((L, ATT_QK_DIM), 0.1),
        "lam_q2": nrm((L, ATT_QK_DIM), 0.1),
        "lam_k2": nrm((L, ATT_QK_DIM), 0.1),
        "subln_g": 1.0 + nrm((L, ATT_V_DIM), 0.01),
        "w_out": nrm((L, D_MIX, D_MODEL), BETA * D_MIX ** -0.5),
        "ln_a_g": 1.0 + nrm((L, D_MODEL), 0.01),
        "ln_a_b": nrm((L, D_MODEL), 0.01),
        "w_rg": nrm((L, D_MODEL, N_GROUPS), D_MODEL ** -0.5),
        "b_rg": nrm((L, N_GROUPS), 0.01),
        "w_re": nrm((L, D_MODEL, N_EXPERTS), D_MODEL ** -0.5),
        "b_re": nrm((L, N_EXPERTS), 0.01),
        "w_gate": nrm((L, N_EXPERTS, D_MODEL, D_EXPERT), D_MODEL ** -0.5),
        "w_up": nrm((L, N_EXPERTS, D_MODEL, D_EXPERT), D_MODEL ** -0.5),
        "w_down": nrm((L, N_EXPERTS, D_EXPERT, D_MODEL), BETA * D_EXPERT ** -0.5),
        "ln_f_g": 1.0 + nrm((L, D_MODEL), 0.01),
        "ln_f_b": nrm((L, D_MODEL), 0.01),
    }


def reference(x, c, ctx, c_ctx, w_mod, b_mod, w_in, b_in, w_fourier, conv_w, conv_b, conv_ln_g, conv_ln_b,
              w_pw, lam_q1, lam_k1, lam_q2, lam_k2, subln_g, w_out, ln_a_g, ln_a_b, w_rg, b_rg, w_re, b_re,
              w_gate, w_up, w_down, ln_f_g, ln_f_b):
    B, S, D = x.shape
    cos, sin = _axial_rope_tables(S, x.dtype)
    sc_l = jax.nn.silu(c)
    sc_c = jax.nn.silu(c_ctx)[None, :]
    xl, xc = x, ctx
    for i in range(DEPTH):
        last = i == DEPTH - 1
        mod_l = jnp.split(sc_l @ w_mod[i] + b_mod[i], N_MOD, axis=-1)
        mod_c = jnp.split(sc_c @ w_mod[i] + b_mod[i], N_MOD, axis=-1)
        lam_init = 0.8 - 0.6 * math.exp(-0.3 * i)
        lam = (jnp.exp(jnp.sum(lam_q1[i].astype(jnp.float32) * lam_k1[i].astype(jnp.float32)))
               - jnp.exp(jnp.sum(lam_q2[i].astype(jnp.float32) * lam_k2[i].astype(jnp.float32)))
               + lam_init)

        hl = _modulate(_layernorm(xl), mod_l[0], mod_l[1])
        hc = _modulate(_layernorm(xc), mod_c[0], mod_c[1])
        fl, al, gl, ql, kl, vl = _split_groups(hl @ w_in[i] + b_in[i])
        if last:
            pkv = hc @ w_in[i][:, OFF_K:] + b_in[i][OFF_K:]
            kc, vc = pkv[..., :OFF_V - OFF_K], pkv[..., OFF_V - OFF_K:]
        else:
            fc, ac, gc, qc, kc, vc = _split_groups(hc @ w_in[i] + b_in[i])
        Kc, Vc = _qk_heads(kc), _v_heads(vc)
        Ql = _apply_rope(_qk_heads(ql), cos, sin)
        Kl = _apply_rope(_qk_heads(kl), cos, sin)
        K_all = jnp.concatenate([Kc, Kl], axis=3)
        V_all = jnp.concatenate([Vc, _v_heads(vl)], axis=2)
        att_l = _diff_head_out(_diff_attention_blocks(Ql, K_all, V_all, lam), subln_g[i], lam_init)
        mix_l = jnp.concatenate([
            _fourier_mix(fl, w_fourier[i]),
            _conformer_conv(al, gl, conv_w[i], conv_b[i], conv_ln_g[i], conv_ln_b[i], w_pw[i]),
            att_l], axis=-1) @ w_out[i]
        if not last:
            att_c = _diff_head_out(_diff_attend(_qk_heads(qc), Kc, Vc, lam), subln_g[i], lam_init)
            mix_c = jnp.concatenate([
                _fourier_mix(fc, w_fourier[i]),
                _conformer_conv(ac, gc, conv_w[i], conv_b[i], conv_ln_g[i], conv_ln_b[i], w_pw[i]),
                att_c], axis=-1) @ w_out[i]
            xc = _ln_affine(ALPHA * xc + mod_c[2][:, None, :] * mix_c, ln_a_g[i], ln_a_b[i])
        xl = _ln_affine(ALPHA * xl + mod_l[2][:, None, :] * mix_l, ln_a_g[i], ln_a_b[i])

        hl = _modulate(_layernorm(xl), mod_l[3], mod_l[4])
        moe = functools_partial_moe = (w_rg[i], b_rg[i], w_re[i], b_re[i], w_gate[i], w_up[i], w_down[i])
        if last:
            y_l = _hier_moe(hl.reshape(-1, D), *moe).reshape(hl.shape)
        else:
            hc = _modulate(_layernorm(xc), mod_c[3], mod_c[4])
            n_c = hc.shape[0] * hc.shape[1]
            y = _hier_moe(jnp.concatenate([hc.reshape(-1, D), hl.reshape(-1, D)], 0), *moe)
            y_c = y[:n_c].reshape(hc.shape)
            y_l = y[n_c:].reshape(hl.shape)
            xc = _ln_affine(ALPHA * xc + mod_c[5][:, None, :] * y_c, ln_f_g[i], ln_f_b[i])
        xl = _ln_affine(ALPHA * xl + mod_l[5][:, None, :] * y_l, ln_f_g[i], ln_f_b[i])
    return xl
```

```python
import functools
import math

import numpy as np
import jax
import jax.numpy as jnp
from jax import lax
from jax.experimental import pallas as pl
from jax.experimental.pallas import tpu as pltpu

F32 = jnp.float32
BF16 = jnp.bfloat16

D_MODEL = 2048
DEPTH = 2
GRID_W = 64
D_FOURIER = 512
N_FOURIER_GROUPS = 4
FOURIER_GROUP = 128
D_CONV = 512
CONV_WIDTH = 31
CONV_PAD = CONV_WIDTH // 2
N_ATT_HEADS = 8
ATT_QK_DIM = 64
ATT_V_DIM = 128
D_ATT = N_ATT_HEADS * ATT_V_DIM
D_MIX = D_FOURIER + D_CONV + D_ATT
QK_COLS = N_ATT_HEADS * 2 * ATT_QK_DIM
ATT_SCALE = 1.0 / math.sqrt(ATT_QK_DIM)
ROPE_BASE = 10000.0
ROPE_PAIRS_PER_AXIS = ATT_QK_DIM // 4
OFF_CONV = D_FOURIER
OFF_Q = OFF_CONV + 2 * D_CONV
OFF_K = OFF_Q + QK_COLS
OFF_V = OFF_K + QK_COLS
D_IN_PROJ = OFF_V + D_ATT
N_GROUPS = 4
EXPERTS_PER_GROUP = 8
N_EXPERTS = N_GROUPS * EXPERTS_PER_GROUP
TOP_K = 2
D_EXPERT = 512
N_MOD = 6
LN_EPS = 1e-6
ALPHA = (2.0 * DEPTH) ** 0.25

LANES = 128
VMEM_LIMIT = 56 * 1024 * 1024
LOG2E = 1.4426950408889634
Q_SCALE = ATT_SCALE * LOG2E
NEG_BIG = -1e30
DFT1 = 128
MOE_ROWS = 256
ROUTER_LANES = 128


def _cparams(*sem):
    return pltpu.CompilerParams(dimension_semantics=tuple(sem), vmem_limit_bytes=VMEM_LIMIT)


def _resident(shape):
    nd = len(shape)
    return pl.BlockSpec(shape, lambda *_: (0,) * nd, pipeline_mode=pl.Buffered(1))


def _ln(x):
    mu = jnp.mean(x, axis=-1, keepdims=True)
    xc = x - mu
    var = jnp.mean(xc * xc, axis=-1, keepdims=True)
    return xc * lax.rsqrt(var + LN_EPS)


def _dot(a, b):
    return jnp.dot(a, b, preferred_element_type=F32)


def _mod_body(c_ref, w_ref, b_ref, o_ref):
    c = c_ref[...]
    sc = (c * jax.nn.sigmoid(c)).astype(BF16)
    o_ref[0] = _dot(sc, w_ref[0].astype(BF16)) + b_ref[0]


def _modulation(c2, w_mod, b_mod):
    nl, d, n = w_mod.shape
    tn = 1024
    return pl.pallas_call(
        _mod_body,
        grid=(nl, n // tn),
        in_specs=[pl.BlockSpec((8, d), lambda l, j: (0, 0)),
                  pl.BlockSpec((1, d, tn), lambda l, j: (l, 0, j)),
                  pl.BlockSpec((1, 1, tn), lambda l, j: (l, 0, j))],
        out_specs=pl.BlockSpec((1, 8, tn), lambda l, j: (l, 0, j)),
        out_shape=jax.ShapeDtypeStruct((nl, 8, n), F32),
        compiler_params=_cparams("arbitrary", "arbitrary"),
    )(c2, w_mod, b_mod.reshape(nl, 1, n))


def _inproj_body(x_ref, mod_ref, w_ref, b_ref, cos_ref, sa_ref, sb_ref,
                 f_ref, u_ref, q_ref, k_ref, v_ref, *, rope):
    h = _ln(x_ref[...]) * (1.0 + mod_ref[1:2, :]) + mod_ref[0:1, :]
    hb = h.astype(BF16)

    def proj(lo, hi):
        return _dot(hb, w_ref[:, lo:hi]) + b_ref[:, lo:hi]

    f_ref[...] = proj(0, OFF_CONV).astype(BF16)
    a = proj(OFF_CONV, OFF_CONV + D_CONV)
    g = proj(OFF_CONV + D_CONV, OFF_Q)
    u_ref[...] = (a * jax.nn.sigmoid(g)).astype(BF16)
    v_ref[...] = proj(OFF_V, D_IN_PROJ).astype(BF16)
    q = proj(OFF_Q, OFF_K)
    k = proj(OFF_K, OFF_V)
    if rope:
        cos, sa, sb = cos_ref[...], sa_ref[...], sb_ref[...]
    for hd in range(N_ATT_HEADS):
        sl = slice(hd * LANES, (hd + 1) * LANES)
        qh, kh = q[:, sl], k[:, sl]
        if rope:
            qh = qh * cos + pltpu.roll(qh, LANES - 32, 1) * sa + pltpu.roll(qh, 32, 1) * sb
            kh = kh * cos + pltpu.roll(kh, LANES - 32, 1) * sa + pltpu.roll(kh, 32, 1) * sb
        q_ref[:, sl] = (qh * Q_SCALE).astype(BF16)
        k_ref[:, sl] = kh.astype(BF16)


def _inproj(x, mod8, w_bf, b, tables, *, rope, tm):
    s, d = x.shape
    cos, sa, sb = tables
    row = lambda n: pl.BlockSpec((tm, n), lambda i: (i, 0))
    outs = [(D_FOURIER, BF16), (D_CONV, BF16), (QK_COLS, BF16), (QK_COLS, BF16), (D_ATT, BF16)]
    return pl.pallas_call(
        functools.partial(_inproj_body, rope=rope),
        grid=(s // tm,),
        in_specs=[row(d), _resident((8, d)), _resident((d, D_IN_PROJ)), _resident((1, D_IN_PROJ)),
                  row(LANES), row(LANES), row(LANES)],
        out_specs=[row(n) for n, _ in outs],
        out_shape=[jax.ShapeDtypeStruct((s, n), dt) for n, dt in outs],
        compiler_params=_cparams("arbitrary"),
    )(x, mod8, w_bf, b.reshape(1, -1), cos, sa, sb)


def _rope_tables(n_tokens):
    rows = n_tokens // GRID_W
    row = jnp.repeat(jnp.arange(rows), GRID_W).astype(F32)
    col = jnp.tile(jnp.arange(GRID_W), rows).astype(F32)
    inv = 1.0 / (ROPE_BASE ** (jnp.arange(ROPE_PAIRS_PER_AXIS, dtype=F32) / ROPE_PAIRS_PER_AXIS))
    ang = jnp.concatenate([row[:, None] * inv, col[:, None] * inv], -1)
    cos, sin = jnp.cos(ang), jnp.sin(ang)
    zero = jnp.zeros_like(sin)
    cos_t = jnp.tile(cos, (1, 4))
    sa = jnp.tile(jnp.concatenate([-sin, zero], -1), (1, 2))
    sb = jnp.tile(jnp.concatenate([zero, sin], -1), (1, 2))
    return cos_t, sa, sb


def _dft_mats(n):
    k = np.arange(n)
    ang = 2.0 * np.pi * ((k[:, None] * k[None, :]) % n) / n
    return np.cos(ang), np.sin(ang)


def _channel_dft(scale):
    c, s = _dft_mats(FOURIER_GROUP)
    eye = np.eye(N_FOURIER_GROUPS)
    return (jnp.asarray(np.kron(eye, c) * scale, BF16), jnp.asarray(np.kron(eye, s) * scale, BF16))


def _fourier1_body(m_ref, x_ref, o_ref):
    o_ref[...] = _dot(m_ref[...], x_ref[...]).astype(BF16)


def _fourier2_body(hr_ref, hi_ref, tc_ref, ts_ref, m2_ref, cc_ref, sc_ref, wf_ref, o_ref, *, n2):
    tc_all, ts_all = tc_ref[0], ts_ref[0]
    for j in range(8):
        hr = hr_ref[j * n2:(j + 1) * n2, :].astype(F32)
        hi = hi_ref[j * n2:(j + 1) * n2, :].astype(F32)
        tc, ts = tc_all[:, j:j + 1], ts_all[:, j:j + 1]
        gr = hr * tc + hi * ts
        gi = hi * tc - hr * ts
        g = jnp.concatenate([gr, gi], axis=0).astype(BF16)
        y = _dot(m2_ref[...], g)
        z = _dot(y[:n2].astype(BF16), cc_ref[...]) + _dot(y[n2:].astype(BF16), sc_ref[...])
        o_ref[:, j, :] = _dot(z.astype(BF16), wf_ref[...]).astype(BF16)


def _fourier_fourstep(f, wf_bf):
    l, ch = f.shape
    n2 = l // DFT1
    c1, s1 = _dft_mats(DFT1)
    m1 = jnp.asarray(np.concatenate([c1, -s1], 0), BF16)
    tn = min(8192, n2 * ch)
    h = pl.pallas_call(
        _fourier1_body,
        grid=(n2 * ch // tn,),
        in_specs=[_resident((2 * DFT1, DFT1)), pl.BlockSpec((DFT1, tn), lambda j: (0, j))],
        out_specs=pl.BlockSpec((2 * DFT1, tn), lambda j: (0, j)),
        out_shape=jax.ShapeDtypeStruct((2 * DFT1, n2 * ch), BF16),
        compiler_params=_cparams("arbitrary"),
    )(m1, f.reshape(DFT1, n2 * ch))
    h = h.reshape(2 * DFT1 * n2, ch)

    k1 = np.arange(DFT1)
    nn = np.arange(n2)
    ang = 2.0 * np.pi * ((nn[:, None] * k1[None, :]) % l) / l
    tc = jnp.asarray(np.cos(ang).reshape(n2, DFT1 // 8, 8).transpose(1, 0, 2), F32)
    ts = jnp.asarray(np.sin(ang).reshape(n2, DFT1 // 8, 8).transpose(1, 0, 2), F32)
    c2, s2 = _dft_mats(n2)
    m2 = jnp.asarray(np.block([[c2, s2], [-s2, c2]]), BF16)
    cc, sc = _channel_dft(1.0 / math.sqrt(l * FOURIER_GROUP))
    nb = DFT1 // 8
    out = pl.pallas_call(
        functools.partial(_fourier2_body, n2=n2),
        grid=(nb,),
        in_specs=[pl.BlockSpec((8 * n2, ch), lambda b: (b, 0)),
                  pl.BlockSpec((8 * n2, ch), lambda b: (b + nb, 0)),
                  pl.BlockSpec((1, n2, 8), lambda b: (b, 0, 0)),
                  pl.BlockSpec((1, n2, 8), lambda b: (b, 0, 0)),
                  _resident((2 * n2, 2 * n2)), _resident((ch, ch)), _resident((ch, ch)), _resident((ch, ch))],
        out_specs=pl.BlockSpec((n2, 8, ch), lambda b: (0, b, 0)),
        out_shape=jax.ShapeDtypeStruct((n2, DFT1, ch), BF16),
        compiler_params=_cparams("arbitrary"),
    )(h, h, tc, ts, m2, cc, sc, wf_bf)
    return out.reshape(l, ch)


def _fourier_dense_body(f_ref, cc_ref, sc_ref, mp_ref, wf_ref, o_ref):
    f = f_ref[...]
    a = jnp.concatenate([_dot(f, cc_ref[...]), _dot(f, sc_ref[...])], axis=0).astype(BF16)
    z = _dot(mp_ref[...], a)
    o_ref[...] = _dot(z.astype(BF16), wf_ref[...]).astype(BF16)


def _fourier_dense(f, wf_bf):
    l, ch = f.shape
    cl, sl = _dft_mats(l)
    mp = jnp.asarray(np.concatenate([cl, -sl], 1), BF16)
    cc, sc = _channel_dft(1.0 / math.sqrt(l * FOURIER_GROUP))
    return pl.pallas_call(
        _fourier_dense_body,
        grid=(1,),
        in_specs=[_resident((l, ch)), _resident((ch, ch)), _resident((ch, ch)), _resident((l, 2 * l)),
                  _resident((ch, ch))],
        out_specs=pl.BlockSpec((l, ch), lambda i: (0, 0)),
        out_shape=jax.ShapeDtypeStruct((l, ch), BF16),
        compiler_params=_cparams("arbitrary"),
    )(f, cc, sc, mp, wf_bf)


CONV_HALO = 16
CONV_ROWS = 64


def _conv_body(prev_ref, cur_ref, next_ref, cw_ref, vec_ref, wpw_ref, o_ref, ubuf, *, tm, nt):
    t = pl.program_id(0)
    ubuf[0:CONV_HALO, :] = jnp.where(t > 0, prev_ref[...].astype(F32), 0.0)
    ubuf[CONV_HALO:CONV_HALO + tm, :] = cur_ref[...].astype(F32)
    ubuf[CONV_HALO + tm:2 * CONV_HALO + tm, :] = jnp.where(t < nt - 1, next_ref[...].astype(F32), 0.0)
    cb, lg, lb = vec_ref[0:1, :], vec_ref[1:2, :], vec_ref[2:3, :]
    first = CONV_HALO - CONV_PAD
    for c in range(tm // CONV_ROWS):
        base = c * CONV_ROWS + first
        acc = cw_ref[0:1, :] * ubuf[base:base + CONV_ROWS, :]
        for j in range(1, CONV_WIDTH):
            acc = acc + cw_ref[j:j + 1, :] * ubuf[base + j:base + j + CONV_ROWS, :]
        y = _ln(acc + cb) * lg + lb
        y = y * jax.nn.sigmoid(y)
        o_ref[c * CONV_ROWS:(c + 1) * CONV_ROWS, :] = _dot(y.astype(BF16), wpw_ref[...]).astype(BF16)


def _conv(u, conv_w, conv_b, ln_g, ln_b, wpw_bf, *, tm):
    s, ch = u.shape
    nt = s // tm
    hb = tm // CONV_HALO
    nh = s // CONV_HALO
    cw = jnp.zeros((32, ch), F32).at[:CONV_WIDTH].set(conv_w)
    vec = jnp.zeros((8, ch), F32).at[0].set(conv_b).at[1].set(ln_g).at[2].set(ln_b)
    return pl.pallas_call(
        functools.partial(_conv_body, tm=tm, nt=nt),
        grid=(nt,),
        in_specs=[pl.BlockSpec((CONV_HALO, ch), lambda t: (jnp.maximum(t * hb - 1, 0), 0)),
                  pl.BlockSpec((tm, ch), lambda t: (t, 0)),
                  pl.BlockSpec((CONV_HALO, ch), lambda t: (jnp.minimum((t + 1) * hb, nh - 1), 0)),
                  _resident((32, ch)), _resident((8, ch)), _resident((ch, ch))],
        out_specs=pl.BlockSpec((tm, ch), lambda t: (t, 0)),
        out_shape=jax.ShapeDtypeStruct((s, ch), BF16),
        scratch_shapes=[pltpu.VMEM((tm + 2 * CONV_HALO, ch), F32)],
        compiler_params=_cparams("arbitrary"),
    )(u, u, u, cw, vec, wpw_bf)


def _attn_body(lam_ref, g_ref, qt_ref, k_ref, vt_ref, o_ref, acc1, acc2, *, nk, tq, lam_init):
    qt = qt_ref[0]
    comp = lax.broadcasted_iota(jnp.int32, qt.shape, 0) < ATT_QK_DIM
    q1 = jnp.where(comp, qt, jnp.zeros_like(qt))
    q2 = jnp.where(comp, jnp.zeros_like(qt), qt)
    acc1[...] = jnp.zeros_like(acc1)
    acc2[...] = jnp.zeros_like(acc2)

    def online(kb, vt, qz, m, l, acc):
        s = _dot(kb, qz)
        mn = jnp.maximum(m, jnp.max(s, axis=0, keepdims=True))
        p = jnp.exp2(s - mn)
        alpha = jnp.exp2(m - mn)
        acc[...] = alpha * acc[...] + _dot(vt, p.astype(BF16))
        return mn, alpha * l + jnp.sum(p, axis=0, keepdims=True)

    def step(i, carry):
        m1, l1, m2, l2 = carry
        kb, vt = k_ref[0, i], vt_ref[0, i]
        m1, l1 = online(kb, vt, q1, m1, l1, acc1)
        m2, l2 = online(kb, vt, q2, m2, l2, acc2)
        return m1, l1, m2, l2

    neg = jnp.full((1, tq), -jnp.inf, F32)
    zero = jnp.zeros((1, tq), F32)
    _, l1, _, l2 = lax.fori_loop(0, nk, step, (neg, zero, neg, zero))

    lp = lam_ref[...]
    lam = (jnp.exp(jnp.sum(lp[0:1] * lp[1:2], axis=-1, keepdims=True))
           - jnp.exp(jnp.sum(lp[2:3] * lp[3:4], axis=-1, keepdims=True)) + lam_init)
    o = acc1[...] / l1 - lam * (acc2[...] / l2)
    ms = jnp.mean(o * o, axis=0, keepdims=True)
    o = o * lax.rsqrt(ms + LN_EPS) * g_ref[...] * (1.0 - lam_init)
    o_ref[...] = o.T.astype(BF16)


def _attention(q, k_all, v_all, lam_p, g, *, lam_init, tq, tk):
    s = q.shape[0]
    lk = k_all.shape[0]
    nk = lk // tk
    qt = q.reshape(s, N_ATT_HEADS, LANES).transpose(1, 2, 0)
    kk = k_all.reshape(nk, tk, N_ATT_HEADS, LANES).transpose(2, 0, 1, 3)
    vt = v_all.reshape(nk, tk, N_ATT_HEADS, ATT_V_DIM).transpose(2, 0, 3, 1)
    return pl.pallas_call(
        functools.partial(_attn_body, nk=nk, tq=tq, lam_init=lam_init),
        grid=(N_ATT_HEADS, s // tq),
        in_specs=[pl.BlockSpec((8, LANES), lambda h, i: (0, 0)),
                  pl.BlockSpec((ATT_V_DIM, 1), lambda h, i: (0, 0)),
                  pl.BlockSpec((1, LANES, tq), lambda h, i: (h, 0, i)),
                  pl.BlockSpec((1, nk, tk, LANES), lambda h, i: (h, 0, 0, 0)),
                  pl.BlockSpec((1, nk, ATT_V_DIM, tk), lambda h, i: (h, 0, 0, 0))],
        out_specs=pl.BlockSpec((tq, ATT_V_DIM), lambda h, i: (i, h)),
        out_shape=jax.ShapeDtypeStruct((s, D_ATT), BF16),
        scratch_shapes=[pltpu.VMEM((ATT_V_DIM, tq), F32), pltpu.VMEM((ATT_V_DIM, tq), F32)],
        compiler_params=_cparams("arbitrary", "arbitrary"),
    )(lam_p, g.reshape(ATT_V_DIM, 1), qt, kk, vt)


def _route(logits):
    lane = lax.broadcasted_iota(jnp.int32, logits.shape, 1)
    lane_f = lane.astype(F32)
    is_g = lane < N_GROUPS
    gl = jnp.where(is_g, logits, NEG_BIG)
    gmax = jnp.max(gl, axis=-1, keepdims=True)
    gidx = jnp.min(jnp.where(gl == gmax, lane_f, float(ROUTER_LANES)), axis=-1, keepdims=True)
    wg = 1.0 / jnp.sum(jnp.where(is_g, jnp.exp(gl - gmax), 0.0), axis=-1, keepdims=True)
    grp = ((lane - N_GROUPS) // EXPERTS_PER_GROUP).astype(F32)
    valid = (lane >= N_GROUPS) & (lane < N_GROUPS + N_EXPERTS) & (grp == gidx)
    el = jnp.where(valid, logits, NEG_BIG)
    m1 = jnp.max(el, axis=-1, keepdims=True)
    i1 = jnp.min(jnp.where(el == m1, lane_f, float(ROUTER_LANES)), axis=-1, keepdims=True)
    el2 = jnp.where(lane_f == i1, NEG_BIG, el)
    m2 = jnp.max(el2, axis=-1, keepdims=True)
    i2 = jnp.min(jnp.where(el2 == m2, lane_f, float(ROUTER_LANES)), axis=-1, keepdims=True)
    t = jnp.exp(m2 - m1)
    w1 = wg / (1.0 + t)
    w2 = wg * t / (1.0 + t)
    e1 = (i1 - N_GROUPS).astype(jnp.int32)
    e2 = (i2 - N_GROUPS).astype(jnp.int32)
    eidx = jnp.where(lane == 0, e1, jnp.where(lane == 1, e2, 0))
    wts = jnp.where(lane == 0, w1, jnp.where(lane == 1, w2, 0.0))
    return eidx, wts


def _outproj_body(x_ref, f_ref, c_ref, a_ref, w_ref, mod_ref, ln_ref, wr_ref, br_ref,
                  xo_ref, h2_ref, ei_ref, wt_ref):
    mix = (_dot(f_ref[...], w_ref[0:D_FOURIER, :])
           + _dot(c_ref[...], w_ref[D_FOURIER:D_FOURIER + D_CONV, :])
           + _dot(a_ref[...], w_ref[D_FOURIER + D_CONV:D_MIX, :]))
    r = ALPHA * x_ref[...] + mod_ref[2:3, :] * mix
    xn = _ln(r) * ln_ref[0:1, :] + ln_ref[1:2, :]
    xo_ref[...] = xn
    h2 = _ln(xn) * (1.0 + mod_ref[4:5, :]) + mod_ref[3:4, :]
    h2_ref[...] = h2
    logits = _dot(h2.astype(BF16), wr_ref[...]) + br_ref[...]
    eidx, wts = _route(logits)
    ei_ref[...] = eidx
    wt_ref[...] = wts


def _outproj(x, four, conv, att, w_out_bf, mod8, ln8, wr_bf, br, *, tm):
    s, d = x.shape
    row = lambda n: pl.BlockSpec((tm, n), lambda i: (i, 0))
    return pl.pallas_call(
        _outproj_body,
        grid=(s // tm,),
        in_specs=[row(d), row(D_FOURIER), row(D_CONV), row(D_ATT), _resident((D_MIX, d)), _resident((8, d)),
                  _resident((8, d)), _resident((d, ROUTER_LANES)), _resident((1, ROUTER_LANES))],
        out_specs=[row(d), row(d), row(ROUTER_LANES), row(ROUTER_LANES)],
        out_shape=[jax.ShapeDtypeStruct((s, d), F32), jax.ShapeDtypeStruct((s, d), F32),
                   jax.ShapeDtypeStruct((s, ROUTER_LANES), jnp.int32),
                   jax.ShapeDtypeStruct((s, ROUTER_LANES), F32)],
        compiler_params=_cparams("arbitrary"),
    )(x, four, conv, att, w_out_bf, mod8, ln8, wr_bf, br)


def _moe_body(src_ref, dst_ref, be_ref, first_ref, nused_ref,
              h_hbm, wg_ref, wu_ref, wd_ref, y_hbm,
              xbuf, ybuf, wgb, wub, wdb, gsem, ssem, *, tb, nb, dump0):
    del be_ref
    i = pl.program_id(0)
    nused = nused_ref[0]

    def start_gather(blk, slot):
        def body(r, c):
            tok = src_ref[blk * tb + r]
            pltpu.make_async_copy(h_hbm.at[pl.ds(tok, 1)], xbuf.at[slot, pl.ds(r, 1)], gsem.at[slot]).start()
            return c
        lax.fori_loop(0, tb, body, 0)

    def wait_gather(slot):
        pltpu.make_async_copy(h_hbm.at[pl.ds(0, tb)], xbuf.at[slot], gsem.at[slot]).wait()

    def start_scatter(blk, slot):
        def body(r, c):
            row = dst_ref[blk * tb + r]
            pltpu.make_async_copy(ybuf.at[slot, pl.ds(r, 1)], y_hbm.at[pl.ds(row, 1)], ssem.at[slot]).start()
            return c
        lax.fori_loop(0, tb, body, 0)

    def wait_scatter(slot):
        pltpu.make_async_copy(ybuf.at[slot], y_hbm.at[pl.ds(0, tb)], ssem.at[slot]).wait()

    @pl.when(i == 0)
    def _():
        start_gather(0, 0)
        ybuf[0] = jnp.zeros_like(ybuf[0])
        for half in range(2):
            cp = pltpu.make_async_copy(ybuf.at[0], y_hbm.at[pl.ds(dump0 + half * tb, tb)], ssem.at[0])
            cp.start()
            cp.wait()

    @pl.when(i + 1 < nused)
    def _():
        start_gather(i + 1, (i + 1) % 2)

    @pl.when(i < nused)
    def _():
        slot = i % 2
        wait_gather(slot)

        @pl.when(first_ref[i] == 1)
        def _():
            wgb[...] = wg_ref[0].astype(BF16)
            wub[...] = wu_ref[0].astype(BF16)
            wdb[...] = wd_ref[0].astype(BF16)

        x = xbuf[slot].astype(BF16)
        g = _dot(x, wgb[...])
        u = _dot(x, wub[...])
        mid = (g * jax.nn.sigmoid(g) * u).astype(BF16)
        y = _dot(mid, wdb[...])

        @pl.when(i >= 2)
        def _():
            wait_scatter(slot)

        ybuf[slot] = y
        start_scatter(i, slot)

    @pl.when(i == nb - 1)
    def _():
        wait_scatter((nused - 1) % 2)

        @pl.when(nused >= 2)
        def _():
            wait_scatter(nused % 2)


def _moe(h2, eidx2, w_gate, w_up, w_down):
    n, d = h2.shape
    tb = MOE_ROWS
    a = n * TOP_K
    nb = -(-a // tb) + N_EXPERTS
    cap = nb * tb
    flat_e = eidx2.reshape(a)
    order = jnp.argsort(flat_e)
    e_sorted = flat_e[order]
    tok = (order // TOP_K).astype(jnp.int32)
    slot = (order % TOP_K).astype(jnp.int32)
    counts = jnp.bincount(flat_e, length=N_EXPERTS)
    padded = (counts + tb - 1) // tb * tb
    pad_end = jnp.cumsum(padded)
    pad_start = pad_end - padded
    start = jnp.cumsum(counts) - counts
    dest = (pad_start[e_sorted] + jnp.arange(a) - start[e_sorted]).astype(jnp.int32)
    src = jnp.zeros((cap,), jnp.int32).at[dest].set(tok)
    dump = (TOP_K * n + jnp.arange(cap) % (2 * tb)).astype(jnp.int32)
    dst = dump.at[dest].set(slot * n + tok)
    block_e = jnp.minimum(jnp.searchsorted(pad_end, jnp.arange(nb) * tb, side='right'),
                          N_EXPERTS - 1).astype(jnp.int32)
    first = jnp.concatenate([jnp.ones((1,), jnp.int32), (block_e[1:] != block_e[:-1]).astype(jnp.int32)])
    nused = (pad_end[-1] // tb).astype(jnp.int32).reshape(1)

    wspec = lambda shape: pl.BlockSpec(shape, lambda i, src, dst, be, first, nused: (be[i], 0, 0))
    grid_spec = pltpu.PrefetchScalarGridSpec(
        num_scalar_prefetch=5,
        grid=(nb,),
        in_specs=[pl.BlockSpec(memory_space=pl.ANY),
                  wspec((1, d, D_EXPERT)), wspec((1, d, D_EXPERT)), wspec((1, D_EXPERT, d))],
        out_specs=pl.BlockSpec(memory_space=pl.ANY),
        scratch_shapes=[pltpu.VMEM((2, tb, d), F32), pltpu.VMEM((2, tb, d), F32),
                        pltpu.VMEM((d, D_EXPERT), BF16), pltpu.VMEM((d, D_EXPERT), BF16),
                        pltpu.VMEM((D_EXPERT, d), BF16),
                        pltpu.SemaphoreType.DMA((2,)), pltpu.SemaphoreType.DMA((2,))],
    )
    return pl.pallas_call(
        functools.partial(_moe_body, tb=tb, nb=nb, dump0=TOP_K * n),
        grid_spec=grid_spec,
        out_shape=jax.ShapeDtypeStruct((TOP_K * n + 2 * tb, d), F32),
        compiler_params=_cparams("arbitrary"),
    )(src, dst, block_e, first, nused, h2, w_gate, w_up, w_down)


def _final_body(x_ref, y0_ref, y1_ref, wt_ref, mod_ref, ln_ref, o_ref):
    w = wt_ref[...]
    y = w[:, 0:1] * y0_ref[...] + w[:, 1:2] * y1_ref[...]
    r = ALPHA * x_ref[...] + mod_ref[5:6, :] * y
    o_ref[...] = _ln(r) * ln_ref[0:1, :] + ln_ref[1:2, :]


def _final(x, y, wt, mod8, ln8, *, row0, n_tot, tm):
    s, d = x.shape
    o0, o1 = row0 // tm, (n_tot + row0) // tm
    return pl.pallas_call(
        _final_body,
        grid=(s // tm,),
        in_specs=[pl.BlockSpec((tm, d), lambda i: (i, 0)),
                  pl.BlockSpec((tm, d), lambda i: (i + o0, 0)),
                  pl.BlockSpec((tm, d), lambda i: (i + o1, 0)),
                  pl.BlockSpec((tm, ROUTER_LANES), lambda i: (i, 0)),
                  _resident((8, d)), _resident((8, d))],
        out_specs=pl.BlockSpec((tm, d), lambda i: (i, 0)),
        out_shape=jax.ShapeDtypeStruct((s, d), F32),
        compiler_params=_cparams("arbitrary"),
    )(x, y, y, wt, mod8, ln8)


def _pick(n, prefs):
    for t in prefs:
        if n % t == 0:
            return t
    raise ValueError(f"no tile of {prefs} divides {n}")


def _rows8(*vecs):
    d = vecs[0].shape[-1]
    out = jnp.zeros((8, d), F32)
    for j, v in enumerate(vecs):
        out = out.at[j].set(v)
    return out


def kernel(x, c, ctx, c_ctx, w_mod, b_mod, w_in, b_in, w_fourier, conv_w, conv_b, conv_ln_g, conv_ln_b, w_pw,
           lam_q1, lam_k1, lam_q2, lam_k2, subln_g, w_out, ln_a_g, ln_a_b, w_rg, b_rg, w_re, b_re,
           w_gate, w_up, w_down, ln_f_g, ln_f_b):
    b, s, d = x.shape
    nc = ctx.shape[1]
    assert b == 1 and d == D_MODEL and c.shape[0] == 1 and ctx.shape[0] == 1
    assert s % (DFT1 * 8) == 0 and nc % MOE_ROWS == 0
    xl, xc = x[0], ctx[0]
    depth = w_mod.shape[0]

    c2 = jnp.zeros((8, d), F32).at[0].set(c[0]).at[1].set(c_ctx)
    mod_all = _modulation(c2, w_mod, b_mod)
    tables_l = _rope_tables(s)
    tables_c = tuple(jnp.zeros((nc, LANES), F32) for _ in range(3))
    tm_l = _pick(s, (512, 256))
    tm_c = _pick(nc, (256,))
    tq = _pick(s, (512, 256))
    tk = _pick(s + nc, (1280, 256))

    for i in range(depth):
        last = i == depth - 1
        lam_init = 0.8 - 0.6 * math.exp(-0.3 * i)
        mod_l = _rows8(*jnp.split(mod_all[i, 0], N_MOD))
        mod_c = _rows8(*jnp.split(mod_all[i, 1], N_MOD))
        w_in_bf = w_in[i].astype(BF16)
        wf_bf = w_fourier[i].astype(BF16)
        wpw_bf = w_pw[i].astype(BF16)
        w_out_bf = w_out[i].astype(BF16)
        wr = jnp.zeros((d, ROUTER_LANES), F32).at[:, :N_GROUPS].set(w_rg[i])
        wr_bf = wr.at[:, N_GROUPS:N_GROUPS + N_EXPERTS].set(w_re[i]).astype(BF16)
        br = jnp.zeros((1, ROUTER_LANES), F32).at[0, :N_GROUPS].set(b_rg[i])
        br = br.at[0, N_GROUPS:N_GROUPS + N_EXPERTS].set(b_re[i])
        lam_p = jnp.zeros((8, LANES), F32).at[0, :ATT_QK_DIM].set(lam_q1[i]).at[1, :ATT_QK_DIM].set(lam_k1[i])
        lam_p = lam_p.at[2, :ATT_QK_DIM].set(lam_q2[i]).at[3, :ATT_QK_DIM].set(lam_k2[i])
        ln_a = _rows8(ln_a_g[i], ln_a_b[i])
        ln_f = _rows8(ln_f_g[i], ln_f_b[i])
        conv_args = (conv_w[i], conv_b[i], conv_ln_g[i], conv_ln_b[i], wpw_bf)

        f_l, u_l, q_l, k_l, v_l = _inproj(xl, mod_l, w_in_bf, b_in[i], tables_l, rope=True, tm=tm_l)
        f_c, u_c, q_c, k_c, v_c = _inproj(xc, mod_c, w_in_bf, b_in[i], tables_c, rope=False, tm=tm_c)
        k_all = jnp.concatenate([k_c, k_l], axis=0)
        v_all = jnp.concatenate([v_c, v_l], axis=0)
        att_l = _attention(q_l, k_all, v_all, lam_p, subln_g[i], lam_init=lam_init, tq=tq, tk=tk)
        four_l = _fourier_fourstep(f_l, wf_bf)
        conv_l = _conv(u_l, *conv_args, tm=tm_l)
        xl, h2_l, ei_l, wt_l = _outproj(xl, four_l, conv_l, att_l, w_out_bf, mod_l, ln_a, wr_bf, br, tm=tm_l)
        if last:
            y = _moe(h2_l, ei_l[:, :TOP_K], w_gate[i], w_up[i], w_down[i])
            xl = _final(xl, y, wt_l, mod_l, ln_f, row0=0, n_tot=s, tm=MOE_ROWS)
        else:
            att_c = _attention(q_c, k_c, v_c, lam_p, subln_g[i], lam_init=lam_init, tq=tm_c, tk=tm_c)
            four_c = _fourier_dense(f_c, wf_bf)
            conv_c = _conv(u_c, *conv_args, tm=tm_c)
            xc, h2_c, ei_c, wt_c = _outproj(xc, four_c, conv_c, att_c, w_out_bf, mod_c, ln_a, wr_bf, br, tm=tm_c)
            h2 = jnp.concatenate([h2_c, h2_l], axis=0)
            ei = jnp.concatenate([ei_c[:, :TOP_K], ei_l[:, :TOP_K]], axis=0)
            y = _moe(h2, ei, w_gate[i], w_up[i], w_down[i])
            xc = _final(xc, y, wt_c, mod_c, ln_f, row0=0, n_tot=s + nc, tm=MOE_ROWS)
            xl = _final(xl, y, wt_l, mod_l, ln_f, row0=nc, n_tot=s + nc, tm=MOE_ROWS)
    return xl[None]
```

```python
import functools
import math

import numpy as np
import jax
import jax.numpy as jnp
from jax import lax
from jax.experimental import pallas as pl
from jax.experimental.pallas import tpu as pltpu

F32 = jnp.float32
BF16 = jnp.bfloat16

D_MODEL = 2048
DEPTH = 2
GRID_W = 64
D_FOURIER = 512
N_FOURIER_GROUPS = 4
FOURIER_GROUP = 128
D_CONV = 512
CONV_WIDTH = 31
CONV_PAD = CONV_WIDTH // 2
N_ATT_HEADS = 8
ATT_QK_DIM = 64
ATT_V_DIM = 128
D_ATT = N_ATT_HEADS * ATT_V_DIM
D_MIX = D_FOURIER + D_CONV + D_ATT
QK_COLS = N_ATT_HEADS * 2 * ATT_QK_DIM
ATT_SCALE = 1.0 / math.sqrt(ATT_QK_DIM)
ROPE_BASE = 10000.0
ROPE_PAIRS_PER_AXIS = ATT_QK_DIM // 4
OFF_CONV = D_FOURIER
OFF_Q = OFF_CONV + 2 * D_CONV
OFF_K = OFF_Q + QK_COLS
OFF_V = OFF_K + QK_COLS
D_IN_PROJ = OFF_V + D_ATT
N_GROUPS = 4
EXPERTS_PER_GROUP = 8
N_EXPERTS = N_GROUPS * EXPERTS_PER_GROUP
TOP_K = 2
D_EXPERT = 512
N_MOD = 6
LN_EPS = 1e-6
ALPHA = (2.0 * DEPTH) ** 0.25

LANES = 128
VMEM_LIMIT = 56 * 1024 * 1024
LOG2E = 1.4426950408889634
Q_SCALE = ATT_SCALE * LOG2E
NEG_BIG = -1e30
DFT1 = 128
MOE_ROWS = 256
ROUTER_LANES = 128

def _cparams(*sem):
    return pltpu.CompilerParams(dimension_semantics=tuple(sem), vmem_limit_bytes=VMEM_LIMIT)


def _resident(shape):
    nd = len(shape)
    return pl.BlockSpec(shape, lambda *_: (0,) * nd, pipeline_mode=pl.Buffered(1))


def _ln(x):
    mu = jnp.mean(x, axis=-1, keepdims=True)
    xc = x - mu
    var = jnp.mean(xc * xc, axis=-1, keepdims=True)
    return xc * lax.rsqrt(var + LN_EPS)


def _dot(a, b):
    return jnp.dot(a, b, preferred_element_type=F32)


def _mod_body(c_ref, w_ref, b_ref, o_ref):
    c = c_ref[...]
    sc = (c * jax.nn.sigmoid(c)).astype(BF16)
    o_ref[0] = _dot(sc, w_ref[0].astype(BF16)) + b_ref[0]


def _modulation(c2, w_mod, b_mod):
    nl, d, n = w_mod.shape
    tn = 1024
    return pl.pallas_call(
        _mod_body,
        grid=(nl, n // tn),
        in_specs=[pl.BlockSpec((8, d), lambda l, j: (0, 0)),
                  pl.BlockSpec((1, d, tn), lambda l, j: (l, 0, j)),
                  pl.BlockSpec((1, 1, tn), lambda l, j: (l, 0, j))],
        out_specs=pl.BlockSpec((1, 8, tn), lambda l, j: (l, 0, j)),
        out_shape=jax.ShapeDtypeStruct((nl, 8, n), F32),
        compiler_params=_cparams("arbitrary", "arbitrary"),
    )(c2, w_mod, b_mod.reshape(nl, 1, n))


def _inproj_body(x_ref, mod_ref, w_ref, b_ref, cos_ref, sa_ref, sb_ref,
                 f_ref, u_ref, q_ref, k_ref, v_ref, *, rope):
    h = _ln(x_ref[...]) * (1.0 + mod_ref[1:2, :]) + mod_ref[0:1, :]
    hb = h.astype(BF16)

    def proj(lo, hi):
        return _dot(hb, w_ref[:, lo:hi]) + b_ref[:, lo:hi]

    f_ref[...] = proj(0, OFF_CONV).astype(BF16)
    a = proj(OFF_CONV, OFF_CONV + D_CONV)
    g = proj(OFF_CONV + D_CONV, OFF_Q)
    u_ref[...] = (a * jax.nn.sigmoid(g)).astype(BF16)
    v_ref[...] = proj(OFF_V, D_IN_PROJ).astype(BF16)
    q = proj(OFF_Q, OFF_K)
    k = proj(OFF_K, OFF_V)
    if rope:
        cos, sa, sb = cos_ref[...], sa_ref[...], sb_ref[...]
    for hd in range(N_ATT_HEADS):
        sl = slice(hd * LANES, (hd + 1) * LANES)
        qh, kh = q[:, sl], k[:, sl]
        if rope:
            qh = qh * cos + pltpu.roll(qh, LANES - 32, 1) * sa + pltpu.roll(qh, 32, 1) * sb
            kh = kh * cos + pltpu.roll(kh, LANES - 32, 1) * sa + pltpu.roll(kh, 32, 1) * sb
        q_ref[:, sl] = (qh * Q_SCALE).astype(BF16)
        k_ref[:, sl] = kh.astype(BF16)


def _inproj(x, mod8, w_bf, b, tables, *, rope, tm):
    s, d = x.shape
    cos, sa, sb = tables
    row = lambda n: pl.BlockSpec((tm, n), lambda i: (i, 0))
    outs = [(D_FOURIER, BF16), (D_CONV, BF16), (QK_COLS, BF16), (QK_COLS, BF16), (D_ATT, BF16)]
    return pl.pallas_call(
        functools.partial(_inproj_body, rope=rope),
        grid=(s // tm,),
        in_specs=[row(d), _resident((8, d)), _resident((d, D_IN_PROJ)), _resident((1, D_IN_PROJ)),
                  row(LANES), row(LANES), row(LANES)],
        out_specs=[row(n) for n, _ in outs],
        out_shape=[jax.ShapeDtypeStruct((s, n), dt) for n, dt in outs],
        compiler_params=_cparams("arbitrary"),
    )(x, mod8, w_bf, b.reshape(1, -1), cos, sa, sb)


def _rope_tables(n_tokens):
    rows = n_tokens // GRID_W
    row = jnp.repeat(jnp.arange(rows), GRID_W).astype(F32)
    col = jnp.tile(jnp.arange(GRID_W), rows).astype(F32)
    inv = 1.0 / (ROPE_BASE ** (jnp.arange(ROPE_PAIRS_PER_AXIS, dtype=F32) / ROPE_PAIRS_PER_AXIS))
    ang = jnp.concatenate([row[:, None] * inv, col[:, None] * inv], -1)
    cos, sin = jnp.cos(ang), jnp.sin(ang)
    zero = jnp.zeros_like(sin)
    cos_t = jnp.tile(cos, (1, 4))
    sa = jnp.tile(jnp.concatenate([-sin, zero], -1), (1, 2))
    sb = jnp.tile(jnp.concatenate([zero, sin], -1), (1, 2))
    return cos_t, sa, sb


def _dft_mats(n):
    k = np.arange(n)
    ang = 2.0 * np.pi * ((k[:, None] * k[None, :]) % n) / n
    return np.cos(ang), np.sin(ang)


def _channel_dft(scale):
    c, s = _dft_mats(FOURIER_GROUP)
    eye = np.eye(N_FOURIER_GROUPS)
    return (jnp.asarray(np.kron(eye, c) * scale, BF16), jnp.asarray(np.kron(eye, s) * scale, BF16))


def _fourier1_body(m_ref, x_ref, o_ref):
    o_ref[...] = _dot(m_ref[...], x_ref[...]).astype(BF16)


def _fourier2_body(hr_ref, hi_ref, tc_ref, ts_ref, m2_ref, cc_ref, sc_ref, wf_ref, o_ref, *, n2):
    tc_all, ts_all = tc_ref[0], ts_ref[0]
    for j in range(8):
        hr = hr_ref[j * n2:(j + 1) * n2, :].astype(F32)
        hi = hi_ref[j * n2:(j + 1) * n2, :].astype(F32)
        tc, ts = tc_all[:, j:j + 1], ts_all[:, j:j + 1]
        gr = hr * tc + hi * ts
        gi = hi * tc - hr * ts
        g = jnp.concatenate([gr, gi], axis=0).astype(BF16)
        y = _dot(m2_ref[...], g)
        z = _dot(y[:n2].astype(BF16), cc_ref[...]) + _dot(y[n2:].astype(BF16), sc_ref[...])
        o_ref[:, j, :] = _dot(z.astype(BF16), wf_ref[...]).astype(BF16)


def _fourier_fourstep(f, wf_bf):
    l, ch = f.shape
    n2 = l // DFT1
    c1, s1 = _dft_mats(DFT1)
    m1 = jnp.asarray(np.concatenate([c1, -s1], 0), BF16)
    tn = min(8192, n2 * ch)
    h = pl.pallas_call(
        _fourier1_body,
        grid=(n2 * ch // tn,),
        in_specs=[_resident((2 * DFT1, DFT1)), pl.BlockSpec((DFT1, tn), lambda j: (0, j))],
        out_specs=pl.BlockSpec((2 * DFT1, tn), lambda j: (0, j)),
        out_shape=jax.ShapeDtypeStruct((2 * DFT1, n2 * ch), BF16),
        compiler_params=_cparams("arbitrary"),
    )(m1, f.reshape(DFT1, n2 * ch))
    h = h.reshape(2 * DFT1 * n2, ch)

    k1 = np.arange(DFT1)
    nn = np.arange(n2)
    ang = 2.0 * np.pi * ((nn[:, None] * k1[None, :]) % l) / l
    tc = jnp.asarray(np.cos(ang).reshape(n2, DFT1 // 8, 8).transpose(1, 0, 2), F32)
    ts = jnp.asarray(np.sin(ang).reshape(n2, DFT1 // 8, 8).transpose(1, 0, 2), F32)
    c2, s2 = _dft_mats(n2)
    m2 = jnp.asarray(np.block([[c2, s2], [-s2, c2]]), BF16)
    cc, sc = _channel_dft(1.0 / math.sqrt(l * FOURIER_GROUP))
    nb = DFT1 // 8
    out = pl.pallas_call(
        functools.partial(_fourier2_body, n2=n2),
        grid=(nb,),
        in_specs=[pl.BlockSpec((8 * n2, ch), lambda b: (b, 0)),
                  pl.BlockSpec((8 * n2, ch), lambda b: (b + nb, 0)),
                  pl.BlockSpec((1, n2, 8), lambda b: (b, 0, 0)),
                  pl.BlockSpec((1, n2, 8), lambda b: (b, 0, 0)),
                  _resident((2 * n2, 2 * n2)), _resident((ch, ch)), _resident((ch, ch)), _resident((ch, ch))],
        out_specs=pl.BlockSpec((n2, 8, ch), lambda b: (0, b, 0)),
        out_shape=jax.ShapeDtypeStruct((n2, DFT1, ch), BF16),
        compiler_params=_cparams("arbitrary"),
    )(h, h, tc, ts, m2, cc, sc, wf_bf)
    return out.reshape(l, ch)


def _fourier_dense_body(f_ref, cc_ref, sc_ref, mp_ref, wf_ref, o_ref):
    f = f_ref[...]
    a = jnp.concatenate([_dot(f, cc_ref[...]), _dot(f, sc_ref[...])], axis=0).astype(BF16)
    z = _dot(mp_ref[...], a)
    o_ref[...] = _dot(z.astype(BF16), wf_ref[...]).astype(BF16)


def _fourier_dense(f, wf_bf):
    l, ch = f.shape
    cl, sl = _dft_mats(l)
    mp = jnp.asarray(np.concatenate([cl, -sl], 1), BF16)
    cc, sc = _channel_dft(1.0 / math.sqrt(l * FOURIER_GROUP))
    return pl.pallas_call(
        _fourier_dense_body,
        grid=(1,),
        in_specs=[_resident((l, ch)), _resident((ch, ch)), _resident((ch, ch)), _resident((l, 2 * l)),
                  _resident((ch, ch))],
        out_specs=pl.BlockSpec((l, ch), lambda i: (0, 0)),
        out_shape=jax.ShapeDtypeStruct((l, ch), BF16),
        compiler_params=_cparams("arbitrary"),
    )(f, cc, sc, mp, wf_bf)


CONV_HALO = 16
CONV_ROWS = 64


def _conv_body(prev_ref, cur_ref, next_ref, cw_ref, vec_ref, wpw_ref, o_ref, ubuf, *, tm, nt):
    t = pl.program_id(0)
    ubuf[0:CONV_HALO, :] = jnp.where(t > 0, prev_ref[...].astype(F32), 0.0)
    ubuf[CONV_HALO:CONV_HALO + tm, :] = cur_ref[...].astype(F32)
    ubuf[CONV_HALO + tm:2 * CONV_HALO + tm, :] = jnp.where(t < nt - 1, next_ref[...].astype(F32), 0.0)
    cb, lg, lb = vec_ref[0:1, :], vec_ref[1:2, :], vec_ref[2:3, :]
    first = CONV_HALO - CONV_PAD
    for c in range(tm // CONV_ROWS):
        base = c * CONV_ROWS + first
        acc = cw_ref[0:1, :] * ubuf[base:base + CONV_ROWS, :]
        for j in range(1, CONV_WIDTH):
            acc = acc + cw_ref[j:j + 1, :] * ubuf[base + j:base + j + CONV_ROWS, :]
        y = _ln(acc + cb) * lg + lb
        y = y * jax.nn.sigmoid(y)
        o_ref[c * CONV_ROWS:(c + 1) * CONV_ROWS, :] = _dot(y.astype(BF16), wpw_ref[...]).astype(BF16)


def _conv(u, conv_w, conv_b, ln_g, ln_b, wpw_bf, *, tm):
    s, ch = u.shape
    nt = s // tm
    hb = tm // CONV_HALO
    nh = s // CONV_HALO
    cw = jnp.zeros((32, ch), F32).at[:CONV_WIDTH].set(conv_w)
    vec = jnp.zeros((8, ch), F32).at[0].set(conv_b).at[1].set(ln_g).at[2].set(ln_b)
    return pl.pallas_call(
        functools.partial(_conv_body, tm=tm, nt=nt),
        grid=(nt,),
        in_specs=[pl.BlockSpec((CONV_HALO, ch), lambda t: (jnp.maximum(t * hb - 1, 0), 0)),
                  pl.BlockSpec((tm, ch), lambda t: (t, 0)),
                  pl.BlockSpec((CONV_HALO, ch), lambda t: (jnp.minimum((t + 1) * hb, nh - 1), 0)),
                  _resident((32, ch)), _resident((8, ch)), _resident((ch, ch))],
        out_specs=pl.BlockSpec((tm, ch), lambda t: (t, 0)),
        out_shape=jax.ShapeDtypeStruct((s, ch), BF16),
        scratch_shapes=[pltpu.VMEM((tm + 2 * CONV_HALO, ch), F32)],
        compiler_params=_cparams("arbitrary"),
    )(u, u, u, cw, vec, wpw_bf)


def _attn_body(lam_ref, g_ref, qt_ref, k_ref, vt_ref, o_ref, acc1, acc2, sa1, sa2, sb1, sb2,
               *, nk, tq, lam_init):
    qt = qt_ref[0]
    comp = lax.broadcasted_iota(jnp.int32, qt.shape, 0) < ATT_QK_DIM
    q1 = jnp.where(comp, qt, jnp.zeros_like(qt))
    q2 = jnp.where(comp, jnp.zeros_like(qt), qt)
    acc1[...] = jnp.zeros_like(acc1)
    acc2[...] = jnp.zeros_like(acc2)

    def scores(i, s1_ref, s2_ref):
        kb = k_ref[0, i]
        s1_ref[...] = _dot(kb, q1)
        s2_ref[...] = _dot(kb, q2)

    def online(vt, s_ref, m, l, acc):
        s = s_ref[...]
        mn = jnp.maximum(m, jnp.max(s, axis=0, keepdims=True))
        p = jnp.exp2(s - mn)
        alpha = jnp.exp2(m - mn)
        acc[...] = alpha * acc[...] + _dot(vt, p.astype(BF16))
        return mn, alpha * l + jnp.sum(p, axis=0, keepdims=True)

    def consume(i, s1_ref, s2_ref, carry):
        m1, l1, m2, l2 = carry
        vt = vt_ref[0, i]
        m1, l1 = online(vt, s1_ref, m1, l1, acc1)
        m2, l2 = online(vt, s2_ref, m2, l2, acc2)
        return m1, l1, m2, l2

    def pair(j, carry):
        i0 = 2 * j
        scores(i0 + 1, sb1, sb2)
        carry = consume(i0, sa1, sa2, carry)
        scores(i0 + 2, sa1, sa2)
        return consume(i0 + 1, sb1, sb2, carry)

    neg = jnp.full((1, tq), -jnp.inf, F32)
    zero = jnp.zeros((1, tq), F32)
    scores(0, sa1, sa2)
    carry = lax.fori_loop(0, nk // 2, pair, (neg, zero, neg, zero))
    _, l1, _, l2 = consume(nk - 1, sa1, sa2, carry)

    lp = lam_ref[...]
    lam = (jnp.exp(jnp.sum(lp[0:1] * lp[1:2], axis=-1, keepdims=True))
           - jnp.exp(jnp.sum(lp[2:3] * lp[3:4], axis=-1, keepdims=True)) + lam_init)
    o = acc1[...] / l1 - lam * (acc2[...] / l2)
    ms = jnp.mean(o * o, axis=0, keepdims=True)
    o = o * lax.rsqrt(ms + LN_EPS) * g_ref[...] * (1.0 - lam_init)
    o_ref[...] = o.T.astype(BF16)


def _attention(q, k_all, v_all, lam_p, g, *, lam_init, tq, tk):
    s = q.shape[0]
    lk = k_all.shape[0]
    nk = lk // tk
    assert nk % 2 == 1, "the score double-buffering consumes key chunks in pairs plus one"
    qt = q.reshape(s, N_ATT_HEADS, LANES).transpose(1, 2, 0)
    kk = k_all.reshape(nk, tk, N_ATT_HEADS, LANES).transpose(2, 0, 1, 3)
    vt = v_all.reshape(nk, tk, N_ATT_HEADS, ATT_V_DIM).transpose(2, 0, 3, 1)
    return pl.pallas_call(
        functools.partial(_attn_body, nk=nk, tq=tq, lam_init=lam_init),
        grid=(N_ATT_HEADS, s // tq),
        in_specs=[pl.BlockSpec((8, LANES), lambda h, i: (0, 0)),
                  pl.BlockSpec((ATT_V_DIM, 1), lambda h, i: (0, 0)),
                  pl.BlockSpec((1, LANES, tq), lambda h, i: (h, 0, i)),
                  pl.BlockSpec((1, nk, tk, LANES), lambda h, i: (h, 0, 0, 0)),
                  pl.BlockSpec((1, nk, ATT_V_DIM, tk), lambda h, i: (h, 0, 0, 0))],
        out_specs=pl.BlockSpec((tq, ATT_V_DIM), lambda h, i: (i, h)),
        out_shape=jax.ShapeDtypeStruct((s, D_ATT), BF16),
        scratch_shapes=[pltpu.VMEM((ATT_V_DIM, tq), F32), pltpu.VMEM((ATT_V_DIM, tq), F32)]
        + [pltpu.VMEM((tk, tq), F32) for _ in range(4)],
        compiler_params=_cparams("arbitrary", "arbitrary"),
    )(lam_p, g.reshape(ATT_V_DIM, 1), qt, kk, vt)


def _route(logits):
    lane = lax.broadcasted_iota(jnp.int32, logits.shape, 1)
    lane_f = lane.astype(F32)
    is_g = lane < N_GROUPS
    gl = jnp.where(is_g, logits, NEG_BIG)
    gmax = jnp.max(gl, axis=-1, keepdims=True)
    gidx = jnp.min(jnp.where(gl == gmax, lane_f, float(ROUTER_LANES)), axis=-1, keepdims=True)
    wg = 1.0 / jnp.sum(jnp.where(is_g, jnp.exp(gl - gmax), 0.0), axis=-1, keepdims=True)
    grp = ((lane - N_GROUPS) // EXPERTS_PER_GROUP).astype(F32)
    valid = (lane >= N_GROUPS) & (lane < N_GROUPS + N_EXPERTS) & (grp == gidx)
    el = jnp.where(valid, logits, NEG_BIG)
    m1 = jnp.max(el, axis=-1, keepdims=True)
    i1 = jnp.min(jnp.where(el == m1, lane_f, float(ROUTER_LANES)), axis=-1, keepdims=True)
    el2 = jnp.where(lane_f == i1, NEG_BIG, el)
    m2 = jnp.max(el2, axis=-1, keepdims=True)
    i2 = jnp.min(jnp.where(el2 == m2, lane_f, float(ROUTER_LANES)), axis=-1, keepdims=True)
    t = jnp.exp(m2 - m1)
    w1 = wg / (1.0 + t)
    w2 = wg * t / (1.0 + t)
    e1 = (i1 - N_GROUPS).astype(jnp.int32)
    e2 = (i2 - N_GROUPS).astype(jnp.int32)
    eidx = jnp.where(lane == 0, e1, jnp.where(lane == 1, e2, 0))
    wts = jnp.where(lane == 0, w1, jnp.where(lane == 1, w2, 0.0))
    return eidx, wts


def _outproj_body(x_ref, f_ref, c_ref, a_ref, w_ref, mod_ref, ln_ref, wr_ref, br_ref,
                  xo_ref, h2_ref, ei_ref, wt_ref):
    mix = (_dot(f_ref[...], w_ref[0:D_FOURIER, :])
           + _dot(c_ref[...], w_ref[D_FOURIER:D_FOURIER + D_CONV, :])
           + _dot(a_ref[...], w_ref[D_FOURIER + D_CONV:D_MIX, :]))
    r = ALPHA * x_ref[...] + mod_ref[2:3, :] * mix
    xn = _ln(r) * ln_ref[0:1, :] + ln_ref[1:2, :]
    xo_ref[...] = xn
    h2 = _ln(xn) * (1.0 + mod_ref[4:5, :]) + mod_ref[3:4, :]
    h2_ref[...] = h2
    logits = _dot(h2.astype(BF16), wr_ref[...]) + br_ref[...]
    eidx, wts = _route(logits)
    ei_ref[...] = eidx
    wt_ref[...] = wts


def _outproj(x, four, conv, att, w_out_bf, mod8, ln8, wr_bf, br, *, tm):
    s, d = x.shape
    row = lambda n: pl.BlockSpec((tm, n), lambda i: (i, 0))
    return pl.pallas_call(
        _outproj_body,
        grid=(s // tm,),
        in_specs=[row(d), row(D_FOURIER), row(D_CONV), row(D_ATT), _resident((D_MIX, d)), _resident((8, d)),
                  _resident((8, d)), _resident((d, ROUTER_LANES)), _resident((1, ROUTER_LANES))],
        out_specs=[row(d), row(d), row(ROUTER_LANES), row(ROUTER_LANES)],
        out_shape=[jax.ShapeDtypeStruct((s, d), F32), jax.ShapeDtypeStruct((s, d), F32),
                   jax.ShapeDtypeStruct((s, ROUTER_LANES), jnp.int32),
                   jax.ShapeDtypeStruct((s, ROUTER_LANES), F32)],
        compiler_params=_cparams("arbitrary"),
    )(x, four, conv, att, w_out_bf, mod8, ln8, wr_bf, br)


TOK_TILE = 256


def _moe_plan(eidx2, tb):
    n = eidx2.shape[0]
    nb = -(-(n * TOP_K) // tb) + N_EXPERTS
    onehot = (eidx2[:, :, None] == jnp.arange(N_EXPERTS, dtype=jnp.int32)).astype(jnp.int32)
    per_tok = onehot.sum(axis=1)
    incl = jnp.cumsum(per_tok, axis=0)
    counts = incl[-1]
    padded = (counts + tb - 1) // tb * tb
    pad_end = jnp.cumsum(padded)
    rank_base = (incl - per_tok) + (pad_end - padded)[None, :]
    dest = jnp.sum(onehot * rank_base[:, None, :], axis=-1).astype(jnp.int32)
    block_e = jnp.minimum(jnp.searchsorted(pad_end, jnp.arange(nb) * tb, side='right'),
                          N_EXPERTS - 1).astype(jnp.int32)
    first = jnp.concatenate([jnp.ones((1,), jnp.int32), (block_e[1:] != block_e[:-1]).astype(jnp.int32)])
    nused = (pad_end[-1] // tb).astype(jnp.int32).reshape(1)
    zrow = jnp.maximum(pad_end - tb, 0).astype(jnp.int32)
    zflag = (counts > 0).astype(jnp.int32)
    return dest, block_e, first, nused, zrow, zflag, nb


def _dispatch_body(zrow_ref, zflag_ref, nused_ref, dest_ref, h_hbm, xs_hbm, zbuf, sem, zsem, *, tb, nt, nb):
    i = pl.program_id(0)

    def zero_copy(e):
        return pltpu.make_async_copy(zbuf, xs_hbm.at[pl.ds(pl.multiple_of(zrow_ref[e], tb), tb)], zsem)

    @pl.when(i == 0)
    def _():
        zbuf[...] = jnp.zeros_like(zbuf)
        for e in range(N_EXPERTS):
            @pl.when(zflag_ref[e] == 1)
            def _():
                zero_copy(e).start()
        for e in range(N_EXPERTS):
            @pl.when(zflag_ref[e] == 1)
            def _():
                zero_copy(e).wait()

        def tail_copy(j):
            return pltpu.make_async_copy(zbuf, xs_hbm.at[pl.ds(pl.multiple_of(j * tb, tb), tb)], zsem)

        def start_tail(j, c):
            tail_copy(j).start()
            return c

        def wait_tail(j, c):
            tail_copy(j).wait()
            return c
        lax.fori_loop(nused_ref[0], nb, start_tail, 0)
        lax.fori_loop(nused_ref[0], nb, wait_tail, 0)

    slot = i % 2

    def body(r, c):
        src = h_hbm.at[pl.ds(i * TOK_TILE + r, 1)]
        for k in range(TOP_K):
            row = dest_ref[0, 0, TOP_K * r + k]
            pltpu.make_async_copy(src, xs_hbm.at[pl.ds(row, 1)], sem.at[slot]).start()
        return c
    lax.fori_loop(0, TOK_TILE, body, 0)

    def wait_tile(sl):
        rows = TOP_K * TOK_TILE
        pltpu.make_async_copy(h_hbm.at[pl.ds(0, rows)], xs_hbm.at[pl.ds(0, rows)], sem.at[sl]).wait()

    @pl.when(i > 0)
    def _():
        wait_tile(1 - slot)

    @pl.when(i == nt - 1)
    def _():
        wait_tile(slot)


def _dispatch(h2, dest, zrow, zflag, nused, *, tb, nb):
    n, d = h2.shape
    nt = n // TOK_TILE
    grid_spec = pltpu.PrefetchScalarGridSpec(
        num_scalar_prefetch=3,
        grid=(nt,),
        in_specs=[pl.BlockSpec((1, 1, TOP_K * TOK_TILE), lambda i, *_: (i, 0, 0), memory_space=pltpu.SMEM),
                  pl.BlockSpec(memory_space=pl.ANY)],
        out_specs=pl.BlockSpec(memory_space=pl.ANY),
        scratch_shapes=[pltpu.VMEM((tb, d), F32), pltpu.SemaphoreType.DMA((2,)), pltpu.SemaphoreType.DMA(())],
    )
    return pl.pallas_call(
        functools.partial(_dispatch_body, tb=tb, nt=nt, nb=nb),
        grid_spec=grid_spec,
        out_shape=jax.ShapeDtypeStruct((nb * tb, d), F32),
        compiler_params=_cparams("arbitrary"),
    )(zrow, zflag, nused, dest.reshape(nt, 1, TOP_K * TOK_TILE), h2)


def _experts_body(be_ref, first_ref, nused_ref, x_ref, wg_ref, wu_ref, wd_ref, o_ref, wgb, wub, wdb):
    del be_ref
    i = pl.program_id(0)

    @pl.when(i < nused_ref[0])
    def _():
        @pl.when(first_ref[i] == 1)
        def _():
            wgb[...] = wg_ref[0].astype(BF16)
            wub[...] = wu_ref[0].astype(BF16)
            wdb[...] = wd_ref[0].astype(BF16)

        x = x_ref[...].astype(BF16)
        g = _dot(x, wgb[...])
        u = _dot(x, wub[...])
        mid = (g * jax.nn.sigmoid(g) * u).astype(BF16)
        o_ref[...] = _dot(mid, wdb[...])

    @pl.when(i >= nused_ref[0])
    def _():
        o_ref[...] = jnp.zeros_like(o_ref)


def _experts(xs, block_e, first, nused, w_gate, w_up, w_down, *, tb, nb):
    d = xs.shape[1]
    wspec = lambda shape: pl.BlockSpec(shape, lambda i, be, first, nused: (be[i], 0, 0))
    xspec = pl.BlockSpec((tb, d), lambda i, be, first, nused: (jnp.minimum(i, nused[0] - 1), 0))
    grid_spec = pltpu.PrefetchScalarGridSpec(
        num_scalar_prefetch=3,
        grid=(nb,),
        in_specs=[xspec, wspec((1, d, D_EXPERT)), wspec((1, d, D_EXPERT)), wspec((1, D_EXPERT, d))],
        out_specs=pl.BlockSpec((tb, d), lambda i, *_: (i, 0)),
        scratch_shapes=[pltpu.VMEM((d, D_EXPERT), BF16), pltpu.VMEM((d, D_EXPERT), BF16),
                        pltpu.VMEM((D_EXPERT, d), BF16)],
    )
    return pl.pallas_call(
        _experts_body,
        grid_spec=grid_spec,
        out_shape=jax.ShapeDtypeStruct((nb * tb, d), F32),
        compiler_params=_cparams("arbitrary"),
    )(block_e, first, nused, xs, w_gate, w_up, w_down)


def _combine_body(dcur_ref, dnext_ref, x_ref, wt_ref, mod_ref, ln_ref, ys_hbm, o_ref, ybuf, sem, *, nt):
    i = pl.program_id(0)

    def issue(d_ref, slot):
        def body(r, c):
            for k in range(TOP_K):
                row = d_ref[0, 0, TOP_K * r + k]
                pltpu.make_async_copy(ys_hbm.at[pl.ds(row, 1)], ybuf.at[slot, k, pl.ds(r, 1)], sem.at[slot]).start()
            return c
        lax.fori_loop(0, TOK_TILE, body, 0)

    @pl.when(i == 0)
    def _():
        issue(dcur_ref, 0)

    @pl.when(i + 1 < nt)
    def _():
        issue(dnext_ref, (i + 1) % 2)

    slot = i % 2
    for k in range(TOP_K):
        pltpu.make_async_copy(ys_hbm.at[pl.ds(0, TOK_TILE)], ybuf.at[slot, k], sem.at[slot]).wait()
    w = wt_ref[...]
    y = w[:, 0:1] * ybuf[slot, 0] + w[:, 1:2] * ybuf[slot, 1]
    r = ALPHA * x_ref[...] + mod_ref[5:6, :] * y
    o_ref[...] = _ln(r) * ln_ref[0:1, :] + ln_ref[1:2, :]


def _combine(x, ys, dest, wt, mod8, ln8, *, row0):
    s, d = x.shape
    nt = s // TOK_TILE
    t0 = row0 // TOK_TILE
    dest3 = dest.reshape(-1, 1, TOP_K * TOK_TILE)
    dspec = lambda off: pl.BlockSpec((1, 1, TOP_K * TOK_TILE),
                                     lambda i: (t0 + jnp.minimum(i + off, nt - 1), 0, 0), memory_space=pltpu.SMEM)
    return pl.pallas_call(
        functools.partial(_combine_body, nt=nt),
        grid=(nt,),
        in_specs=[dspec(0), dspec(1),
                  pl.BlockSpec((TOK_TILE, d), lambda i: (i, 0)),
                  pl.BlockSpec((TOK_TILE, ROUTER_LANES), lambda i: (i, 0)),
                  _resident((8, d)), _resident((8, d)),
                  pl.BlockSpec(memory_space=pl.ANY)],
        out_specs=pl.BlockSpec((TOK_TILE, d), lambda i: (i, 0)),
        out_shape=jax.ShapeDtypeStruct((s, d), F32),
        scratch_shapes=[pltpu.VMEM((2, TOP_K, TOK_TILE, d), F32), pltpu.SemaphoreType.DMA((2,))],
        compiler_params=_cparams("arbitrary"),
    )(dest3, dest3, x, wt, mod8, ln8, ys)


def _moe(h2, eidx2, w_gate, w_up, w_down):
    tb = MOE_ROWS
    dest, block_e, first, nused, zrow, zflag, nb = _moe_plan(eidx2, tb)
    xs = _dispatch(h2, dest, zrow, zflag, nused, tb=tb, nb=nb)
    ys = _experts(xs, block_e, first, nused, w_gate, w_up, w_down, tb=tb, nb=nb)
    return ys, dest


def _pick(n, prefs):
    for t in prefs:
        if n % t == 0:
            return t
    raise ValueError(f"no tile of {prefs} divides {n}")


def _rows8(*vecs):
    d = vecs[0].shape[-1]
    out = jnp.zeros((8, d), F32)
    for j, v in enumerate(vecs):
        out = out.at[j].set(v)
    return out


def kernel(x, c, ctx, c_ctx, w_mod, b_mod, w_in, b_in, w_fourier, conv_w, conv_b, conv_ln_g, conv_ln_b, w_pw,
           lam_q1, lam_k1, lam_q2, lam_k2, subln_g, w_out, ln_a_g, ln_a_b, w_rg, b_rg, w_re, b_re,
           w_gate, w_up, w_down, ln_f_g, ln_f_b):
    b, s, d = x.shape
    nc = ctx.shape[1]
    assert b == 1 and d == D_MODEL and c.shape[0] == 1 and ctx.shape[0] == 1
    assert s % (DFT1 * 8) == 0 and nc % TOK_TILE == 0
    xl, xc = x[0], ctx[0]
    depth = w_mod.shape[0]

    c2 = jnp.zeros((8, d), F32).at[0].set(c[0]).at[1].set(c_ctx)
    mod_all = _modulation(c2, w_mod, b_mod)
    tables_l = _rope_tables(s)
    tables_c = tuple(jnp.zeros((nc, LANES), F32) for _ in range(3))
    tm_l = _pick(s, (512, 256))
    tm_c = _pick(nc, (256,))
    tq = _pick(s, (512, 256))
    tk = _pick(s + nc, (1280, 256))

    for i in range(depth):
        last = i == depth - 1
        lam_init = 0.8 - 0.6 * math.exp(-0.3 * i)
        mod_l = _rows8(*jnp.split(mod_all[i, 0], N_MOD))
        mod_c = _rows8(*jnp.split(mod_all[i, 1], N_MOD))
        w_in_bf = w_in[i].astype(BF16)
        wf_bf = w_fourier[i].astype(BF16)
        wpw_bf = w_pw[i].astype(BF16)
        w_out_bf = w_out[i].astype(BF16)
        wr = jnp.zeros((d, ROUTER_LANES), F32).at[:, :N_GROUPS].set(w_rg[i])
        wr_bf = wr.at[:, N_GROUPS:N_GROUPS + N_EXPERTS].set(w_re[i]).astype(BF16)
        br = jnp.zeros((1, ROUTER_LANES), F32).at[0, :N_GROUPS].set(b_rg[i])
        br = br.at[0, N_GROUPS:N_GROUPS + N_EXPERTS].set(b_re[i])
        lam_p = jnp.zeros((8, LANES), F32).at[0, :ATT_QK_DIM].set(lam_q1[i]).at[1, :ATT_QK_DIM].set(lam_k1[i])
        lam_p = lam_p.at[2, :ATT_QK_DIM].set(lam_q2[i]).at[3, :ATT_QK_DIM].set(lam_k2[i])
        ln_a = _rows8(ln_a_g[i], ln_a_b[i])
        ln_f = _rows8(ln_f_g[i], ln_f_b[i])
        conv_args = (conv_w[i], conv_b[i], conv_ln_g[i], conv_ln_b[i], wpw_bf)

        f_l, u_l, q_l, k_l, v_l = _inproj(xl, mod_l, w_in_bf, b_in[i], tables_l, rope=True, tm=tm_l)
        f_c, u_c, q_c, k_c, v_c = _inproj(xc, mod_c, w_in_bf, b_in[i], tables_c, rope=False, tm=tm_c)
        k_all = jnp.concatenate([k_c, k_l], axis=0)
        v_all = jnp.concatenate([v_c, v_l], axis=0)
        att_l = _attention(q_l, k_all, v_all, lam_p, subln_g[i], lam_init=lam_init, tq=tq, tk=tk)
        four_l = _fourier_fourstep(f_l, wf_bf)
        conv_l = _conv(u_l, *conv_args, tm=tm_l)
        xl, h2_l, ei_l, wt_l = _outproj(xl, four_l, conv_l, att_l, w_out_bf, mod_l, ln_a, wr_bf, br, tm=tm_l)
        if last:
            ys, dest = _moe(h2_l, ei_l[:, :TOP_K], w_gate[i], w_up[i], w_down[i])
            xl = _combine(xl, ys, dest, wt_l, mod_l, ln_f, row0=0)
        else:
            att_c = _attention(q_c, k_c, v_c, lam_p, subln_g[i], lam_init=lam_init, tq=tm_c, tk=tm_c)
            four_c = _fourier_dense(f_c, wf_bf)
            conv_c = _conv(u_c, *conv_args, tm=tm_c)
            xc, h2_c, ei_c, wt_c = _outproj(xc, four_c, conv_c, att_c, w_out_bf, mod_c, ln_a, wr_bf, br, tm=tm_c)
            h2 = jnp.concatenate([h2_c, h2_l], axis=0)
            ei = jnp.concatenate([ei_c[:, :TOP_K], ei_l[:, :TOP_K]], axis=0)
            ys, dest = _moe(h2, ei, w_gate[i], w_up[i], w_down[i])
            xc = _combine(xc, ys, dest, wt_c, mod_c, ln_f, row0=0)
            xl = _combine(xl, ys, dest, wt_l, mod_l, ln_f, row0=nc)
    return xl[None]
```

```python
import functools
import math

import numpy as np
import jax
import jax.numpy as jnp
from jax import lax
from jax.experimental import pallas as pl
from jax.experimental.pallas import tpu as pltpu

F32 = jnp.float32
BF16 = jnp.bfloat16

D_MODEL = 2048
DEPTH = 2
GRID_W = 64
D_FOURIER = 512
N_FOURIER_GROUPS = 4
FOURIER_GROUP = 128
D_CONV = 512
CONV_WIDTH = 31
CONV_PAD = CONV_WIDTH // 2
N_ATT_HEADS = 8
ATT_QK_DIM = 64
ATT_V_DIM = 128
D_ATT = N_ATT_HEADS * ATT_V_DIM
D_MIX = D_FOURIER + D_CONV + D_ATT
QK_COLS = N_ATT_HEADS * 2 * ATT_QK_DIM
ATT_SCALE = 1.0 / math.sqrt(ATT_QK_DIM)
ROPE_BASE = 10000.0
ROPE_PAIRS_PER_AXIS = ATT_QK_DIM // 4
OFF_CONV = D_FOURIER
OFF_Q = OFF_CONV + 2 * D_CONV
OFF_K = OFF_Q + QK_COLS
OFF_V = OFF_K + QK_COLS
D_IN_PROJ = OFF_V + D_ATT
N_GROUPS = 4
EXPERTS_PER_GROUP = 8
N_EXPERTS = N_GROUPS * EXPERTS_PER_GROUP
TOP_K = 2
D_EXPERT = 512
N_MOD = 6
LN_EPS = 1e-6
ALPHA = (2.0 * DEPTH) ** 0.25

LANES = 128
VMEM_LIMIT = 56 * 1024 * 1024
LOG2E = 1.4426950408889634
Q_SCALE = ATT_SCALE * LOG2E
NEG_BIG = -1e30
DFT1 = 128
MOE_ROWS = 256
ROUTER_LANES = 128

def _cparams(*sem):
    return pltpu.CompilerParams(dimension_semantics=tuple(sem), vmem_limit_bytes=VMEM_LIMIT)


def _resident(shape):
    nd = len(shape)
    return pl.BlockSpec(shape, lambda *_: (0,) * nd, pipeline_mode=pl.Buffered(1))


def _ln(x):
    mu = jnp.mean(x, axis=-1, keepdims=True)
    xc = x - mu
    var = jnp.mean(xc * xc, axis=-1, keepdims=True)
    return xc * lax.rsqrt(var + LN_EPS)


def _dot(a, b):
    return jnp.dot(a, b, preferred_element_type=F32)


def _mod_body(c_ref, w_ref, b_ref, o_ref):
    c = c_ref[...]
    sc = (c * jax.nn.sigmoid(c)).astype(BF16)
    o_ref[0] = _dot(sc, w_ref[0].astype(BF16)) + b_ref[0]


def _modulation(c2, w_mod, b_mod):
    nl, d, n = w_mod.shape
    tn = 1024
    return pl.pallas_call(
        _mod_body,
        grid=(nl, n // tn),
        in_specs=[pl.BlockSpec((8, d), lambda l, j: (0, 0)),
                  pl.BlockSpec((1, d, tn), lambda l, j: (l, 0, j)),
                  pl.BlockSpec((1, 1, tn), lambda l, j: (l, 0, j))],
        out_specs=pl.BlockSpec((1, 8, tn), lambda l, j: (l, 0, j)),
        out_shape=jax.ShapeDtypeStruct((nl, 8, n), F32),
        compiler_params=_cparams("arbitrary", "arbitrary"),
    )(c2, w_mod, b_mod.reshape(nl, 1, n))


def _inproj_body(x_ref, mod_ref, w_ref, b_ref, cos_ref, sa_ref, sb_ref,
                 f_ref, u_ref, q_ref, k_ref, v_ref, *, rope):
    h = _ln(x_ref[...]) * (1.0 + mod_ref[1:2, :]) + mod_ref[0:1, :]
    hb = h.astype(BF16)

    def proj(lo, hi):
        return _dot(hb, w_ref[:, lo:hi]) + b_ref[:, lo:hi]

    f_ref[...] = proj(0, OFF_CONV).astype(BF16)
    a = proj(OFF_CONV, OFF_CONV + D_CONV)
    g = proj(OFF_CONV + D_CONV, OFF_Q)
    u_ref[...] = (a * jax.nn.sigmoid(g)).astype(BF16)
    v_ref[...] = proj(OFF_V, D_IN_PROJ).astype(BF16)
    q = proj(OFF_Q, OFF_K)
    k = proj(OFF_K, OFF_V)
    if rope:
        cos, sa, sb = cos_ref[...], sa_ref[...], sb_ref[...]
    for hd in range(N_ATT_HEADS):
        sl = slice(hd * LANES, (hd + 1) * LANES)
        qh, kh = q[:, sl], k[:, sl]
        if rope:
            qh = qh * cos + pltpu.roll(qh, LANES - 32, 1) * sa + pltpu.roll(qh, 32, 1) * sb
            kh = kh * cos + pltpu.roll(kh, LANES - 32, 1) * sa + pltpu.roll(kh, 32, 1) * sb
        q_ref[:, sl] = (qh * Q_SCALE).astype(BF16)
        k_ref[:, sl] = kh.astype(BF16)


def _inproj(x, mod8, w_bf, b, tables, *, rope, tm):
    s, d = x.shape
    cos, sa, sb = tables
    row = lambda n: pl.BlockSpec((tm, n), lambda i: (i, 0))
    outs = [(D_FOURIER, BF16), (D_CONV, BF16), (QK_COLS, BF16), (QK_COLS, BF16), (D_ATT, BF16)]
    return pl.pallas_call(
        functools.partial(_inproj_body, rope=rope),
        grid=(s // tm,),
        in_specs=[row(d), _resident((8, d)), _resident((d, D_IN_PROJ)), _resident((1, D_IN_PROJ)),
                  row(LANES), row(LANES), row(LANES)],
        out_specs=[row(n) for n, _ in outs],
        out_shape=[jax.ShapeDtypeStruct((s, n), dt) for n, dt in outs],
        compiler_params=_cparams("arbitrary"),
    )(x, mod8, w_bf, b.reshape(1, -1), cos, sa, sb)


def _rope_tables(n_tokens):
    rows = n_tokens // GRID_W
    row = jnp.repeat(jnp.arange(rows), GRID_W).astype(F32)
    col = jnp.tile(jnp.arange(GRID_W), rows).astype(F32)
    inv = 1.0 / (ROPE_BASE ** (jnp.arange(ROPE_PAIRS_PER_AXIS, dtype=F32) / ROPE_PAIRS_PER_AXIS))
    ang = jnp.concatenate([row[:, None] * inv, col[:, None] * inv], -1)
    cos, sin = jnp.cos(ang), jnp.sin(ang)
    zero = jnp.zeros_like(sin)
    cos_t = jnp.tile(cos, (1, 4))
    sa = jnp.tile(jnp.concatenate([-sin, zero], -1), (1, 2))
    sb = jnp.tile(jnp.concatenate([zero, sin], -1), (1, 2))
    return cos_t, sa, sb


def _dft_mats(n):
    k = np.arange(n)
    ang = 2.0 * np.pi * ((k[:, None] * k[None, :]) % n) / n
    return np.cos(ang), np.sin(ang)


def _channel_dft(scale):
    c, s = _dft_mats(FOURIER_GROUP)
    eye = np.eye(N_FOURIER_GROUPS)
    return (jnp.asarray(np.kron(eye, c) * scale, BF16), jnp.asarray(np.kron(eye, s) * scale, BF16))


def _fourier1_body(m_ref, x_ref, o_ref):
    o_ref[...] = _dot(m_ref[...], x_ref[...]).astype(BF16)


def _fourier2_body(hr_ref, hi_ref, tc_ref, ts_ref, m2_ref, cc_ref, sc_ref, wf_ref, o_ref, *, n2):
    tc_all, ts_all = tc_ref[0], ts_ref[0]
    for j in range(8):
        hr = hr_ref[j * n2:(j + 1) * n2, :].astype(F32)
        hi = hi_ref[j * n2:(j + 1) * n2, :].astype(F32)
        tc, ts = tc_all[:, j:j + 1], ts_all[:, j:j + 1]
        gr = hr * tc + hi * ts
        gi = hi * tc - hr * ts
        g = jnp.concatenate([gr, gi], axis=0).astype(BF16)
        y = _dot(m2_ref[...], g)
        z = _dot(y[:n2].astype(BF16), cc_ref[...]) + _dot(y[n2:].astype(BF16), sc_ref[...])
        o_ref[:, j, :] = _dot(z.astype(BF16), wf_ref[...]).astype(BF16)


def _fourier_fourstep(f, wf_bf):
    l, ch = f.shape
    n2 = l // DFT1
    c1, s1 = _dft_mats(DFT1)
    m1 = jnp.asarray(np.concatenate([c1, -s1], 0), BF16)
    tn = min(8192, n2 * ch)
    h = pl.pallas_call(
        _fourier1_body,
        grid=(n2 * ch // tn,),
        in_specs=[_resident((2 * DFT1, DFT1)), pl.BlockSpec((DFT1, tn), lambda j: (0, j))],
        out_specs=pl.BlockSpec((2 * DFT1, tn), lambda j: (0, j)),
        out_shape=jax.ShapeDtypeStruct((2 * DFT1, n2 * ch), BF16),
        compiler_params=_cparams("arbitrary"),
    )(m1, f.reshape(DFT1, n2 * ch))
    h = h.reshape(2 * DFT1 * n2, ch)

    k1 = np.arange(DFT1)
    nn = np.arange(n2)
    ang = 2.0 * np.pi * ((nn[:, None] * k1[None, :]) % l) / l
    tc = jnp.asarray(np.cos(ang).reshape(n2, DFT1 // 8, 8).transpose(1, 0, 2), F32)
    ts = jnp.asarray(np.sin(ang).reshape(n2, DFT1 // 8, 8).transpose(1, 0, 2), F32)
    c2, s2 = _dft_mats(n2)
    m2 = jnp.asarray(np.block([[c2, s2], [-s2, c2]]), BF16)
    cc, sc = _channel_dft(1.0 / math.sqrt(l * FOURIER_GROUP))
    nb = DFT1 // 8
    out = pl.pallas_call(
        functools.partial(_fourier2_body, n2=n2),
        grid=(nb,),
        in_specs=[pl.BlockSpec((8 * n2, ch), lambda b: (b, 0)),
                  pl.BlockSpec((8 * n2, ch), lambda b: (b + nb, 0)),
                  pl.BlockSpec((1, n2, 8), lambda b: (b, 0, 0)),
                  pl.BlockSpec((1, n2, 8), lambda b: (b, 0, 0)),
                  _resident((2 * n2, 2 * n2)), _resident((ch, ch)), _resident((ch, ch)), _resident((ch, ch))],
        out_specs=pl.BlockSpec((n2, 8, ch), lambda b: (0, b, 0)),
        out_shape=jax.ShapeDtypeStruct((n2, DFT1, ch), BF16),
        compiler_params=_cparams("arbitrary"),
    )(h, h, tc, ts, m2, cc, sc, wf_bf)
    return out.reshape(l, ch)


def _fourier_dense_body(f_ref, cc_ref, sc_ref, mp_ref, wf_ref, o_ref):
    f = f_ref[...]
    a = jnp.concatenate([_dot(f, cc_ref[...]), _dot(f, sc_ref[...])], axis=0).astype(BF16)
    z = _dot(mp_ref[...], a)
    o_ref[...] = _dot(z.astype(BF16), wf_ref[...]).astype(BF16)


def _fourier_dense(f, wf_bf):
    l, ch = f.shape
    cl, sl = _dft_mats(l)
    mp = jnp.asarray(np.concatenate([cl, -sl], 1), BF16)
    cc, sc = _channel_dft(1.0 / math.sqrt(l * FOURIER_GROUP))
    return pl.pallas_call(
        _fourier_dense_body,
        grid=(1,),
        in_specs=[_resident((l, ch)), _resident((ch, ch)), _resident((ch, ch)), _resident((l, 2 * l)),
                  _resident((ch, ch))],
        out_specs=pl.BlockSpec((l, ch), lambda i: (0, 0)),
        out_shape=jax.ShapeDtypeStruct((l, ch), BF16),
        compiler_params=_cparams("arbitrary"),
    )(f, cc, sc, mp, wf_bf)


CONV_HALO = 16
CONV_ROWS = 64


def _conv_body(prev_ref, cur_ref, next_ref, cw_ref, vec_ref, wpw_ref, o_ref, ubuf, *, tm, nt):
    t = pl.program_id(0)
    ubuf[0:CONV_HALO, :] = jnp.where(t > 0, prev_ref[...].astype(F32), 0.0)
    ubuf[CONV_HALO:CONV_HALO + tm, :] = cur_ref[...].astype(F32)
    ubuf[CONV_HALO + tm:2 * CONV_HALO + tm, :] = jnp.where(t < nt - 1, next_ref[...].astype(F32), 0.0)
    cb, lg, lb = vec_ref[0:1, :], vec_ref[1:2, :], vec_ref[2:3, :]
    first = CONV_HALO - CONV_PAD
    for c in range(tm // CONV_ROWS):
        base = c * CONV_ROWS + first
        acc = cw_ref[0:1, :] * ubuf[base:base + CONV_ROWS, :]
        for j in range(1, CONV_WIDTH):
            acc = acc + cw_ref[j:j + 1, :] * ubuf[base + j:base + j + CONV_ROWS, :]
        y = _ln(acc + cb) * lg + lb
        y = y * jax.nn.sigmoid(y)
        o_ref[c * CONV_ROWS:(c + 1) * CONV_ROWS, :] = _dot(y.astype(BF16), wpw_ref[...]).astype(BF16)


def _conv(u, conv_w, conv_b, ln_g, ln_b, wpw_bf, *, tm):
    s, ch = u.shape
    nt = s // tm
    hb = tm // CONV_HALO
    nh = s // CONV_HALO
    cw = jnp.zeros((32, ch), F32).at[:CONV_WIDTH].set(conv_w)
    vec = jnp.zeros((8, ch), F32).at[0].set(conv_b).at[1].set(ln_g).at[2].set(ln_b)
    return pl.pallas_call(
        functools.partial(_conv_body, tm=tm, nt=nt),
        grid=(nt,),
        in_specs=[pl.BlockSpec((CONV_HALO, ch), lambda t: (jnp.maximum(t * hb - 1, 0), 0)),
                  pl.BlockSpec((tm, ch), lambda t: (t, 0)),
                  pl.BlockSpec((CONV_HALO, ch), lambda t: (jnp.minimum((t + 1) * hb, nh - 1), 0)),
                  _resident((32, ch)), _resident((8, ch)), _resident((ch, ch))],
        out_specs=pl.BlockSpec((tm, ch), lambda t: (t, 0)),
        out_shape=jax.ShapeDtypeStruct((s, ch), BF16),
        scratch_shapes=[pltpu.VMEM((tm + 2 * CONV_HALO, ch), F32)],
        compiler_params=_cparams("arbitrary"),
    )(u, u, u, cw, vec, wpw_bf)


def _attn_body(lam_ref, g_ref, qt_ref, k_ref, vt_ref, o_ref, acc1, acc2, sa1, sa2, sb1, sb2,
               *, nk, tq, lam_init):
    qt = qt_ref[0]
    comp = lax.broadcasted_iota(jnp.int32, qt.shape, 0) < ATT_QK_DIM
    q1 = jnp.where(comp, qt, jnp.zeros_like(qt))
    q2 = jnp.where(comp, jnp.zeros_like(qt), qt)
    acc1[...] = jnp.zeros_like(acc1)
    acc2[...] = jnp.zeros_like(acc2)

    def scores(i, s1_ref, s2_ref):
        kb = k_ref[0, i]
        s1_ref[...] = _dot(kb, q1)
        s2_ref[...] = _dot(kb, q2)

    def online(vt, s_ref, m, l, acc):
        s = s_ref[...]
        mn = jnp.maximum(m, jnp.max(s, axis=0, keepdims=True))
        p = jnp.exp2(s - mn)
        alpha = jnp.exp2(m - mn)
        acc[...] = alpha * acc[...] + _dot(vt, p.astype(BF16))
        return mn, alpha * l + jnp.sum(p, axis=0, keepdims=True)

    def consume(i, s1_ref, s2_ref, carry):
        m1, l1, m2, l2 = carry
        vt = vt_ref[0, i]
        m1, l1 = online(vt, s1_ref, m1, l1, acc1)
        m2, l2 = online(vt, s2_ref, m2, l2, acc2)
        return m1, l1, m2, l2

    def pair(j, carry):
        i0 = 2 * j
        scores(i0 + 1, sb1, sb2)
        carry = consume(i0, sa1, sa2, carry)
        scores(i0 + 2, sa1, sa2)
        return consume(i0 + 1, sb1, sb2, carry)

    neg = jnp.full((1, tq), -jnp.inf, F32)
    zero = jnp.zeros((1, tq), F32)
    scores(0, sa1, sa2)
    carry = lax.fori_loop(0, nk // 2, pair, (neg, zero, neg, zero))
    _, l1, _, l2 = consume(nk - 1, sa1, sa2, carry)

    lp = lam_ref[...]
    lam = (jnp.exp(jnp.sum(lp[0:1] * lp[1:2], axis=-1, keepdims=True))
           - jnp.exp(jnp.sum(lp[2:3] * lp[3:4], axis=-1, keepdims=True)) + lam_init)
    o = acc1[...] / l1 - lam * (acc2[...] / l2)
    ms = jnp.mean(o * o, axis=0, keepdims=True)
    o = o * lax.rsqrt(ms + LN_EPS) * g_ref[...] * (1.0 - lam_init)
    o_ref[...] = o.T.astype(BF16)


def _attention(q, k_all, v_all, lam_p, g, *, lam_init, tq, tk):
    s = q.shape[0]
    lk = k_all.shape[0]
    nk = lk // tk
    assert nk % 2 == 1, "the score double-buffering consumes key chunks in pairs plus one"
    qt = q.reshape(s, N_ATT_HEADS, LANES).transpose(1, 2, 0)
    kk = k_all.reshape(nk, tk, N_ATT_HEADS, LANES).transpose(2, 0, 1, 3)
    vt = v_all.reshape(nk, tk, N_ATT_HEADS, ATT_V_DIM).transpose(2, 0, 3, 1)
    return pl.pallas_call(
        functools.partial(_attn_body, nk=nk, tq=tq, lam_init=lam_init),
        grid=(N_ATT_HEADS, s // tq),
        in_specs=[pl.BlockSpec((8, LANES), lambda h, i: (0, 0)),
                  pl.BlockSpec((ATT_V_DIM, 1), lambda h, i: (0, 0)),
                  pl.BlockSpec((1, LANES, tq), lambda h, i: (h, 0, i)),
                  pl.BlockSpec((1, nk, tk, LANES), lambda h, i: (h, 0, 0, 0)),
                  pl.BlockSpec((1, nk, ATT_V_DIM, tk), lambda h, i: (h, 0, 0, 0))],
        out_specs=pl.BlockSpec((tq, ATT_V_DIM), lambda h, i: (i, h)),
        out_shape=jax.ShapeDtypeStruct((s, D_ATT), BF16),
        scratch_shapes=[pltpu.VMEM((ATT_V_DIM, tq), F32), pltpu.VMEM((ATT_V_DIM, tq), F32)]
        + [pltpu.VMEM((tk, tq), F32) for _ in range(4)],
        compiler_params=_cparams("arbitrary", "arbitrary"),
    )(lam_p, g.reshape(ATT_V_DIM, 1), qt, kk, vt)


def _route(logits):
    lane = lax.broadcasted_iota(jnp.int32, logits.shape, 1)
    lane_f = lane.astype(F32)
    is_g = lane < N_GROUPS
    gl = jnp.where(is_g, logits, NEG_BIG)
    gmax = jnp.max(gl, axis=-1, keepdims=True)
    gidx = jnp.min(jnp.where(gl == gmax, lane_f, float(ROUTER_LANES)), axis=-1, keepdims=True)
    wg = 1.0 / jnp.sum(jnp.where(is_g, jnp.exp(gl - gmax), 0.0), axis=-1, keepdims=True)
    grp = ((lane - N_GROUPS) // EXPERTS_PER_GROUP).astype(F32)
    valid = (lane >= N_GROUPS) & (lane < N_GROUPS + N_EXPERTS) & (grp == gidx)
    el = jnp.where(valid, logits, NEG_BIG)
    m1 = jnp.max(el, axis=-1, keepdims=True)
    i1 = jnp.min(jnp.where(el == m1, lane_f, float(ROUTER_LANES)), axis=-1, keepdims=True)
    el2 = jnp.where(lane_f == i1, NEG_BIG, el)
    m2 = jnp.max(el2, axis=-1, keepdims=True)
    i2 = jnp.min(jnp.where(el2 == m2, lane_f, float(ROUTER_LANES)), axis=-1, keepdims=True)
    t = jnp.exp(m2 - m1)
    w1 = wg / (1.0 + t)
    w2 = wg * t / (1.0 + t)
    e1 = (i1 - N_GROUPS).astype(jnp.int32)
    e2 = (i2 - N_GROUPS).astype(jnp.int32)
    eidx = jnp.where(lane == 0, e1, jnp.where(lane == 1, e2, 0))
    wts = jnp.where(lane == 0, w1, jnp.where(lane == 1, w2, 0.0))
    return eidx, wts


def _outproj_body(x_ref, f_ref, c_ref, a_ref, w_ref, mod_ref, ln_ref, wr_ref, br_ref,
                  xo_ref, h2_ref, ei_ref, wt_ref):
    mix = (_dot(f_ref[...], w_ref[0:D_FOURIER, :])
           + _dot(c_ref[...], w_ref[D_FOURIER:D_FOURIER + D_CONV, :])
           + _dot(a_ref[...], w_ref[D_FOURIER + D_CONV:D_MIX, :]))
    r = ALPHA * x_ref[...] + mod_ref[2:3, :] * mix
    xn = _ln(r) * ln_ref[0:1, :] + ln_ref[1:2, :]
    xo_ref[...] = xn
    h2 = _ln(xn) * (1.0 + mod_ref[4:5, :]) + mod_ref[3:4, :]
    h2_ref[...] = h2
    logits = _dot(h2.astype(BF16), wr_ref[...]) + br_ref[...]
    eidx, wts = _route(logits)
    ei_ref[...] = eidx
    wt_ref[...] = wts


def _outproj(x, four, conv, att, w_out_bf, mod8, ln8, wr_bf, br, *, tm):
    s, d = x.shape
    row = lambda n: pl.BlockSpec((tm, n), lambda i: (i, 0))
    return pl.pallas_call(
        _outproj_body,
        grid=(s // tm,),
        in_specs=[row(d), row(D_FOURIER), row(D_CONV), row(D_ATT), _resident((D_MIX, d)), _resident((8, d)),
                  _resident((8, d)), _resident((d, ROUTER_LANES)), _resident((1, ROUTER_LANES))],
        out_specs=[row(d), row(d), row(ROUTER_LANES), row(ROUTER_LANES)],
        out_shape=[jax.ShapeDtypeStruct((s, d), F32), jax.ShapeDtypeStruct((s, d), F32),
                   jax.ShapeDtypeStruct((s, ROUTER_LANES), jnp.int32),
                   jax.ShapeDtypeStruct((s, ROUTER_LANES), F32)],
        compiler_params=_cparams("arbitrary"),
    )(x, four, conv, att, w_out_bf, mod8, ln8, wr_bf, br)


TOK_TILE = 256


def _moe_plan(eidx2, tb):
    n = eidx2.shape[0]
    nb = -(-(n * TOP_K) // tb) + N_EXPERTS
    onehot = (eidx2[:, :, None] == jnp.arange(N_EXPERTS, dtype=jnp.int32)).astype(jnp.int32)
    per_tok = onehot.sum(axis=1)
    incl = jnp.cumsum(per_tok, axis=0)
    counts = incl[-1]
    padded = (counts + tb - 1) // tb * tb
    pad_end = jnp.cumsum(padded)
    rank_base = (incl - per_tok) + (pad_end - padded)[None, :]
    dest = jnp.sum(onehot * rank_base[:, None, :], axis=-1).astype(jnp.int32)
    block_row = jnp.arange(nb, dtype=jnp.int32) * tb
    block_e = jnp.minimum(jnp.sum(pad_end[None, :] <= block_row[:, None], axis=1), N_EXPERTS - 1).astype(jnp.int32)
    first =jnp.concatenate([jnp.ones((1,), jnp.int32), (block_e[1:] != block_e[:-1]).astype(jnp.int32)])
    nused = (pad_end[-1] // tb).astype(jnp.int32).reshape(1)
    zrow = jnp.maximum(pad_end - tb, 0).astype(jnp.int32)
    zflag = (counts > 0).astype(jnp.int32)
    return dest, block_e, first, nused, zrow, zflag, nb


def _dispatch_body(zrow_ref, zflag_ref, nused_ref, dest_ref, h_ref, xs_hbm, zbuf, stage, sem, zsem,
                   *, tb, nt, nb):
    i = pl.program_id(0)

    def zero_copy(e):
        return pltpu.make_async_copy(zbuf, xs_hbm.at[pl.ds(pl.multiple_of(zrow_ref[e], tb), tb)], zsem)

    @pl.when(i == 0)
    def _():
        zbuf[...] = jnp.zeros_like(zbuf)
        for e in range(N_EXPERTS):
            @pl.when(zflag_ref[e] == 1)
            def _():
                zero_copy(e).start()
        for e in range(N_EXPERTS):
            @pl.when(zflag_ref[e] == 1)
            def _():
                zero_copy(e).wait()

        def tail_copy(j):
            return pltpu.make_async_copy(zbuf, xs_hbm.at[pl.ds(pl.multiple_of(j * tb, tb), tb)], zsem)

        def start_tail(j, c):
            tail_copy(j).start()
            return c

        def wait_tail(j, c):
            tail_copy(j).wait()
            return c
        lax.fori_loop(nused_ref[0], nb, start_tail, 0)
        lax.fori_loop(nused_ref[0], nb, wait_tail, 0)

    slot = i % 2

    def wait_tile(sl):
        for _ in range(TOP_K):
            pltpu.make_async_copy(stage.at[sl], xs_hbm.at[pl.ds(0, TOK_TILE)], sem.at[sl]).wait()

    @pl.when(i >= 2)
    def _():
        wait_tile(slot)

    stage[slot] = h_ref[...]

    def body(r, c):
        src = stage.at[slot, pl.ds(r, 1)]
        for k in range(TOP_K):
            row = dest_ref[0, 0, TOP_K * r + k]
            pltpu.make_async_copy(src, xs_hbm.at[pl.ds(row, 1)], sem.at[slot]).start()
        return c
    lax.fori_loop(0, TOK_TILE, body, 0)

    @pl.when(i == nt - 1)
    def _():
        wait_tile(slot)
        if nt >= 2:
            wait_tile(1 - slot)


def _dispatch(h2, dest, zrow, zflag, nused, *, tb, nb):
    n, d = h2.shape
    nt = n // TOK_TILE
    grid_spec = pltpu.PrefetchScalarGridSpec(
        num_scalar_prefetch=3,
        grid=(nt,),
        in_specs=[pl.BlockSpec((1, 1, TOP_K * TOK_TILE), lambda i, *_: (i, 0, 0), memory_space=pltpu.SMEM),
                  pl.BlockSpec((TOK_TILE, d), lambda i, *_: (i, 0))],
        out_specs=pl.BlockSpec(memory_space=pl.ANY),
        scratch_shapes=[pltpu.VMEM((tb, d), F32), pltpu.VMEM((2, TOK_TILE, d), F32),
                        pltpu.SemaphoreType.DMA((2,)), pltpu.SemaphoreType.DMA(())],
    )
    return pl.pallas_call(
        functools.partial(_dispatch_body, tb=tb, nt=nt, nb=nb),
        grid_spec=grid_spec,
        out_shape=jax.ShapeDtypeStruct((nb * tb, d), F32),
        compiler_params=_cparams("arbitrary"),
    )(zrow, zflag, nused, dest.reshape(nt, 1, TOP_K * TOK_TILE), h2)


def _experts_body(be_ref, first_ref, nused_ref, x_ref, wg_ref, wu_ref, wd_ref, o_ref, wgb, wub, wdb):
    del be_ref
    i = pl.program_id(0)

    @pl.when(i < nused_ref[0])
    def _():
        @pl.when(first_ref[i] == 1)
        def _():
            wgb[...] = wg_ref[0].astype(BF16)
            wub[...] = wu_ref[0].astype(BF16)
            wdb[...] = wd_ref[0].astype(BF16)

        x = x_ref[...].astype(BF16)
        g = _dot(x, wgb[...])
        u = _dot(x, wub[...])
        mid = (g * jax.nn.sigmoid(g) * u).astype(BF16)
        o_ref[...] = _dot(mid, wdb[...])

    @pl.when(i >= nused_ref[0])
    def _():
        o_ref[...] = jnp.zeros_like(o_ref)


def _experts(xs, block_e, first, nused, w_gate, w_up, w_down, *, tb, nb):
    d = xs.shape[1]
    wspec = lambda shape: pl.BlockSpec(shape, lambda i, be, first, nused: (be[i], 0, 0))
    xspec = pl.BlockSpec((tb, d), lambda i, be, first, nused: (jnp.minimum(i, nused[0] - 1), 0))
    grid_spec = pltpu.PrefetchScalarGridSpec(
        num_scalar_prefetch=3,
        grid=(nb,),
        in_specs=[xspec, wspec((1, d, D_EXPERT)), wspec((1, d, D_EXPERT)), wspec((1, D_EXPERT, d))],
        out_specs=pl.BlockSpec((tb, d), lambda i, *_: (i, 0)),
        scratch_shapes=[pltpu.VMEM((d, D_EXPERT), BF16), pltpu.VMEM((d, D_EXPERT), BF16),
                        pltpu.VMEM((D_EXPERT, d), BF16)],
    )
    return pl.pallas_call(
        _experts_body,
        grid_spec=grid_spec,
        out_shape=jax.ShapeDtypeStruct((nb * tb, d), F32),
        compiler_params=_cparams("arbitrary"),
    )(block_e, first, nused, xs, w_gate, w_up, w_down)


def _combine_body(dcur_ref, dnext_ref, x_ref, wt_ref, mod_ref, ln_ref, ys_hbm, o_ref, ybuf, sem, *, nt):
    i = pl.program_id(0)

    def issue(d_ref, slot):
        def body(r, c):
            for k in range(TOP_K):
                row = d_ref[0, 0, TOP_K * r + k]
                pltpu.make_async_copy(ys_hbm.at[pl.ds(row, 1)], ybuf.at[slot, k, pl.ds(r, 1)], sem.at[slot]).start()
            return c
        lax.fori_loop(0, TOK_TILE, body, 0)

    @pl.when(i == 0)
    def _():
        issue(dcur_ref, 0)

    @pl.when(i + 1 < nt)
    def _():
        issue(dnext_ref, (i + 1) % 2)

    slot = i % 2
    for k in range(TOP_K):
        pltpu.make_async_copy(ys_hbm.at[pl.ds(0, TOK_TILE)], ybuf.at[slot, k], sem.at[slot]).wait()
    w = wt_ref[...]
    y = w[:, 0:1] * ybuf[slot, 0] + w[:, 1:2] * ybuf[slot, 1]
    r = ALPHA * x_ref[...] + mod_ref[5:6, :] * y
    o_ref[...] = _ln(r) * ln_ref[0:1, :] + ln_ref[1:2, :]


def _combine(x, ys, dest, wt, mod8, ln8, *, row0):
    s, d = x.shape
    nt = s // TOK_TILE
    t0 = row0 // TOK_TILE
    dest3 = dest.reshape(-1, 1, TOP_K * TOK_TILE)
    dspec = lambda off: pl.BlockSpec((1, 1, TOP_K * TOK_TILE),
                                     lambda i: (t0 + jnp.minimum(i + off, nt - 1), 0, 0), memory_space=pltpu.SMEM)
    return pl.pallas_call(
        functools.partial(_combine_body, nt=nt),
        grid=(nt,),
        in_specs=[dspec(0), dspec(1),
                  pl.BlockSpec((TOK_TILE, d), lambda i: (i, 0)),
                  pl.BlockSpec((TOK_TILE, ROUTER_LANES), lambda i: (i, 0)),
                  _resident((8, d)), _resident((8, d)),
                  pl.BlockSpec(memory_space=pl.ANY)],
        out_specs=pl.BlockSpec((TOK_TILE, d), lambda i: (i, 0)),
        out_shape=jax.ShapeDtypeStruct((s, d), F32),
        scratch_shapes=[pltpu.VMEM((2, TOP_K, TOK_TILE, d), F32), pltpu.SemaphoreType.DMA((2,))],
        compiler_params=_cparams("arbitrary"),
    )(dest3, dest3, x, wt, mod8, ln8, ys)


def _moe(h2, eidx2, w_gate, w_up, w_down):
    tb = MOE_ROWS
    dest, block_e, first, nused, zrow, zflag, nb = _moe_plan(eidx2, tb)
    xs = _dispatch(h2, dest, zrow, zflag, nused, tb=tb, nb=nb)
    ys = _experts(xs, block_e, first, nused, w_gate, w_up, w_down, tb=tb, nb=nb)
    return ys, dest


def _pick(n, prefs):
    for t in prefs:
        if n % t == 0:
            return t
    raise ValueError(f"no tile of {prefs} divides {n}")


def _rows8(*vecs):
    d = vecs[0].shape[-1]
    out = jnp.zeros((8, d), F32)
    for j, v in enumerate(vecs):
        out = out.at[j].set(v)
    return out


def kernel(x, c, ctx, c_ctx, w_mod, b_mod, w_in, b_in, w_fourier, conv_w, conv_b, conv_ln_g, conv_ln_b, w_pw,
           lam_q1, lam_k1, lam_q2, lam_k2, subln_g, w_out, ln_a_g, ln_a_b, w_rg, b_rg, w_re, b_re,
           w_gate, w_up, w_down, ln_f_g, ln_f_b):
    b, s, d = x.shape
    nc = ctx.shape[1]
    assert b == 1 and d == D_MODEL and c.shape[0] == 1 and ctx.shape[0] == 1
    assert s % (DFT1 * 8) == 0 and nc % TOK_TILE == 0
    xl, xc = x[0], ctx[0]
    depth = w_mod.shape[0]

    c2 = jnp.zeros((8, d), F32).at[0].set(c[0]).at[1].set(c_ctx)
    mod_all = _modulation(c2, w_mod, b_mod)
    tables_l = _rope_tables(s)
    tables_c = tuple(jnp.zeros((nc, LANES), F32) for _ in range(3))
    tm_l = _pick(s, (512, 256))
    tm_c = _pick(nc, (256,))
    tq = _pick(s, (512, 256))
    tk = _pick(s + nc, (1280, 256))

    for i in range(depth):
        last = i == depth - 1
        lam_init = 0.8 - 0.6 * math.exp(-0.3 * i)
        mod_l = _rows8(*jnp.split(mod_all[i, 0], N_MOD))
        mod_c = _rows8(*jnp.split(mod_all[i, 1], N_MOD))
        w_in_bf = w_in[i].astype(BF16)
        wf_bf = w_fourier[i].astype(BF16)
        wpw_bf = w_pw[i].astype(BF16)
        w_out_bf = w_out[i].astype(BF16)
        wr = jnp.zeros((d, ROUTER_LANES), F32).at[:, :N_GROUPS].set(w_rg[i])
        wr_bf = wr.at[:, N_GROUPS:N_GROUPS + N_EXPERTS].set(w_re[i]).astype(BF16)
        br = jnp.zeros((1, ROUTER_LANES), F32).at[0, :N_GROUPS].set(b_rg[i])
        br = br.at[0, N_GROUPS:N_GROUPS + N_EXPERTS].set(b_re[i])
        lam_p = jnp.zeros((8, LANES), F32).at[0, :ATT_QK_DIM].set(lam_q1[i]).at[1, :ATT_QK_DIM].set(lam_k1[i])
        lam_p = lam_p.at[2, :ATT_QK_DIM].set(lam_q2[i]).at[3, :ATT_QK_DIM].set(lam_k2[i])
        ln_a = _rows8(ln_a_g[i], ln_a_b[i])
        ln_f = _rows8(ln_f_g[i], ln_f_b[i])
        conv_args = (conv_w[i], conv_b[i], conv_ln_g[i], conv_ln_b[i], wpw_bf)

        f_l, u_l, q_l, k_l, v_l = _inproj(xl, mod_l, w_in_bf, b_in[i], tables_l, rope=True, tm=tm_l)
        f_c, u_c, q_c, k_c, v_c = _inproj(xc, mod_c, w_in_bf, b_in[i], tables_c, rope=False, tm=tm_c)
        k_all = jnp.concatenate([k_c, k_l], axis=0)
        v_all = jnp.concatenate([v_c, v_l], axis=0)
        att_l = _attention(q_l, k_all, v_all, lam_p, subln_g[i], lam_init=lam_init, tq=tq, tk=tk)
        four_l = _fourier_fourstep(f_l, wf_bf)
        conv_l = _conv(u_l, *conv_args, tm=tm_l)
        xl, h2_l, ei_l, wt_l = _outproj(xl, four_l, conv_l, att_l, w_out_bf, mod_l, ln_a, wr_bf, br, tm=tm_l)
        if last:
            ys, dest = _moe(h2_l, ei_l[:, :TOP_K], w_gate[i], w_up[i], w_down[i])
            xl = _combine(xl, ys, dest, wt_l, mod_l, ln_f, row0=0)
        else:
            att_c = _attention(q_c, k_c, v_c, lam_p, subln_g[i], lam_init=lam_init, tq=tm_c, tk=tm_c)
            four_c = _fourier_dense(f_c, wf_bf)
            conv_c = _conv(u_c, *conv_args, tm=tm_c)
            xc, h2_c, ei_c, wt_c = _outproj(xc, four_c, conv_c, att_c, w_out_bf, mod_c, ln_a, wr_bf, br, tm=tm_c)
            h2 = jnp.concatenate([h2_c, h2_l], axis=0)
            ei = jnp.concatenate([ei_c[:, :TOP_K], ei_l[:, :TOP_K]], axis=0)
            ys, dest = _moe(h2, ei, w_gate[i], w_up[i], w_down[i])
            xc = _combine(xc, ys, dest, wt_c, mod_c, ln_f, row0=0)
            xl = _combine(xl, ys, dest, wt_l, mod_l, ln_f, row0=nc)
    return xl[None]
```

```python
import functools
import math

import numpy as np
import jax
import jax.numpy as jnp
from jax import lax
from jax.experimental import pallas as pl
from jax.experimental.pallas import tpu as pltpu

F32 = jnp.float32
BF16 = jnp.bfloat16

D_MODEL = 2048
DEPTH = 2
GRID_W = 64
D_FOURIER = 512
N_FOURIER_GROUPS = 4
FOURIER_GROUP = 128
D_CONV = 512
CONV_WIDTH = 31
CONV_PAD = CONV_WIDTH // 2
N_ATT_HEADS = 8
ATT_QK_DIM = 64
ATT_V_DIM = 128
D_ATT = N_ATT_HEADS * ATT_V_DIM
D_MIX = D_FOURIER + D_CONV + D_ATT
QK_COLS = N_ATT_HEADS * 2 * ATT_QK_DIM
ATT_SCALE = 1.0 / math.sqrt(ATT_QK_DIM)
ROPE_BASE = 10000.0
ROPE_PAIRS_PER_AXIS = ATT_QK_DIM // 4
OFF_CONV = D_FOURIER
OFF_Q = OFF_CONV + 2 * D_CONV
OFF_K = OFF_Q + QK_COLS
OFF_V = OFF_K + QK_COLS
D_IN_PROJ = OFF_V + D_ATT
N_GROUPS = 4
EXPERTS_PER_GROUP = 8
N_EXPERTS = N_GROUPS * EXPERTS_PER_GROUP
TOP_K = 2
D_EXPERT = 512
N_MOD = 6
LN_EPS = 1e-6
ALPHA = (2.0 * DEPTH) ** 0.25

LANES = 128
VMEM_LIMIT = 56 * 1024 * 1024
LOG2E = 1.4426950408889634
Q_SCALE = ATT_SCALE * LOG2E
NEG_BIG = -1e30
DFT1 = 128
MOE_ROWS = 256
ROUTER_LANES = 128
ATT_V_ROWS = ATT_V_DIM + 16
ATT_SUB = 256
def _cparams(*sem):
    return pltpu.CompilerParams(dimension_semantics=tuple(sem), vmem_limit_bytes=VMEM_LIMIT)


def _resident(shape):
    nd = len(shape)
    return pl.BlockSpec(shape, lambda *_: (0,) * nd, pipeline_mode=pl.Buffered(1))


def _ln(x):
    mu = jnp.mean(x, axis=-1, keepdims=True)
    xc = x - mu
    var = jnp.mean(xc * xc, axis=-1, keepdims=True)
    return xc * lax.rsqrt(var + LN_EPS)


def _dot(a, b):
    return jnp.dot(a, b, preferred_element_type=F32)


def _mod_body(c_ref, w_ref, b_ref, o_ref):
    c = c_ref[...]
    sc = (c * jax.nn.sigmoid(c)).astype(BF16)
    o_ref[0] = _dot(sc, w_ref[0].astype(BF16)) + b_ref[0]


def _modulation(c2, w_mod, b_mod):
    nl, d, n = w_mod.shape
    tn = 1024
    return pl.pallas_call(
        _mod_body,
        grid=(nl, n // tn),
        in_specs=[pl.BlockSpec((8, d), lambda l, j: (0, 0)),
                  pl.BlockSpec((1, d, tn), lambda l, j: (l, 0, j)),
                  pl.BlockSpec((1, 1, tn), lambda l, j: (l, 0, j))],
        out_specs=pl.BlockSpec((1, 8, tn), lambda l, j: (l, 0, j)),
        out_shape=jax.ShapeDtypeStruct((nl, 8, n), F32),
        compiler_params=_cparams("arbitrary", "arbitrary"),
    )(c2, w_mod, b_mod.reshape(nl, 1, n))


def _inproj_body(x_ref, mod_ref, w_ref, b_ref, cos_ref, sa_ref, sb_ref,
                 f_ref, u_ref, q_ref, k_ref, v_ref, *, rope):
    h = _ln(x_ref[...]) * (1.0 + mod_ref[1:2, :]) + mod_ref[0:1, :]
    hb = h.astype(BF16)

    def proj(lo, hi):
        return _dot(hb, w_ref[:, lo:hi]) + b_ref[:, lo:hi]

    f_ref[...] = proj(0, OFF_CONV).astype(BF16)
    a = proj(OFF_CONV, OFF_CONV + D_CONV)
    g = proj(OFF_CONV + D_CONV, OFF_Q)
    u_ref[...] = (a * jax.nn.sigmoid(g)).astype(BF16)
    v_ref[...] = proj(OFF_V, D_IN_PROJ).astype(BF16)
    q = proj(OFF_Q, OFF_K)
    k = proj(OFF_K, OFF_V)
    if rope:
        cos, sa, sb = cos_ref[...], sa_ref[...], sb_ref[...]
    for hd in range(N_ATT_HEADS):
        sl = slice(hd * LANES, (hd + 1) * LANES)
        qh, kh = q[:, sl], k[:, sl]
        if rope:
            qh = qh * cos + pltpu.roll(qh, LANES - 32, 1) * sa + pltpu.roll(qh, 32, 1) * sb
            kh = kh * cos + pltpu.roll(kh, LANES - 32, 1) * sa + pltpu.roll(kh, 32, 1) * sb
        q_ref[:, sl] = (qh * Q_SCALE).astype(BF16)
        k_ref[:, sl] = kh.astype(BF16)


def _inproj(x, mod8, w_bf, b, tables, *, rope, tm):
    s, d = x.shape
    cos, sa, sb = tables
    row = lambda n: pl.BlockSpec((tm, n), lambda i: (i, 0))
    outs = [(D_FOURIER, BF16), (D_CONV, BF16), (QK_COLS, BF16), (QK_COLS, BF16), (D_ATT, BF16)]
    return pl.pallas_call(
        functools.partial(_inproj_body, rope=rope),
        grid=(s // tm,),
        in_specs=[row(d), _resident((8, d)), _resident((d, D_IN_PROJ)), _resident((1, D_IN_PROJ)),
                  row(LANES), row(LANES), row(LANES)],
        out_specs=[row(n) for n, _ in outs],
        out_shape=[jax.ShapeDtypeStruct((s, n), dt) for n, dt in outs],
        compiler_params=_cparams("arbitrary"),
    )(x, mod8, w_bf, b.reshape(1, -1), cos, sa, sb)


def _rope_tables(n_tokens):
    rows = n_tokens // GRID_W
    row = jnp.repeat(jnp.arange(rows), GRID_W).astype(F32)
    col = jnp.tile(jnp.arange(GRID_W), rows).astype(F32)
    inv = 1.0 / (ROPE_BASE ** (jnp.arange(ROPE_PAIRS_PER_AXIS, dtype=F32) / ROPE_PAIRS_PER_AXIS))
    ang = jnp.concatenate([row[:, None] * inv, col[:, None] * inv], -1)
    cos, sin = jnp.cos(ang), jnp.sin(ang)
    zero = jnp.zeros_like(sin)
    cos_t = jnp.tile(cos, (1, 4))
    sa = jnp.tile(jnp.concatenate([-sin, zero], -1), (1, 2))
    sb = jnp.tile(jnp.concatenate([zero, sin], -1), (1, 2))
    return cos_t, sa, sb


def _dft_mats(n):
    k = np.arange(n)
    ang = 2.0 * np.pi * ((k[:, None] * k[None, :]) % n) / n
    return np.cos(ang), np.sin(ang)


def _channel_dft(scale):
    c, s = _dft_mats(FOURIER_GROUP)
    eye = np.eye(N_FOURIER_GROUPS)
    return (jnp.asarray(np.kron(eye, c) * scale, BF16), jnp.asarray(np.kron(eye, s) * scale, BF16))


def _fourier1_body(m_ref, x_ref, o_ref):
    o_ref[...] = _dot(m_ref[...], x_ref[...]).astype(BF16)


def _fourier2_body(hr_ref, hi_ref, tc_ref, ts_ref, m2_ref, cc_ref, sc_ref, wf_ref, o_ref, *, n2):
    tc_all, ts_all = tc_ref[0], ts_ref[0]
    for j in range(8):
        hr = hr_ref[j * n2:(j + 1) * n2, :].astype(F32)
        hi = hi_ref[j * n2:(j + 1) * n2, :].astype(F32)
        tc, ts = tc_all[:, j:j + 1], ts_all[:, j:j + 1]
        gr = hr * tc + hi * ts
        gi = hi * tc - hr * ts
        g = jnp.concatenate([gr, gi], axis=0).astype(BF16)
        y = _dot(m2_ref[...], g)
        z = _dot(y[:n2].astype(BF16), cc_ref[...]) + _dot(y[n2:].astype(BF16), sc_ref[...])
        o_ref[:, j, :] = _dot(z.astype(BF16), wf_ref[...]).astype(BF16)


def _fourier_fourstep(f, wf_bf):
    l, ch = f.shape
    n2 = l // DFT1
    c1, s1 = _dft_mats(DFT1)
    m1 = jnp.asarray(np.concatenate([c1, -s1], 0), BF16)
    tn = min(8192, n2 * ch)
    h = pl.pallas_call(
        _fourier1_body,
        grid=(n2 * ch // tn,),
        in_specs=[_resident((2 * DFT1, DFT1)), pl.BlockSpec((DFT1, tn), lambda j: (0, j))],
        out_specs=pl.BlockSpec((2 * DFT1, tn), lambda j: (0, j)),
        out_shape=jax.ShapeDtypeStruct((2 * DFT1, n2 * ch), BF16),
        compiler_params=_cparams("arbitrary"),
    )(m1, f.reshape(DFT1, n2 * ch))
    h = h.reshape(2 * DFT1 * n2, ch)

    k1 = np.arange(DFT1)
    nn = np.arange(n2)
    ang = 2.0 * np.pi * ((nn[:, None] * k1[None, :]) % l) / l
    tc = jnp.asarray(np.cos(ang).reshape(n2, DFT1 // 8, 8).transpose(1, 0, 2), F32)
    ts = jnp.asarray(np.sin(ang).reshape(n2, DFT1 // 8, 8).transpose(1, 0, 2), F32)
    c2, s2 = _dft_mats(n2)
    m2 = jnp.asarray(np.block([[c2, s2], [-s2, c2]]), BF16)
    cc, sc = _channel_dft(1.0 / math.sqrt(l * FOURIER_GROUP))
    nb = DFT1 // 8
    out = pl.pallas_call(
        functools.partial(_fourier2_body, n2=n2),
        grid=(nb,),
        in_specs=[pl.BlockSpec((8 * n2, ch), lambda b: (b, 0)),
                  pl.BlockSpec((8 * n2, ch), lambda b: (b + nb, 0)),
                  pl.BlockSpec((1, n2, 8), lambda b: (b, 0, 0)),
                  pl.BlockSpec((1, n2, 8), lambda b: (b, 0, 0)),
                  _resident((2 * n2, 2 * n2)), _resident((ch, ch)), _resident((ch, ch)), _resident((ch, ch))],
        out_specs=pl.BlockSpec((n2, 8, ch), lambda b: (0, b, 0)),
        out_shape=jax.ShapeDtypeStruct((n2, DFT1, ch), BF16),
        compiler_params=_cparams("arbitrary"),
    )(h, h, tc, ts, m2, cc, sc, wf_bf)
    return out.reshape(l, ch)


def _fourier_dense_body(f_ref, cc_ref, sc_ref, mp_ref, wf_ref, o_ref):
    f = f_ref[...]
    a = jnp.concatenate([_dot(f, cc_ref[...]), _dot(f, sc_ref[...])], axis=0).astype(BF16)
    z = _dot(mp_ref[...], a)
    o_ref[...] = _dot(z.astype(BF16), wf_ref[...]).astype(BF16)


def _fourier_dense(f, wf_bf):
    l, ch = f.shape
    cl, sl = _dft_mats(l)
    mp = jnp.asarray(np.concatenate([cl, -sl], 1), BF16)
    cc, sc = _channel_dft(1.0 / math.sqrt(l * FOURIER_GROUP))
    return pl.pallas_call(
        _fourier_dense_body,
        grid=(1,),
        in_specs=[_resident((l, ch)), _resident((ch, ch)), _resident((ch, ch)), _resident((l, 2 * l)),
                  _resident((ch, ch))],
        out_specs=pl.BlockSpec((l, ch), lambda i: (0, 0)),
        out_shape=jax.ShapeDtypeStruct((l, ch), BF16),
        compiler_params=_cparams("arbitrary"),
    )(f, cc, sc, mp, wf_bf)


CONV_HALO = 16
CONV_ROWS = 64


def _conv_body(prev_ref, cur_ref, next_ref, cw_ref, vec_ref, wpw_ref, o_ref, ubuf, *, tm, nt):
    t = pl.program_id(0)
    ubuf[0:CONV_HALO, :] = jnp.where(t > 0, prev_ref[...].astype(F32), 0.0)
    ubuf[CONV_HALO:CONV_HALO + tm, :] = cur_ref[...].astype(F32)
    ubuf[CONV_HALO + tm:2 * CONV_HALO + tm, :] = jnp.where(t < nt - 1, next_ref[...].astype(F32), 0.0)
    cb, lg, lb = vec_ref[0:1, :], vec_ref[1:2, :], vec_ref[2:3, :]
    first = CONV_HALO - CONV_PAD
    for c in range(tm // CONV_ROWS):
        base = c * CONV_ROWS + first
        acc = cw_ref[0:1, :] * ubuf[base:base + CONV_ROWS, :]
        for j in range(1, CONV_WIDTH):
            acc = acc + cw_ref[j:j + 1, :] * ubuf[base + j:base + j + CONV_ROWS, :]
        y = _ln(acc + cb) * lg + lb
        y = y * jax.nn.sigmoid(y)
        o_ref[c * CONV_ROWS:(c + 1) * CONV_ROWS, :] = _dot(y.astype(BF16), wpw_ref[...]).astype(BF16)


def _conv(u, conv_w, conv_b, ln_g, ln_b, wpw_bf, *, tm):
    s, ch = u.shape
    nt = s // tm
    hb = tm // CONV_HALO
    nh = s // CONV_HALO
    cw = jnp.zeros((32, ch), F32).at[:CONV_WIDTH].set(conv_w)
    vec = jnp.zeros((8, ch), F32).at[0].set(conv_b).at[1].set(ln_g).at[2].set(ln_b)
    return pl.pallas_call(
        functools.partial(_conv_body, tm=tm, nt=nt),
        grid=(nt,),
        in_specs=[pl.BlockSpec((CONV_HALO, ch), lambda t: (jnp.maximum(t * hb - 1, 0), 0)),
                  pl.BlockSpec((tm, ch), lambda t: (t, 0)),
                  pl.BlockSpec((CONV_HALO, ch), lambda t: (jnp.minimum((t + 1) * hb, nh - 1), 0)),
                  _resident((32, ch)), _resident((8, ch)), _resident((ch, ch))],
        out_specs=pl.BlockSpec((tm, ch), lambda t: (t, 0)),
        out_shape=jax.ShapeDtypeStruct((s, ch), BF16),
        scratch_shapes=[pltpu.VMEM((tm + 2 * CONV_HALO, ch), F32)],
        compiler_params=_cparams("arbitrary"),
    )(u, u, u, cw, vec, wpw_bf)


def _attn_body(lam_ref, g_ref, qt_ref, k_ref, vt_ref, o_ref, acc1, acc2, s1_buf, s2_buf, *, nk, tq, lam_init):
    qt = qt_ref[0]
    comp = lax.broadcasted_iota(jnp.int32, qt.shape, 0) < ATT_QK_DIM
    q1 = jnp.where(comp, qt, jnp.zeros_like(qt))
    q2 = jnp.where(comp, jnp.zeros_like(qt), qt)
    acc1[...] = jnp.zeros_like(acc1)
    acc2[...] = jnp.zeros_like(acc2)

    tk = k_ref.shape[2]
    bufs, qs, accs = (s1_buf, s2_buf), (q1, q2), (acc1, acc2)

    def chunk_step(i, slot, mx, m, *, consume, produce):
        if consume:
            mn = [jnp.maximum(m[c], mx[c]) for c in range(2)]
        new_mx, pv = [None, None], [None, None]
        for r in range(tk // ATT_SUB):
            rows = slice(r * ATT_SUB, (r + 1) * ATT_SUB)
            if produce:
                kb = k_ref[0, i + 1, rows, :]
                for c in range(2):
                    s = _dot(kb, qs[c])
                    bufs[c][1 - slot, rows, :] = s
                    smax = jnp.max(s, axis=0, keepdims=True)
                    new_mx[c] = smax if new_mx[c] is None else jnp.maximum(new_mx[c], smax)
            if consume:
                vt = vt_ref[0, i, :, rows]
                for c in range(2):
                    p = jnp.exp2((bufs[c][slot, rows, :] - mn[c]).astype(BF16))
                    d = _dot(vt, p)
                    pv[c] = d if pv[c] is None else pv[c] + d
        if consume:
            for c in range(2):
                accs[c][...] = jnp.exp2(m[c] - mn[c]) * accs[c][...] + pv[c]
            m = tuple(mn)
        return tuple(new_mx), m

    def step(i, carry):
        mx, m = carry
        return lax.cond(i % 2 == 0,
                        lambda: chunk_step(i, 0, mx, m, consume=True, produce=True),
                        lambda: chunk_step(i, 1, mx, m, consume=True, produce=True))

    neg = jnp.full((1, tq), -jnp.inf, F32)
    mx0, _ = chunk_step(-1, 1, None, None, consume=False, produce=True)
    mx, m = lax.fori_loop(0, nk - 1, step, (mx0, (neg, neg)))
    chunk_step(nk - 1, (nk - 1) % 2, mx, m, consume=True, produce=False)

    lp = lam_ref[...]
    lam = (jnp.exp(jnp.sum(lp[0:1] * lp[1:2], axis=-1, keepdims=True))
           - jnp.exp(jnp.sum(lp[2:3] * lp[3:4], axis=-1, keepdims=True)) + lam_init)
    dv = ATT_V_DIM
    o = acc1[0:dv, :] / acc1[dv:dv + 1, :] - lam * (acc2[0:dv, :] / acc2[dv:dv + 1, :])
    ms = jnp.mean(o * o, axis=0, keepdims=True)
    o = o * lax.rsqrt(ms + LN_EPS) * g_ref[...] * (1.0 - lam_init)
    o_ref[...] = o.T.astype(BF16)


def _attention(q, k_all, v_all, lam_p, g, *, lam_init, tq, tk):
    s = q.shape[0]
    lk = k_all.shape[0]
    nk = lk // tk
    qt = q.reshape(s, N_ATT_HEADS, LANES).transpose(1, 2, 0)
    kk = k_all.reshape(nk, tk, N_ATT_HEADS, LANES).transpose(2, 0, 1, 3)
    vt = v_all.reshape(nk, tk, N_ATT_HEADS, ATT_V_DIM).transpose(2, 0, 3, 1)
    ones = jnp.zeros((N_ATT_HEADS, nk, ATT_V_ROWS - ATT_V_DIM, tk), BF16).at[:, :, 0, :].set(1.0)
    vt = jnp.concatenate([vt, ones], axis=2)
    return pl.pallas_call(
        functools.partial(_attn_body, nk=nk, tq=tq, lam_init=lam_init),
        grid=(N_ATT_HEADS, s // tq),
        in_specs=[pl.BlockSpec((8, LANES), lambda h, i: (0, 0)),
                  pl.BlockSpec((ATT_V_DIM, 1), lambda h, i: (0, 0)),
                  pl.BlockSpec((1, LANES, tq), lambda h, i: (h, 0, i)),
                  pl.BlockSpec((1, nk, tk, LANES), lambda h, i: (h, 0, 0, 0)),
                  pl.BlockSpec((1, nk, ATT_V_ROWS, tk), lambda h, i: (h, 0, 0, 0))],
        out_specs=pl.BlockSpec((tq, ATT_V_DIM), lambda h, i: (i, h)),
        out_shape=jax.ShapeDtypeStruct((s, D_ATT), BF16),
        scratch_shapes=[pltpu.VMEM((ATT_V_ROWS, tq), F32), pltpu.VMEM((ATT_V_ROWS, tq), F32)]
        + [pltpu.VMEM((2, tk, tq), F32) for _ in range(2)],
        compiler_params=_cparams("arbitrary", "arbitrary"),
    )(lam_p, g.reshape(ATT_V_DIM, 1), qt, kk, vt)


def _route(logits):
    lane = lax.broadcasted_iota(jnp.int32, logits.shape, 1)
    lane_f = lane.astype(F32)
    is_g = lane < N_GROUPS
    gl = jnp.where(is_g, logits, NEG_BIG)
    gmax = jnp.max(gl, axis=-1, keepdims=True)
    gidx = jnp.min(jnp.where(gl == gmax, lane_f, float(ROUTER_LANES)), axis=-1, keepdims=True)
    wg = 1.0 / jnp.sum(jnp.where(is_g, jnp.exp(gl - gmax), 0.0), axis=-1, keepdims=True)
    grp = ((lane - N_GROUPS) // EXPERTS_PER_GROUP).astype(F32)
    valid = (lane >= N_GROUPS) & (lane < N_GROUPS + N_EXPERTS) & (grp == gidx)
    el = jnp.where(valid, logits, NEG_BIG)
    m1 = jnp.max(el, axis=-1, keepdims=True)
    i1 = jnp.min(jnp.where(el == m1, lane_f, float(ROUTER_LANES)), axis=-1, keepdims=True)
    el2 = jnp.where(lane_f == i1, NEG_BIG, el)
    m2 = jnp.max(el2, axis=-1, keepdims=True)
    i2 = jnp.min(jnp.where(el2 == m2, lane_f, float(ROUTER_LANES)), axis=-1, keepdims=True)
    t = jnp.exp(m2 - m1)
    w1 = wg / (1.0 + t)
    w2 = wg * t / (1.0 + t)
    e1 = (i1 - N_GROUPS).astype(jnp.int32)
    e2 = (i2 - N_GROUPS).astype(jnp.int32)
    eidx = jnp.where(lane == 0, e1, jnp.where(lane == 1, e2, 0))
    wts = jnp.where(lane == 0, w1, jnp.where(lane == 1, w2, 0.0))
    return eidx, wts


def _outproj_body(x_ref, f_ref, c_ref, a_ref, w_ref, mod_ref, ln_ref, wr_ref, br_ref,
                  xo_ref, h2_ref, ei_ref, wt_ref):
    mix = (_dot(f_ref[...], w_ref[0:D_FOURIER, :])
           + _dot(c_ref[...], w_ref[D_FOURIER:D_FOURIER + D_CONV, :])
           + _dot(a_ref[...], w_ref[D_FOURIER + D_CONV:D_MIX, :]))
    r = ALPHA * x_ref[...] + mod_ref[2:3, :] * mix
    xn = _ln(r) * ln_ref[0:1, :] + ln_ref[1:2, :]
    xo_ref[...] = xn
    h2 = _ln(xn) * (1.0 + mod_ref[4:5, :]) + mod_ref[3:4, :]
    h2_ref[...] = h2
    logits = _dot(h2.astype(BF16), wr_ref[...]) + br_ref[...]
    eidx, wts = _route(logits)
    ei_ref[...] = eidx
    wt_ref[...] = wts


def _outproj(x, four, conv, att, w_out_bf, mod8, ln8, wr_bf, br, *, tm):
    s, d = x.shape
    row = lambda n: pl.BlockSpec((tm, n), lambda i: (i, 0))
    return pl.pallas_call(
        _outproj_body,
        grid=(s // tm,),
        in_specs=[row(d), row(D_FOURIER), row(D_CONV), row(D_ATT), _resident((D_MIX, d)), _resident((8, d)),
                  _resident((8, d)), _resident((d, ROUTER_LANES)), _resident((1, ROUTER_LANES))],
        out_specs=[row(d), row(d), row(ROUTER_LANES), row(ROUTER_LANES)],
        out_shape=[jax.ShapeDtypeStruct((s, d), F32), jax.ShapeDtypeStruct((s, d), F32),
                   jax.ShapeDtypeStruct((s, ROUTER_LANES), jnp.int32),
                   jax.ShapeDtypeStruct((s, ROUTER_LANES), F32)],
        compiler_params=_cparams("arbitrary"),
    )(x, four, conv, att, w_out_bf, mod8, ln8, wr_bf, br)


TOK_TILE = 256


def _moe_plan(eidx2, tb):
    n = eidx2.shape[0]
    nb = -(-(n * TOP_K) // tb) + N_EXPERTS
    onehot = (eidx2[:, :, None] == jnp.arange(N_EXPERTS, dtype=jnp.int32)).astype(jnp.int32)
    per_tok = onehot.sum(axis=1)
    incl = jnp.cumsum(per_tok, axis=0)
    counts = incl[-1]
    padded = (counts + tb - 1) // tb * tb
    pad_end = jnp.cumsum(padded)
    rank_base = (incl - per_tok) + (pad_end - padded)[None, :]
    dest = jnp.sum(onehot * rank_base[:, None, :], axis=-1).astype(jnp.int32)
    block_row = jnp.arange(nb, dtype=jnp.int32) * tb
    block_e = jnp.minimum(jnp.sum(pad_end[None, :] <= block_row[:, None], axis=1), N_EXPERTS - 1).astype(jnp.int32)
    first = jnp.concatenate([jnp.ones((1,), jnp.int32), (block_e[1:] != block_e[:-1]).astype(jnp.int32)])
    nused = (pad_end[-1] // tb).astype(jnp.int32).reshape(1)
    zrow = jnp.maximum(pad_end - tb, 0).astype(jnp.int32)
    zflag = (counts > 0).astype(jnp.int32)
    return dest, block_e, first, nused, zrow, zflag, nb


def _dispatch_body(zrow_ref, zflag_ref, nused_ref, dest_ref, h_ref, xs_hbm, zbuf, stage, sem, zsem,
                   *, tb, nt, nb):
    i = pl.program_id(0)

    def zero_copy(e):
        return pltpu.make_async_copy(zbuf, xs_hbm.at[pl.ds(pl.multiple_of(zrow_ref[e], tb), tb)], zsem)

    @pl.when(i == 0)
    def _():
        zbuf[...] = jnp.zeros_like(zbuf)
        for e in range(N_EXPERTS):
            @pl.when(zflag_ref[e] == 1)
            def _():
                zero_copy(e).start()
        for e in range(N_EXPERTS):
            @pl.when(zflag_ref[e] == 1)
            def _():
                zero_copy(e).wait()

        def tail_copy(j):
            return pltpu.make_async_copy(zbuf, xs_hbm.at[pl.ds(pl.multiple_of(j * tb, tb), tb)], zsem)

        def start_tail(j, c):
            tail_copy(j).start()
            return c

        def wait_tail(j, c):
            tail_copy(j).wait()
            return c
        lax.fori_loop(nused_ref[0], nb, start_tail, 0)
        lax.fori_loop(nused_ref[0], nb, wait_tail, 0)

    slot = i % 2

    def wait_tile(sl):
        for _ in range(TOP_K):
            pltpu.make_async_copy(stage.at[sl], xs_hbm.at[pl.ds(0, TOK_TILE)], sem.at[sl]).wait()

    @pl.when(i >= 2)
    def _():
        wait_tile(slot)

    stage[slot] = h_ref[...]

    def body(r, c):
        src = stage.at[slot, pl.ds(r, 1)]
        for k in range(TOP_K):
            row = dest_ref[0, 0, TOP_K * r + k]
            pltpu.make_async_copy(src, xs_hbm.at[pl.ds(row, 1)], sem.at[slot]).start()
        return c
    lax.fori_loop(0, TOK_TILE, body, 0)

    @pl.when(i == nt - 1)
    def _():
        wait_tile(slot)
        if nt >= 2:
            wait_tile(1 - slot)


def _dispatch(h2, dest, zrow, zflag, nused, *, tb, nb):
    n, d = h2.shape
    nt = n // TOK_TILE
    grid_spec = pltpu.PrefetchScalarGridSpec(
        num_scalar_prefetch=3,
        grid=(nt,),
        in_specs=[pl.BlockSpec((1, 1, TOP_K * TOK_TILE), lambda i, *_: (i, 0, 0), memory_space=pltpu.SMEM),
                  pl.BlockSpec((TOK_TILE, d), lambda i, *_: (i, 0))],
        out_specs=pl.BlockSpec(memory_space=pl.ANY),
        scratch_shapes=[pltpu.VMEM((tb, d), F32), pltpu.VMEM((2, TOK_TILE, d), F32),
                        pltpu.SemaphoreType.DMA((2,)), pltpu.SemaphoreType.DMA(())],
    )
    return pl.pallas_call(
        functools.partial(_dispatch_body, tb=tb, nt=nt, nb=nb),
        grid_spec=grid_spec,
        out_shape=jax.ShapeDtypeStruct((nb * tb, d), F32),
        compiler_params=_cparams("arbitrary"),
    )(zrow, zflag, nused, dest.reshape(nt, 1, TOP_K * TOK_TILE), h2)


def _experts_body(be_ref, first_ref, nused_ref, x_ref, wg_ref, wu_ref, wd_ref, o_ref, wgb, wub, wdb):
    del be_ref
    i = pl.program_id(0)

    @pl.when(i < nused_ref[0])
    def _():
        @pl.when(first_ref[i] == 1)
        def _():
            wgb[...] = wg_ref[0].astype(BF16)
            wub[...] = wu_ref[0].astype(BF16)
            wdb[...] = wd_ref[0].astype(BF16)

        x = x_ref[...].astype(BF16)
        g = _dot(x, wgb[...])
        u = _dot(x, wub[...])
        mid = (g * jax.nn.sigmoid(g) * u).astype(BF16)
        o_ref[...] = _dot(mid, wdb[...])

    @pl.when(i >= nused_ref[0])
    def _():
        o_ref[...] = jnp.zeros_like(o_ref)


def _experts(xs, block_e, first, nused, w_gate, w_up, w_down, *, tb, nb):
    d = xs.shape[1]
    wspec = lambda shape: pl.BlockSpec(shape, lambda i, be, first, nused: (be[i], 0, 0))
    xspec = pl.BlockSpec((tb, d), lambda i, be, first, nused: (jnp.minimum(i, nused[0] - 1), 0))
    grid_spec = pltpu.PrefetchScalarGridSpec(
        num_scalar_prefetch=3,
        grid=(nb,),
        in_specs=[xspec, wspec((1, d, D_EXPERT)), wspec((1, d, D_EXPERT)), wspec((1, D_EXPERT, d))],
        out_specs=pl.BlockSpec((tb, d), lambda i, *_: (i, 0)),
        scratch_shapes=[pltpu.VMEM((d, D_EXPERT), BF16), pltpu.VMEM((d, D_EXPERT), BF16),
                        pltpu.VMEM((D_EXPERT, d), BF16)],
    )
    return pl.pallas_call(
        _experts_body,
        grid_spec=grid_spec,
        out_shape=jax.ShapeDtypeStruct((nb * tb, d), F32),
        compiler_params=_cparams("arbitrary"),
    )(block_e, first, nused, xs, w_gate, w_up, w_down)


def _combine_body(dcur_ref, dnext_ref, x_ref, wt_ref, mod_ref, ln_ref, ys_hbm, o_ref, ybuf, sem, *, nt):
    i = pl.program_id(0)

    def issue(d_ref, slot):
        def body(r, c):
            for k in range(TOP_K):
                row = d_ref[0, 0, TOP_K * r + k]
                pltpu.make_async_copy(ys_hbm.at[pl.ds(row, 1)], ybuf.at[slot, k, pl.ds(r, 1)], sem.at[slot]).start()
            return c
        lax.fori_loop(0, TOK_TILE, body, 0)

    @pl.when(i == 0)
    def _():
        issue(dcur_ref, 0)

    @pl.when(i + 1 < nt)
    def _():
        issue(dnext_ref, (i + 1) % 2)

    slot = i % 2
    for k in range(TOP_K):
        pltpu.make_async_copy(ys_hbm.at[pl.ds(0, TOK_TILE)], ybuf.at[slot, k], sem.at[slot]).wait()
    w = wt_ref[...]
    y = w[:, 0:1] * ybuf[slot, 0] + w[:, 1:2] * ybuf[slot, 1]
    r = ALPHA * x_ref[...] + mod_ref[5:6, :] * y
    o_ref[...] = _ln(r) * ln_ref[0:1, :] + ln_ref[1:2, :]


def _combine(x, ys, dest, wt, mod8, ln8, *, row0):
    s, d = x.shape
    nt = s // TOK_TILE
    t0 = row0 // TOK_TILE
    dest3 = dest.reshape(-1, 1, TOP_K * TOK_TILE)
    dspec = lambda off: pl.BlockSpec((1, 1, TOP_K * TOK_TILE),
                                     lambda i: (t0 + jnp.minimum(i + off, nt - 1), 0, 0), memory_space=pltpu.SMEM)
    return pl.pallas_call(
        functools.partial(_combine_body, nt=nt),
        grid=(nt,),
        in_specs=[dspec(0), dspec(1),
                  pl.BlockSpec((TOK_TILE, d), lambda i: (i, 0)),
                  pl.BlockSpec((TOK_TILE, ROUTER_LANES), lambda i: (i, 0)),
                  _resident((8, d)), _resident((8, d)),
                  pl.BlockSpec(memory_space=pl.ANY)],
        out_specs=pl.BlockSpec((TOK_TILE, d), lambda i: (i, 0)),
        out_shape=jax.ShapeDtypeStruct((s, d), F32),
        scratch_shapes=[pltpu.VMEM((2, TOP_K, TOK_TILE, d), F32), pltpu.SemaphoreType.DMA((2,))],
        compiler_params=_cparams("arbitrary"),
    )(dest3, dest3, x, wt, mod8, ln8, ys)


def _moe(h2, eidx2, w_gate, w_up, w_down):
    tb = MOE_ROWS
    dest, block_e, first, nused, zrow, zflag, nb = _moe_plan(eidx2, tb)
    xs = _dispatch(h2, dest, zrow, zflag, nused, tb=tb, nb=nb)
    ys = _experts(xs, block_e, first, nused, w_gate, w_up, w_down, tb=tb, nb=nb)
    return ys, dest


def _pick(n, prefs):
    for t in prefs:
        if n % t == 0:
            return t
    raise ValueError(f"no tile of {prefs} divides {n}")


def _rows8(*vecs):
    d = vecs[0].shape[-1]
    out = jnp.zeros((8, d), F32)
    for j, v in enumerate(vecs):
        out = out.at[j].set(v)
    return out


def kernel(x, c, ctx, c_ctx, w_mod, b_mod, w_in, b_in, w_fourier, conv_w, conv_b, conv_ln_g, conv_ln_b, w_pw,
           lam_q1, lam_k1, lam_q2, lam_k2, subln_g, w_out, ln_a_g, ln_a_b, w_rg, b_rg, w_re, b_re,
           w_gate, w_up, w_down, ln_f_g, ln_f_b):
    b, s, d = x.shape
    nc = ctx.shape[1]
    assert b == 1 and d == D_MODEL and c.shape[0] == 1 and ctx.shape[0] == 1
    assert s % (DFT1 * 8) == 0 and nc % TOK_TILE == 0
    xl, xc = x[0], ctx[0]
    depth = w_mod.shape[0]

    c2 = jnp.zeros((8, d), F32).at[0].set(c[0]).at[1].set(c_ctx)
    mod_all = _modulation(c2, w_mod, b_mod)
    tables_l = _rope_tables(s)
    tables_c = tuple(jnp.zeros((nc, LANES), F32) for _ in range(3))
    tm_l = _pick(s, (512, 256))
    tm_c = _pick(nc, (256,))
    tq = _pick(s, (512, 256))
    tk = _pick(s + nc, (3328, 1280, 256))

    for i in range(depth):
        last = i == depth - 1
        lam_init = 0.8 - 0.6 * math.exp(-0.3 * i)
        mod_l = _rows8(*jnp.split(mod_all[i, 0], N_MOD))
        mod_c = _rows8(*jnp.split(mod_all[i, 1], N_MOD))
        w_in_bf = w_in[i].astype(BF16)
        wf_bf = w_fourier[i].astype(BF16)
        wpw_bf = w_pw[i].astype(BF16)
        w_out_bf = w_out[i].astype(BF16)
        wr = jnp.zeros((d, ROUTER_LANES), F32).at[:, :N_GROUPS].set(w_rg[i])
        wr_bf = wr.at[:, N_GROUPS:N_GROUPS + N_EXPERTS].set(w_re[i]).astype(BF16)
        br = jnp.zeros((1, ROUTER_LANES), F32).at[0, :N_GROUPS].set(b_rg[i])
        br = br.at[0, N_GROUPS:N_GROUPS + N_EXPERTS].set(b_re[i])
        lam_p = jnp.zeros((8, LANES), F32).at[0, :ATT_QK_DIM].set(lam_q1[i]).at[1, :ATT_QK_DIM].set(lam_k1[i])
        lam_p = lam_p.at[2, :ATT_QK_DIM].set(lam_q2[i]).at[3, :ATT_QK_DIM].set(lam_k2[i])
        ln_a = _rows8(ln_a_g[i], ln_a_b[i])
        ln_f = _rows8(ln_f_g[i], ln_f_b[i])
        conv_args = (conv_w[i], conv_b[i], conv_ln_g[i], conv_ln_b[i], wpw_bf)

        f_l, u_l, q_l, k_l, v_l = _inproj(xl, mod_l, w_in_bf, b_in[i], tables_l, rope=True, tm=tm_l)
        f_c, u_c, q_c, k_c, v_c = _inproj(xc, mod_c, w_in_bf, b_in[i], tables_c, rope=False, tm=tm_c)
        k_all = jnp.concatenate([k_c, k_l], axis=0)
        v_all = jnp.concatenate([v_c, v_l], axis=0)
        att_l = _attention(q_l, k_all, v_all, lam_p, subln_g[i], lam_init=lam_init, tq=tq, tk=tk)
        four_l = _fourier_fourstep(f_l, wf_bf)
        conv_l = _conv(u_l, *conv_args, tm=tm_l)
        xl, h2_l, ei_l, wt_l = _outproj(xl, four_l, conv_l, att_l, w_out_bf, mod_l, ln_a, wr_bf, br, tm=tm_l)
        if last:
            ys, dest = _moe(h2_l, ei_l[:, :TOP_K], w_gate[i], w_up[i], w_down[i])
            xl = _combine(xl, ys, dest, wt_l, mod_l, ln_f, row0=0)
        else:
            att_c = _attention(q_c, k_c, v_c, lam_p, subln_g[i], lam_init=lam_init, tq=tm_c, tk=tm_c)
            four_c = _fourier_dense(f_c, wf_bf)
            conv_c = _conv(u_c, *conv_args, tm=tm_c)
            xc, h2_c, ei_c, wt_c = _outproj(xc, four_c, conv_c, att_c, w_out_bf, mod_c, ln_a, wr_bf, br, tm=tm_c)
            h2 = jnp.concatenate([h2_c, h2_l], axis=0)
            ei = jnp.concatenate([ei_c[:, :TOP_K], ei_l[:, :TOP_K]], axis=0)
            ys, dest = _moe(h2, ei, w_gate[i], w_up[i], w_down[i])
            xc = _combine(xc, ys, dest, wt_c, mod_c, ln_f, row0=0)
            xl = _combine(xl, ys, dest, wt_l, mod_l, ln_f, row0=nc)
    return xl[None]
```

```python
import functools
import math

import numpy as np
import jax
import jax.numpy as jnp
from jax import lax
from jax.experimental import pallas as pl
from jax.experimental.pallas import tpu as pltpu

F32 = jnp.float32
BF16 = jnp.bfloat16
F8 = jnp.float8_e4m3fn

D_MODEL = 2048
DEPTH = 2
GRID_W = 64
D_FOURIER = 512
N_FOURIER_GROUPS = 4
FOURIER_GROUP = 128
D_CONV = 512
CONV_WIDTH = 31
CONV_PAD = CONV_WIDTH // 2
N_ATT_HEADS = 8
ATT_QK_DIM = 64
ATT_V_DIM = 128
D_ATT = N_ATT_HEADS * ATT_V_DIM
D_MIX = D_FOURIER + D_CONV + D_ATT
QK_COLS = N_ATT_HEADS * 2 * ATT_QK_DIM
ATT_SCALE = 1.0 / math.sqrt(ATT_QK_DIM)
ROPE_BASE = 10000.0
ROPE_PAIRS_PER_AXIS = ATT_QK_DIM // 4
OFF_CONV = D_FOURIER
OFF_Q = OFF_CONV + 2 * D_CONV
OFF_K = OFF_Q + QK_COLS
OFF_V = OFF_K + QK_COLS
D_IN_PROJ = OFF_V + D_ATT
N_GROUPS = 4
EXPERTS_PER_GROUP = 8
N_EXPERTS = N_GROUPS * EXPERTS_PER_GROUP
TOP_K = 2
D_EXPERT = 512
N_MOD = 6
LN_EPS = 1e-6
ALPHA = (2.0 * DEPTH) ** 0.25

LANES = 128
VMEM_LIMIT = 56 * 1024 * 1024
LOG2E = 1.4426950408889634
Q_SCALE = ATT_SCALE * LOG2E
NEG_BIG = -1e30
DFT1 = 128
MOE_ROWS = 256
ROUTER_LANES = 128
ATT_V_ROWS = ATT_V_DIM + 16
ATT_SUB = 256
def _cparams(*sem):
    return pltpu.CompilerParams(dimension_semantics=tuple(sem), vmem_limit_bytes=VMEM_LIMIT)


def _resident(shape):
    nd = len(shape)
    return pl.BlockSpec(shape, lambda *_: (0,) * nd, pipeline_mode=pl.Buffered(1))


def _ln(x):
    mu = jnp.mean(x, axis=-1, keepdims=True)
    xc = x - mu
    var = jnp.mean(xc * xc, axis=-1, keepdims=True)
    return xc * lax.rsqrt(var + LN_EPS)


def _dot(a, b):
    return jnp.dot(a, b, preferred_element_type=F32)


def _mod_body(c_ref, w_ref, b_ref, o_ref):
    c = c_ref[...]
    sc = (c * jax.nn.sigmoid(c)).astype(BF16)
    o_ref[0] = _dot(sc, w_ref[0].astype(BF16)) + b_ref[0]


def _modulation(c2, w_mod, b_mod):
    nl, d, n = w_mod.shape
    tn = 1024
    return pl.pallas_call(
        _mod_body,
        grid=(nl, n // tn),
        in_specs=[pl.BlockSpec((8, d), lambda l, j: (0, 0)),
                  pl.BlockSpec((1, d, tn), lambda l, j: (l, 0, j)),
                  pl.BlockSpec((1, 1, tn), lambda l, j: (l, 0, j))],
        out_specs=pl.BlockSpec((1, 8, tn), lambda l, j: (l, 0, j)),
        out_shape=jax.ShapeDtypeStruct((nl, 8, n), F32),
        compiler_params=_cparams("arbitrary", "arbitrary"),
    )(c2, w_mod, b_mod.reshape(nl, 1, n))


def _inproj_body(x_ref, mod_ref, w_ref, b_ref, cos_ref, sa_ref, sb_ref,
                 f_ref, u_ref, q_ref, k_ref, v_ref, stat_ref, *, rope):
    h = _ln(x_ref[...]) * (1.0 + mod_ref[1:2, :]) + mod_ref[0:1, :]
    hb = h.astype(BF16)

    def proj(lo, hi):
        return _dot(hb, w_ref[:, lo:hi]) + b_ref[:, lo:hi]

    f_ref[...] = proj(0, OFF_CONV).astype(BF16)
    a = proj(OFF_CONV, OFF_CONV + D_CONV)
    g = proj(OFF_CONV + D_CONV, OFF_Q)
    u_ref[...] = (a * jax.nn.sigmoid(g)).astype(BF16)
    v_ref[...] = proj(OFF_V, D_IN_PROJ).astype(BF16)
    q = proj(OFF_Q, OFF_K)
    k = proj(OFF_K, OFF_V)
    if rope:
        cos, sa, sb = cos_ref[...], sa_ref[...], sb_ref[...]
    qmax = kmax = None
    for hd in range(N_ATT_HEADS):
        sl = slice(hd * LANES, (hd + 1) * LANES)
        qh, kh = q[:, sl], k[:, sl]
        if rope:
            qh = qh * cos + pltpu.roll(qh, LANES - 32, 1) * sa + pltpu.roll(qh, 32, 1) * sb
            kh = kh * cos + pltpu.roll(kh, LANES - 32, 1) * sa + pltpu.roll(kh, 32, 1) * sb
        qb, kb = (qh * Q_SCALE).astype(BF16), kh.astype(BF16)
        q_ref[:, sl] = qb
        k_ref[:, sl] = kb
        qa, ka = jnp.abs(qb.astype(F32)), jnp.abs(kb.astype(F32))
        qmax = qa if qmax is None else jnp.maximum(qmax, qa)
        kmax = ka if kmax is None else jnp.maximum(kmax, ka)
    qm = jnp.max(jnp.max(qmax, axis=0, keepdims=True), axis=1, keepdims=True)
    km = jnp.max(jnp.max(kmax, axis=0, keepdims=True), axis=1, keepdims=True)
    row = lax.broadcasted_iota(jnp.int32, (8, LANES), 0)
    stat_ref[0] = jnp.where(row == 0, qm, jnp.where(row == 1, km, 0.0))


def _inproj(x, mod8, w_bf, b, tables, *, rope, tm):
    s, d = x.shape
    cos, sa, sb = tables
    row = lambda n: pl.BlockSpec((tm, n), lambda i: (i, 0))
    outs = [(D_FOURIER, BF16), (D_CONV, BF16), (QK_COLS, BF16), (QK_COLS, BF16), (D_ATT, BF16)]
    return pl.pallas_call(
        functools.partial(_inproj_body, rope=rope),
        grid=(s // tm,),
        in_specs=[row(d), _resident((8, d)), _resident((d, D_IN_PROJ)), _resident((1, D_IN_PROJ)),
                  row(LANES), row(LANES), row(LANES)],
        out_specs=[row(n) for n, _ in outs] + [pl.BlockSpec((1, 8, LANES), lambda i: (i, 0, 0))],
        out_shape=[jax.ShapeDtypeStruct((s, n), dt) for n, dt in outs]
        + [jax.ShapeDtypeStruct((s // tm, 8, LANES), F32)],
        compiler_params=_cparams("arbitrary"),
    )(x, mod8, w_bf, b.reshape(1, -1), cos, sa, sb)


def _rope_tables(n_tokens):
    rows = n_tokens // GRID_W
    row = jnp.repeat(jnp.arange(rows), GRID_W).astype(F32)
    col = jnp.tile(jnp.arange(GRID_W), rows).astype(F32)
    inv = 1.0 / (ROPE_BASE ** (jnp.arange(ROPE_PAIRS_PER_AXIS, dtype=F32) / ROPE_PAIRS_PER_AXIS))
    ang = jnp.concatenate([row[:, None] * inv, col[:, None] * inv], -1)
    cos, sin = jnp.cos(ang), jnp.sin(ang)
    zero = jnp.zeros_like(sin)
    cos_t = jnp.tile(cos, (1, 4))
    sa = jnp.tile(jnp.concatenate([-sin, zero], -1), (1, 2))
    sb = jnp.tile(jnp.concatenate([zero, sin], -1), (1, 2))
    return cos_t, sa, sb


def _dft_mats(n):
    k = np.arange(n)
    ang = 2.0 * np.pi * ((k[:, None] * k[None, :]) % n) / n
    return np.cos(ang), np.sin(ang)


def _channel_dft(scale):
    c, s = _dft_mats(FOURIER_GROUP)
    eye = np.eye(N_FOURIER_GROUPS)
    return (jnp.asarray(np.kron(eye, c) * scale, BF16), jnp.asarray(np.kron(eye, s) * scale, BF16))


def _fourier1_body(m_ref, x_ref, o_ref):
    o_ref[...] = _dot(m_ref[...], x_ref[...]).astype(BF16)


def _fourier2_body(hr_ref, hi_ref, tc_ref, ts_ref, m2_ref, cc_ref, sc_ref, wf_ref, o_ref, *, n2):
    tc_all, ts_all = tc_ref[0], ts_ref[0]
    for j in range(8):
        hr = hr_ref[j * n2:(j + 1) * n2, :].astype(F32)
        hi = hi_ref[j * n2:(j + 1) * n2, :].astype(F32)
        tc, ts = tc_all[:, j:j + 1], ts_all[:, j:j + 1]
        gr = hr * tc + hi * ts
        gi = hi * tc - hr * ts
        g = jnp.concatenate([gr, gi], axis=0).astype(BF16)
        y = _dot(m2_ref[...], g)
        z = _dot(y[:n2].astype(BF16), cc_ref[...]) + _dot(y[n2:].astype(BF16), sc_ref[...])
        o_ref[:, j, :] = _dot(z.astype(BF16), wf_ref[...]).astype(BF16)


def _fourier_fourstep(f, wf_bf):
    l, ch = f.shape
    n2 = l // DFT1
    c1, s1 = _dft_mats(DFT1)
    m1 = jnp.asarray(np.concatenate([c1, -s1], 0), BF16)
    tn = min(8192, n2 * ch)
    h = pl.pallas_call(
        _fourier1_body,
        grid=(n2 * ch // tn,),
        in_specs=[_resident((2 * DFT1, DFT1)), pl.BlockSpec((DFT1, tn), lambda j: (0, j))],
        out_specs=pl.BlockSpec((2 * DFT1, tn), lambda j: (0, j)),
        out_shape=jax.ShapeDtypeStruct((2 * DFT1, n2 * ch), BF16),
        compiler_params=_cparams("arbitrary"),
    )(m1, f.reshape(DFT1, n2 * ch))
    h = h.reshape(2 * DFT1 * n2, ch)

    k1 = np.arange(DFT1)
    nn = np.arange(n2)
    ang = 2.0 * np.pi * ((nn[:, None] * k1[None, :]) % l) / l
    tc = jnp.asarray(np.cos(ang).reshape(n2, DFT1 // 8, 8).transpose(1, 0, 2), F32)
    ts = jnp.asarray(np.sin(ang).reshape(n2, DFT1 // 8, 8).transpose(1, 0, 2), F32)
    c2, s2 = _dft_mats(n2)
    m2 = jnp.asarray(np.block([[c2, s2], [-s2, c2]]), BF16)
    cc, sc = _channel_dft(1.0 / math.sqrt(l * FOURIER_GROUP))
    nb = DFT1 // 8
    out = pl.pallas_call(
        functools.partial(_fourier2_body, n2=n2),
        grid=(nb,),
        in_specs=[pl.BlockSpec((8 * n2, ch), lambda b: (b, 0)),
                  pl.BlockSpec((8 * n2, ch), lambda b: (b + nb, 0)),
                  pl.BlockSpec((1, n2, 8), lambda b: (b, 0, 0)),
                  pl.BlockSpec((1, n2, 8), lambda b: (b, 0, 0)),
                  _resident((2 * n2, 2 * n2)), _resident((ch, ch)), _resident((ch, ch)), _resident((ch, ch))],
        out_specs=pl.BlockSpec((n2, 8, ch), lambda b: (0, b, 0)),
        out_shape=jax.ShapeDtypeStruct((n2, DFT1, ch), BF16),
        compiler_params=_cparams("arbitrary"),
    )(h, h, tc, ts, m2, cc, sc, wf_bf)
    return out.reshape(l, ch)


def _fourier_dense_body(f_ref, cc_ref, sc_ref, mp_ref, wf_ref, o_ref):
    f = f_ref[...]
    a = jnp.concatenate([_dot(f, cc_ref[...]), _dot(f, sc_ref[...])], axis=0).astype(BF16)
    z = _dot(mp_ref[...], a)
    o_ref[...] = _dot(z.astype(BF16), wf_ref[...]).astype(BF16)


def _fourier_dense(f, wf_bf):
    l, ch = f.shape
    cl, sl = _dft_mats(l)
    mp = jnp.asarray(np.concatenate([cl, -sl], 1), BF16)
    cc, sc = _channel_dft(1.0 / math.sqrt(l * FOURIER_GROUP))
    return pl.pallas_call(
        _fourier_dense_body,
        grid=(1,),
        in_specs=[_resident((l, ch)), _resident((ch, ch)), _resident((ch, ch)), _resident((l, 2 * l)),
                  _resident((ch, ch))],
        out_specs=pl.BlockSpec((l, ch), lambda i: (0, 0)),
        out_shape=jax.ShapeDtypeStruct((l, ch), BF16),
        compiler_params=_cparams("arbitrary"),
    )(f, cc, sc, mp, wf_bf)


CONV_HALO = 16
CONV_ROWS = 64


def _conv_body(prev_ref, cur_ref, next_ref, cw_ref, vec_ref, wpw_ref, o_ref, ubuf, *, tm, nt):
    t = pl.program_id(0)
    ubuf[0:CONV_HALO, :] = jnp.where(t > 0, prev_ref[...].astype(F32), 0.0)
    ubuf[CONV_HALO:CONV_HALO + tm, :] = cur_ref[...].astype(F32)
    ubuf[CONV_HALO + tm:2 * CONV_HALO + tm, :] = jnp.where(t < nt - 1, next_ref[...].astype(F32), 0.0)
    cb, lg, lb = vec_ref[0:1, :], vec_ref[1:2, :], vec_ref[2:3, :]
    first = CONV_HALO - CONV_PAD
    for c in range(tm // CONV_ROWS):
        base = c * CONV_ROWS + first
        acc = cw_ref[0:1, :] * ubuf[base:base + CONV_ROWS, :]
        for j in range(1, CONV_WIDTH):
            acc = acc + cw_ref[j:j + 1, :] * ubuf[base + j:base + j + CONV_ROWS, :]
        y = _ln(acc + cb) * lg + lb
        y = y * jax.nn.sigmoid(y)
        o_ref[c * CONV_ROWS:(c + 1) * CONV_ROWS, :] = _dot(y.astype(BF16), wpw_ref[...]).astype(BF16)


def _conv(u, conv_w, conv_b, ln_g, ln_b, wpw_bf, *, tm):
    s, ch = u.shape
    nt = s // tm
    hb = tm // CONV_HALO
    nh = s // CONV_HALO
    cw = jnp.zeros((32, ch), F32).at[:CONV_WIDTH].set(conv_w)
    vec = jnp.zeros((8, ch), F32).at[0].set(conv_b).at[1].set(ln_g).at[2].set(ln_b)
    return pl.pallas_call(
        functools.partial(_conv_body, tm=tm, nt=nt),
        grid=(nt,),
        in_specs=[pl.BlockSpec((CONV_HALO, ch), lambda t: (jnp.maximum(t * hb - 1, 0), 0)),
                  pl.BlockSpec((tm, ch), lambda t: (t, 0)),
                  pl.BlockSpec((CONV_HALO, ch), lambda t: (jnp.minimum((t + 1) * hb, nh - 1), 0)),
                  _resident((32, ch)), _resident((8, ch)), _resident((ch, ch))],
        out_specs=pl.BlockSpec((tm, ch), lambda t: (t, 0)),
        out_shape=jax.ShapeDtypeStruct((s, ch), BF16),
        scratch_shapes=[pltpu.VMEM((tm + 2 * CONV_HALO, ch), F32)],
        compiler_params=_cparams("arbitrary"),
    )(u, u, u, cw, vec, wpw_bf)


def _split8(x):
    hi = x.astype(F8).astype(F32)
    return hi, (x - hi).astype(F8).astype(F32)


def _prep_q_body(sc_ref, q_ref, o_ref):
    hi, lo = _split8(q_ref[...].astype(F32) * sc_ref[0])
    first = lax.broadcasted_iota(jnp.int32, hi.shape, 1) < ATT_QK_DIM
    hl = (jnp.where(first, hi, pltpu.roll(lo, ATT_QK_DIM, 1)),
          jnp.where(first, pltpu.roll(hi, ATT_QK_DIM, 1), lo))
    for c in range(2):
        t = hl[c].T.astype(F8)
        o_ref[0, c, 0:LANES, :] = t
        o_ref[0, c, LANES:2 * LANES, :] = t


def _prep_k_body(sc_ref, k_ref, o_ref):
    hi, lo = _split8(k_ref[...].astype(F32) * sc_ref[0])
    first = lax.broadcasted_iota(jnp.int32, hi.shape, 1) < ATT_QK_DIM
    his, los = pltpu.roll(hi, ATT_QK_DIM, 1), pltpu.roll(lo, ATT_QK_DIM, 1)
    o_ref[0, 0, :, 0:LANES] = jnp.where(first, hi, his).astype(F8)
    o_ref[0, 0, :, LANES:2 * LANES] = jnp.where(first, lo, los).astype(F8)
    o_ref[0, 1, :, 0:LANES] = jnp.where(first, his, hi).astype(F8)
    o_ref[0, 1, :, LANES:2 * LANES] = jnp.where(first, los, lo).astype(F8)


def _prep_v_body(v_ref, o_ref):
    o_ref[0, 0, 0:ATT_V_DIM, :] = v_ref[...].astype(F32).T.astype(BF16)
    row = lax.broadcasted_iota(jnp.int32, (ATT_V_ROWS - ATT_V_DIM, v_ref.shape[0]), 0)
    o_ref[0, 0, ATT_V_DIM:ATT_V_ROWS, :] = jnp.where(row == 0, 1.0, 0.0).astype(BF16)


def _pow2_scale(mx):
    _, e = jnp.frexp(mx)
    return jnp.ldexp(jnp.float32(1.0), 5 - e).astype(F32)


def _attn_prep(q, k_all, v_all, qmax, kmax, *, tk, tm):
    s, lk = q.shape[0], k_all.shape[0]
    nk = lk // tk
    aq, ak = _pow2_scale(qmax), _pow2_scale(kmax)
    smem = pl.BlockSpec(memory_space=pltpu.SMEM)
    head_rows = lambda t: pl.BlockSpec((t, LANES), lambda h, i: (i, h))
    qt8 = pl.pallas_call(
        _prep_q_body,
        grid=(N_ATT_HEADS, s // tm),
        in_specs=[smem, head_rows(tm)],
        out_specs=pl.BlockSpec((1, 2, 2 * LANES, tm), lambda h, i: (h, 0, 0, i)),
        out_shape=jax.ShapeDtypeStruct((N_ATT_HEADS, 2, 2 * LANES, s), F8),
        compiler_params=_cparams("arbitrary", "arbitrary"),
    )(aq.reshape(1), q)
    tmk = _pick(lk, (1280, 256))
    k8 = pl.pallas_call(
        _prep_k_body,
        grid=(N_ATT_HEADS, lk // tmk),
        in_specs=[smem, head_rows(tmk)],
        out_specs=pl.BlockSpec((1, 2, tmk, 2 * LANES), lambda h, i: (h, 0, i, 0)),
        out_shape=jax.ShapeDtypeStruct((N_ATT_HEADS, 2, lk, 2 * LANES), F8),
        compiler_params=_cparams("arbitrary", "arbitrary"),
    )(ak.reshape(1), k_all)
    vt = pl.pallas_call(
        _prep_v_body,
        grid=(N_ATT_HEADS, nk),
        in_specs=[head_rows(tk)],
        out_specs=pl.BlockSpec((1, 1, ATT_V_ROWS, tk), lambda h, i: (h, i, 0, 0)),
        out_shape=jax.ShapeDtypeStruct((N_ATT_HEADS, nk, ATT_V_ROWS, tk), BF16),
        compiler_params=_cparams("arbitrary", "arbitrary"),
    )(v_all)
    return qt8, k8.reshape(N_ATT_HEADS, 2, nk, tk, 2 * LANES), vt, (1.0 / (aq * ak)).reshape(1)


def _attn_body(c_ref, lam_ref, g_ref, qt_ref, k_ref, vt_ref, o_ref, acc1, acc2, s1_buf, s2_buf,
               *, nk, tq, lam_init):
    cs = c_ref[0]
    cs_bf = cs.astype(BF16)
    acc1[...] = jnp.zeros_like(acc1)
    acc2[...] = jnp.zeros_like(acc2)

    tk = k_ref.shape[3]
    bufs, qs, accs = (s1_buf, s2_buf), (qt_ref[0, 0], qt_ref[0, 1]), (acc1, acc2)

    def chunk_step(i, slot, mx, m, *, consume, produce):
        if consume:
            mn = [jnp.maximum(m[c], mx[c]) for c in range(2)]
        new_mx, pv = [None, None], [None, None]
        for r in range(tk // ATT_SUB):
            rows = slice(r * ATT_SUB, (r + 1) * ATT_SUB)
            if produce:
                for c in range(2):
                    s = _dot(k_ref[0, c, i + 1, rows, :], qs[c])
                    bufs[c][1 - slot, rows, :] = s
                    smax = jnp.max(s, axis=0, keepdims=True)
                    new_mx[c] = smax if new_mx[c] is None else jnp.maximum(new_mx[c], smax)
            if consume:
                vt = vt_ref[0, i, :, rows]
                for c in range(2):
                    p = jnp.exp2((bufs[c][slot, rows, :] - mn[c]).astype(BF16) * cs_bf)
                    d = _dot(vt, p)
                    pv[c] = d if pv[c] is None else pv[c] + d
        if consume:
            for c in range(2):
                accs[c][...] = jnp.exp2((m[c] - mn[c]) * cs) * accs[c][...] + pv[c]
            m = tuple(mn)
        return tuple(new_mx), m

    def step(i, carry):
        mx, m = carry
        return lax.cond(i % 2 == 0,
                        lambda: chunk_step(i, 0, mx, m, consume=True, produce=True),
                        lambda: chunk_step(i, 1, mx, m, consume=True, produce=True))

    neg = jnp.full((1, tq), -jnp.inf, F32)
    mx0, _ = chunk_step(-1, 1, None, None, consume=False, produce=True)
    mx, m = lax.fori_loop(0, nk - 1, step, (mx0, (neg, neg)))
    chunk_step(nk - 1, (nk - 1) % 2, mx, m, consume=True, produce=False)

    lp = lam_ref[...]
    lam = (jnp.exp(jnp.sum(lp[0:1] * lp[1:2], axis=-1, keepdims=True))
           - jnp.exp(jnp.sum(lp[2:3] * lp[3:4], axis=-1, keepdims=True)) + lam_init)
    dv = ATT_V_DIM
    o = acc1[0:dv, :] / acc1[dv:dv + 1, :] - lam * (acc2[0:dv, :] / acc2[dv:dv + 1, :])
    ms = jnp.mean(o * o, axis=0, keepdims=True)
    o = o * lax.rsqrt(ms + LN_EPS) * g_ref[...] * (1.0 - lam_init)
    o_ref[...] = o.T.astype(BF16)


def _attention(q, k_all, v_all, qmax, kmax, lam_p, g, *, lam_init, tq, tk):
    s = q.shape[0]
    nk = k_all.shape[0] // tk
    qt8, k8, vt, c = _attn_prep(q, k_all, v_all, qmax, kmax, tk=tk, tm=tq)
    per_head = lambda shape: pl.BlockSpec(shape, lambda h, i: (h,) + (0,) * (len(shape) - 1),
                                          pipeline_mode=pl.Buffered(1))
    return pl.pallas_call(
        functools.partial(_attn_body, nk=nk, tq=tq, lam_init=lam_init),
        grid=(N_ATT_HEADS, s // tq),
        in_specs=[pl.BlockSpec(memory_space=pltpu.SMEM),
                  pl.BlockSpec((8, LANES), lambda h, i: (0, 0)),
                  pl.BlockSpec((ATT_V_DIM, 1), lambda h, i: (0, 0)),
                  pl.BlockSpec((1, 2, 2 * LANES, tq), lambda h, i: (h, 0, 0, i)),
                  per_head((1, 2, nk, tk, 2 * LANES)),
                  per_head((1, nk, ATT_V_ROWS, tk))],
        out_specs=pl.BlockSpec((tq, ATT_V_DIM), lambda h, i: (i, h)),
        out_shape=jax.ShapeDtypeStruct((s, D_ATT), BF16),
        scratch_shapes=[pltpu.VMEM((ATT_V_ROWS, tq), F32), pltpu.VMEM((ATT_V_ROWS, tq), F32)]
        + [pltpu.VMEM((2, tk, tq), F32) for _ in range(2)],
        compiler_params=_cparams("arbitrary", "arbitrary"),
    )(c, lam_p, g.reshape(ATT_V_DIM, 1), qt8, k8, vt)


def _route(logits):
    lane = lax.broadcasted_iota(jnp.int32, logits.shape, 1)
    lane_f = lane.astype(F32)
    is_g = lane < N_GROUPS
    gl = jnp.where(is_g, logits, NEG_BIG)
    gmax = jnp.max(gl, axis=-1, keepdims=True)
    gidx = jnp.min(jnp.where(gl == gmax, lane_f, float(ROUTER_LANES)), axis=-1, keepdims=True)
    wg = 1.0 / jnp.sum(jnp.where(is_g, jnp.exp(gl - gmax), 0.0), axis=-1, keepdims=True)
    grp = ((lane - N_GROUPS) // EXPERTS_PER_GROUP).astype(F32)
    valid = (lane >= N_GROUPS) & (lane < N_GROUPS + N_EXPERTS) & (grp == gidx)
    el = jnp.where(valid, logits, NEG_BIG)
    m1 = jnp.max(el, axis=-1, keepdims=True)
    i1 = jnp.min(jnp.where(el == m1, lane_f, float(ROUTER_LANES)), axis=-1, keepdims=True)
    el2 = jnp.where(lane_f == i1, NEG_BIG, el)
    m2 = jnp.max(el2, axis=-1, keepdims=True)
    i2 = jnp.min(jnp.where(el2 == m2, lane_f, float(ROUTER_LANES)), axis=-1, keepdims=True)
    t = jnp.exp(m2 - m1)
    w1 = wg / (1.0 + t)
    w2 = wg * t / (1.0 + t)
    e1 = (i1 - N_GROUPS).astype(jnp.int32)
    e2 = (i2 - N_GROUPS).astype(jnp.int32)
    eidx = jnp.where(lane == 0, e1, jnp.where(lane == 1, e2, 0))
    wts = jnp.where(lane == 0, w1, jnp.where(lane == 1, w2, 0.0))
    return eidx, wts


def _outproj_body(x_ref, f_ref, c_ref, a_ref, w_ref, mod_ref, ln_ref, wr_ref, br_ref,
                  xo_ref, h2_ref, ei_ref, wt_ref):
    mix = (_dot(f_ref[...], w_ref[0:D_FOURIER, :])
           + _dot(c_ref[...], w_ref[D_FOURIER:D_FOURIER + D_CONV, :])
           + _dot(a_ref[...], w_ref[D_FOURIER + D_CONV:D_MIX, :]))
    r = ALPHA * x_ref[...] + mod_ref[2:3, :] * mix
    xn = _ln(r) * ln_ref[0:1, :] + ln_ref[1:2, :]
    xo_ref[...] = xn
    h2 = _ln(xn) * (1.0 + mod_ref[4:5, :]) + mod_ref[3:4, :]
    h2_ref[...] = h2
    logits = _dot(h2.astype(BF16), wr_ref[...]) + br_ref[...]
    eidx, wts = _route(logits)
    ei_ref[...] = eidx
    wt_ref[...] = wts


def _outproj(x, four, conv, att, w_out_bf, mod8, ln8, wr_bf, br, *, tm):
    s, d = x.shape
    row = lambda n: pl.BlockSpec((tm, n), lambda i: (i, 0))
    return pl.pallas_call(
        _outproj_body,
        grid=(s // tm,),
        in_specs=[row(d), row(D_FOURIER), row(D_CONV), row(D_ATT), _resident((D_MIX, d)), _resident((8, d)),
                  _resident((8, d)), _resident((d, ROUTER_LANES)), _resident((1, ROUTER_LANES))],
        out_specs=[row(d), row(d), row(ROUTER_LANES), row(ROUTER_LANES)],
        out_shape=[jax.ShapeDtypeStruct((s, d), F32), jax.ShapeDtypeStruct((s, d), F32),
                   jax.ShapeDtypeStruct((s, ROUTER_LANES), jnp.int32),
                   jax.ShapeDtypeStruct((s, ROUTER_LANES), F32)],
        compiler_params=_cparams("arbitrary"),
    )(x, four, conv, att, w_out_bf, mod8, ln8, wr_bf, br)


TOK_TILE = 256


def _moe_plan(eidx2, tb):
    n = eidx2.shape[0]
    nb = -(-(n * TOP_K) // tb) + N_EXPERTS
    onehot = (eidx2[:, :, None] == jnp.arange(N_EXPERTS, dtype=jnp.int32)).astype(jnp.int32)
    per_tok = onehot.sum(axis=1)
    incl = jnp.cumsum(per_tok, axis=0)
    counts = incl[-1]
    padded = (counts + tb - 1) // tb * tb
    pad_end = jnp.cumsum(padded)
    rank_base = (incl - per_tok) + (pad_end - padded)[None, :]
    dest = jnp.sum(onehot * rank_base[:, None, :], axis=-1).astype(jnp.int32)
    block_row = jnp.arange(nb, dtype=jnp.int32) * tb
    block_e = jnp.minimum(jnp.sum(pad_end[None, :] <= block_row[:, None], axis=1), N_EXPERTS - 1).astype(jnp.int32)
    first = jnp.concatenate([jnp.ones((1,), jnp.int32), (block_e[1:] != block_e[:-1]).astype(jnp.int32)])
    nused = (pad_end[-1] // tb).astype(jnp.int32).reshape(1)
    zrow = jnp.maximum(pad_end - tb, 0).astype(jnp.int32)
    zflag = (counts > 0).astype(jnp.int32)
    return dest, block_e, first, nused, zrow, zflag, nb


def _dispatch_body(zrow_ref, zflag_ref, nused_ref, dest_ref, h_ref, xs_hbm, zbuf, stage, sem, zsem,
                   *, tb, nt, nb):
    i = pl.program_id(0)

    def zero_copy(e):
        return pltpu.make_async_copy(zbuf, xs_hbm.at[pl.ds(pl.multiple_of(zrow_ref[e], tb), tb)], zsem)

    @pl.when(i == 0)
    def _():
        zbuf[...] = jnp.zeros_like(zbuf)
        for e in range(N_EXPERTS):
            @pl.when(zflag_ref[e] == 1)
            def _():
                zero_copy(e).start()
        for e in range(N_EXPERTS):
            @pl.when(zflag_ref[e] == 1)
            def _():
                zero_copy(e).wait()

        def tail_copy(j):
            return pltpu.make_async_copy(zbuf, xs_hbm.at[pl.ds(pl.multiple_of(j * tb, tb), tb)], zsem)

        def start_tail(j, c):
            tail_copy(j).start()
            return c

        def wait_tail(j, c):
            tail_copy(j).wait()
            return c
        lax.fori_loop(nused_ref[0], nb, start_tail, 0)
        lax.fori_loop(nused_ref[0], nb, wait_tail, 0)

    slot = i % 2

    def wait_tile(sl):
        for _ in range(TOP_K):
            pltpu.make_async_copy(stage.at[sl], xs_hbm.at[pl.ds(0, TOK_TILE)], sem.at[sl]).wait()

    @pl.when(i >= 2)
    def _():
        wait_tile(slot)

    stage[slot] = h_ref[...]

    def body(r, c):
        src = stage.at[slot, pl.ds(r, 1)]
        for k in range(TOP_K):
            row = dest_ref[0, 0, TOP_K * r + k]
            pltpu.make_async_copy(src, xs_hbm.at[pl.ds(row, 1)], sem.at[slot]).start()
        return c
    lax.fori_loop(0, TOK_TILE, body, 0)

    @pl.when(i == nt - 1)
    def _():
        wait_tile(slot)
        if nt >= 2:
            wait_tile(1 - slot)


def _dispatch(h2, dest, zrow, zflag, nused, *, tb, nb):
    n, d = h2.shape
    nt = n // TOK_TILE
    grid_spec = pltpu.PrefetchScalarGridSpec(
        num_scalar_prefetch=3,
        grid=(nt,),
        in_specs=[pl.BlockSpec((1, 1, TOP_K * TOK_TILE), lambda i, *_: (i, 0, 0), memory_space=pltpu.SMEM),
                  pl.BlockSpec((TOK_TILE, d), lambda i, *_: (i, 0))],
        out_specs=pl.BlockSpec(memory_space=pl.ANY),
        scratch_shapes=[pltpu.VMEM((tb, d), F32), pltpu.VMEM((2, TOK_TILE, d), F32),
                        pltpu.SemaphoreType.DMA((2,)), pltpu.SemaphoreType.DMA(())],
    )
    return pl.pallas_call(
        functools.partial(_dispatch_body, tb=tb, nt=nt, nb=nb),
        grid_spec=grid_spec,
        out_shape=jax.ShapeDtypeStruct((nb * tb, d), F32),
        compiler_params=_cparams("arbitrary"),
    )(zrow, zflag, nused, dest.reshape(nt, 1, TOP_K * TOK_TILE), h2)


def _experts_body(be_ref, first_ref, nused_ref, x_ref, wg_ref, wu_ref, wd_ref, o_ref, wgb, wub, wdb):
    del be_ref
    i = pl.program_id(0)

    @pl.when(i < nused_ref[0])
    def _():
        @pl.when(first_ref[i] == 1)
        def _():
            wgb[...] = wg_ref[0].astype(BF16)
            wub[...] = wu_ref[0].astype(BF16)
            wdb[...] = wd_ref[0].astype(BF16)

        x = x_ref[...].astype(BF16)
        g = _dot(x, wgb[...])
        u = _dot(x, wub[...])
        mid = (g * jax.nn.sigmoid(g) * u).astype(BF16)
        o_ref[...] = _dot(mid, wdb[...])

    @pl.when(i >= nused_ref[0])
    def _():
        o_ref[...] = jnp.zeros_like(o_ref)


def _experts(xs, block_e, first, nused, w_gate, w_up, w_down, *, tb, nb):
    d = xs.shape[1]
    wspec = lambda shape: pl.BlockSpec(shape, lambda i, be, first, nused: (be[i], 0, 0))
    xspec = pl.BlockSpec((tb, d), lambda i, be, first, nused: (jnp.minimum(i, nused[0] - 1), 0))
    grid_spec = pltpu.PrefetchScalarGridSpec(
        num_scalar_prefetch=3,
        grid=(nb,),
        in_specs=[xspec, wspec((1, d, D_EXPERT)), wspec((1, d, D_EXPERT)), wspec((1, D_EXPERT, d))],
        out_specs=pl.BlockSpec((tb, d), lambda i, *_: (i, 0)),
        scratch_shapes=[pltpu.VMEM((d, D_EXPERT), BF16), pltpu.VMEM((d, D_EXPERT), BF16),
                        pltpu.VMEM((D_EXPERT, d), BF16)],
    )
    return pl.pallas_call(
        _experts_body,
        grid_spec=grid_spec,
        out_shape=jax.ShapeDtypeStruct((nb * tb, d), F32),
        compiler_params=_cparams("arbitrary"),
    )(block_e, first, nused, xs, w_gate, w_up, w_down)


def _combine_body(dcur_ref, dnext_ref, x_ref, wt_ref, mod_ref, ln_ref, ys_hbm, o_ref, ybuf, sem, *, nt):
    i = pl.program_id(0)

    def issue(d_ref, slot):
        def body(r, c):
            for k in range(TOP_K):
                row = d_ref[0, 0, TOP_K * r + k]
                pltpu.make_async_copy(ys_hbm.at[pl.ds(row, 1)], ybuf.at[slot, k, pl.ds(r, 1)], sem.at[slot]).start()
            return c
        lax.fori_loop(0, TOK_TILE, body, 0)

    @pl.when(i == 0)
    def _():
        issue(dcur_ref, 0)

    @pl.when(i + 1 < nt)
    def _():
        issue(dnext_ref, (i + 1) % 2)

    slot = i % 2
    for k in range(TOP_K):
        pltpu.make_async_copy(ys_hbm.at[pl.ds(0, TOK_TILE)], ybuf.at[slot, k], sem.at[slot]).wait()
    w = wt_ref[...]
    y = w[:, 0:1] * ybuf[slot, 0] + w[:, 1:2] * ybuf[slot, 1]
    r = ALPHA * x_ref[...] + mod_ref[5:6, :] * y
    o_ref[...] = _ln(r) * ln_ref[0:1, :] + ln_ref[1:2, :]


def _combine(x, ys, dest, wt, mod8, ln8, *, row0):
    s, d = x.shape
    nt = s // TOK_TILE
    t0 = row0 // TOK_TILE
    dest3 = dest.reshape(-1, 1, TOP_K * TOK_TILE)
    dspec = lambda off: pl.BlockSpec((1, 1, TOP_K * TOK_TILE),
                                     lambda i: (t0 + jnp.minimum(i + off, nt - 1), 0, 0), memory_space=pltpu.SMEM)
    return pl.pallas_call(
        functools.partial(_combine_body, nt=nt),
        grid=(nt,),
        in_specs=[dspec(0), dspec(1),
                  pl.BlockSpec((TOK_TILE, d), lambda i: (i, 0)),
                  pl.BlockSpec((TOK_TILE, ROUTER_LANES), lambda i: (i, 0)),
                  _resident((8, d)), _resident((8, d)),
                  pl.BlockSpec(memory_space=pl.ANY)],
        out_specs=pl.BlockSpec((TOK_TILE, d), lambda i: (i, 0)),
        out_shape=jax.ShapeDtypeStruct((s, d), F32),
        scratch_shapes=[pltpu.VMEM((2, TOP_K, TOK_TILE, d), F32), pltpu.SemaphoreType.DMA((2,))],
        compiler_params=_cparams("arbitrary"),
    )(dest3, dest3, x, wt, mod8, ln8, ys)


def _moe(h2, eidx2, w_gate, w_up, w_down):
    tb = MOE_ROWS
    dest, block_e, first, nused, zrow, zflag, nb = _moe_plan(eidx2, tb)
    xs = _dispatch(h2, dest, zrow, zflag, nused, tb=tb, nb=nb)
    ys = _experts(xs, block_e, first, nused, w_gate, w_up, w_down, tb=tb, nb=nb)
    return ys, dest


def _pick(n, prefs):
    for t in prefs:
        if n % t == 0:
            return t
    raise ValueError(f"no tile of {prefs} divides {n}")


def _rows8(*vecs):
    d = vecs[0].shape[-1]
    out = jnp.zeros((8, d), F32)
    for j, v in enumerate(vecs):
        out = out.at[j].set(v)
    return out


def kernel(x, c, ctx, c_ctx, w_mod, b_mod, w_in, b_in, w_fourier, conv_w, conv_b, conv_ln_g, conv_ln_b, w_pw,
           lam_q1, lam_k1, lam_q2, lam_k2, subln_g, w_out, ln_a_g, ln_a_b, w_rg, b_rg, w_re, b_re,
           w_gate, w_up, w_down, ln_f_g, ln_f_b):
    b, s, d = x.shape
    nc = ctx.shape[1]
    assert b == 1 and d == D_MODEL and c.shape[0] == 1 and ctx.shape[0] == 1
    assert s % (DFT1 * 8) == 0 and nc % TOK_TILE == 0
    xl, xc = x[0], ctx[0]
    depth = w_mod.shape[0]

    c2 = jnp.zeros((8, d), F32).at[0].set(c[0]).at[1].set(c_ctx)
    mod_all = _modulation(c2, w_mod, b_mod)
    tables_l = _rope_tables(s)
    tables_c = tuple(jnp.zeros((nc, LANES), F32) for _ in range(3))
    tm_l = _pick(s, (512, 256))
    tm_c = _pick(nc, (256,))
    tq = _pick(s, (512, 256))
    tk = _pick(s + nc, (3328, 1280, 256))

    for i in range(depth):
        last = i == depth - 1
        lam_init = 0.8 - 0.6 * math.exp(-0.3 * i)
        mod_l = _rows8(*jnp.split(mod_all[i, 0], N_MOD))
        mod_c = _rows8(*jnp.split(mod_all[i, 1], N_MOD))
        w_in_bf = w_in[i].astype(BF16)
        wf_bf = w_fourier[i].astype(BF16)
        wpw_bf = w_pw[i].astype(BF16)
        w_out_bf = w_out[i].astype(BF16)
        wr = jnp.zeros((d, ROUTER_LANES), F32).at[:, :N_GROUPS].set(w_rg[i])
        wr_bf = wr.at[:, N_GROUPS:N_GROUPS + N_EXPERTS].set(w_re[i]).astype(BF16)
        br = jnp.zeros((1, ROUTER_LANES), F32).at[0, :N_GROUPS].set(b_rg[i])
        br = br.at[0, N_GROUPS:N_GROUPS + N_EXPERTS].set(b_re[i])
        lam_p = jnp.zeros((8, LANES), F32).at[0, :ATT_QK_DIM].set(lam_q1[i]).at[1, :ATT_QK_DIM].set(lam_k1[i])
        lam_p = lam_p.at[2, :ATT_QK_DIM].set(lam_q2[i]).at[3, :ATT_QK_DIM].set(lam_k2[i])
        ln_a = _rows8(ln_a_g[i], ln_a_b[i])
        ln_f = _rows8(ln_f_g[i], ln_f_b[i])
        conv_args = (conv_w[i], conv_b[i], conv_ln_g[i], conv_ln_b[i], wpw_bf)

        f_l, u_l, q_l, k_l, v_l, st_l = _inproj(xl, mod_l, w_in_bf, b_in[i], tables_l, rope=True, tm=tm_l)
        f_c, u_c, q_c, k_c, v_c, st_c = _inproj(xc, mod_c, w_in_bf, b_in[i], tables_c, rope=False, tm=tm_c)
        qmax_l, qmax_c = jnp.max(st_l[:, 0, 0]), jnp.max(st_c[:, 0, 0])
        kmax_c = jnp.max(st_c[:, 1, 0])
        kmax_all = jnp.maximum(jnp.max(st_l[:, 1, 0]), kmax_c)
        k_all = jnp.concatenate([k_c, k_l], axis=0)
        v_all = jnp.concatenate([v_c, v_l], axis=0)
        att_l = _attention(q_l, k_all, v_all, qmax_l, kmax_all, lam_p, subln_g[i],
                           lam_init=lam_init, tq=tq, tk=tk)
        four_l = _fourier_fourstep(f_l, wf_bf)
        conv_l = _conv(u_l, *conv_args, tm=tm_l)
        xl, h2_l, ei_l, wt_l = _outproj(xl, four_l, conv_l, att_l, w_out_bf, mod_l, ln_a, wr_bf, br, tm=tm_l)
        if last:
            ys, dest = _moe(h2_l, ei_l[:, :TOP_K], w_gate[i], w_up[i], w_down[i])
            xl = _combine(xl, ys, dest, wt_l, mod_l, ln_f, row0=0)
        else:
            att_c = _attention(q_c, k_c, v_c, qmax_c, kmax_c, lam_p, subln_g[i],
                               lam_init=lam_init, tq=tm_c, tk=tm_c)
            four_c = _fourier_dense(f_c, wf_bf)
            conv_c = _conv(u_c, *conv_args, tm=tm_c)
            xc, h2_c, ei_c, wt_c = _outproj(xc, four_c, conv_c, att_c, w_out_bf, mod_c, ln_a, wr_bf, br, tm=tm_c)
            h2 = jnp.concatenate([h2_c, h2_l], axis=0)
            ei = jnp.concatenate([ei_c[:, :TOP_K], ei_l[:, :TOP_K]], axis=0)
            ys, dest = _moe(h2, ei, w_gate[i], w_up[i], w_down[i])
            xc = _combine(xc, ys, dest, wt_c, mod_c, ln_f, row0=0)
            xl = _combine(xl, ys, dest, wt_l, mod_l, ln_f, row0=nc)
    return xl[None]
```

```python
import functools
import math

import numpy as np
import jax
import jax.numpy as jnp
from jax import lax
from jax.experimental import pallas as pl
from jax.experimental.pallas import tpu as pltpu

F32 = jnp.float32
BF16 = jnp.bfloat16
F8 = jnp.float8_e4m3fn

D_MODEL = 2048
DEPTH = 2
GRID_W = 64
D_FOURIER = 512
N_FOURIER_GROUPS = 4
FOURIER_GROUP = 128
D_CONV = 512
CONV_WIDTH = 31
CONV_PAD = CONV_WIDTH // 2
N_ATT_HEADS = 8
ATT_QK_DIM = 64
ATT_V_DIM = 128
D_ATT = N_ATT_HEADS * ATT_V_DIM
D_MIX = D_FOURIER + D_CONV + D_ATT
QK_COLS = N_ATT_HEADS * 2 * ATT_QK_DIM
ATT_SCALE = 1.0 / math.sqrt(ATT_QK_DIM)
ROPE_BASE = 10000.0
ROPE_PAIRS_PER_AXIS = ATT_QK_DIM // 4
OFF_CONV = D_FOURIER
OFF_Q = OFF_CONV + 2 * D_CONV
OFF_K = OFF_Q + QK_COLS
OFF_V = OFF_K + QK_COLS
D_IN_PROJ = OFF_V + D_ATT
N_GROUPS = 4
EXPERTS_PER_GROUP = 8
N_EXPERTS = N_GROUPS * EXPERTS_PER_GROUP
TOP_K = 2
D_EXPERT = 512
N_MOD = 6
LN_EPS = 1e-6
ALPHA = (2.0 * DEPTH) ** 0.25

LANES = 128
VMEM_LIMIT = 56 * 1024 * 1024
LOG2E = 1.4426950408889634
Q_SCALE = ATT_SCALE * LOG2E
NEG_BIG = -1e30
DFT1 = 128
MOE_ROWS = 256
ROUTER_LANES = 128
ATT_V_ROWS = ATT_V_DIM + 16
ATT_SUB = 256
def _cparams(*sem):
    return pltpu.CompilerParams(dimension_semantics=tuple(sem), vmem_limit_bytes=VMEM_LIMIT)


def _resident(shape):
    nd = len(shape)
    return pl.BlockSpec(shape, lambda *_: (0,) * nd, pipeline_mode=pl.Buffered(1))


def _ln(x):
    mu = jnp.mean(x, axis=-1, keepdims=True)
    xc = x - mu
    var = jnp.mean(xc * xc, axis=-1, keepdims=True)
    return xc * lax.rsqrt(var + LN_EPS)


def _dot(a, b):
    return jnp.dot(a, b, preferred_element_type=F32)


def _mod_body(c_ref, w_ref, b_ref, o_ref):
    c = c_ref[...]
    sc = (c * jax.nn.sigmoid(c)).astype(BF16)
    o_ref[0] = _dot(sc, w_ref[0].astype(BF16)) + b_ref[0]


def _modulation(c2, w_mod, b_mod):
    nl, d, n = w_mod.shape
    tn = 1024
    return pl.pallas_call(
        _mod_body,
        grid=(nl, n // tn),
        in_specs=[pl.BlockSpec((8, d), lambda l, j: (0, 0)),
                  pl.BlockSpec((1, d, tn), lambda l, j: (l, 0, j)),
                  pl.BlockSpec((1, 1, tn), lambda l, j: (l, 0, j))],
        out_specs=pl.BlockSpec((1, 8, tn), lambda l, j: (l, 0, j)),
        out_shape=jax.ShapeDtypeStruct((nl, 8, n), F32),
        compiler_params=_cparams("arbitrary", "arbitrary"),
    )(c2, w_mod, b_mod.reshape(nl, 1, n))


def _inproj_body(x_ref, mod_ref, w_ref, b_ref, cos_ref, sa_ref, sb_ref,
                 f_ref, u_ref, q_ref, k_ref, v_ref, stat_ref, *, rope):
    h = _ln(x_ref[...]) * (1.0 + mod_ref[1:2, :]) + mod_ref[0:1, :]
    hb = h.astype(BF16)

    def proj(lo, hi):
        return _dot(hb, w_ref[:, lo:hi]) + b_ref[:, lo:hi]

    f_ref[...] = proj(0, OFF_CONV).astype(BF16)
    a = proj(OFF_CONV, OFF_CONV + D_CONV)
    g = proj(OFF_CONV + D_CONV, OFF_Q)
    u_ref[...] = (a * jax.nn.sigmoid(g)).astype(BF16)
    v_ref[...] = proj(OFF_V, D_IN_PROJ).astype(BF16)
    q = proj(OFF_Q, OFF_K)
    k = proj(OFF_K, OFF_V)
    if rope:
        cos, sa, sb = cos_ref[...], sa_ref[...], sb_ref[...]
    qmax = kmax = None
    for hd in range(N_ATT_HEADS):
        sl = slice(hd * LANES, (hd + 1) * LANES)
        qh, kh = q[:, sl], k[:, sl]
        if rope:
            qh = qh * cos + pltpu.roll(qh, LANES - 32, 1) * sa + pltpu.roll(qh, 32, 1) * sb
            kh = kh * cos + pltpu.roll(kh, LANES - 32, 1) * sa + pltpu.roll(kh, 32, 1) * sb
        qb, kb = (qh * Q_SCALE).astype(BF16), kh.astype(BF16)
        q_ref[:, sl] = qb
        k_ref[:, sl] = kb
        qa, ka = jnp.abs(qb.astype(F32)), jnp.abs(kb.astype(F32))
        qmax = qa if qmax is None else jnp.maximum(qmax, qa)
        kmax = ka if kmax is None else jnp.maximum(kmax, ka)
    qm = jnp.max(jnp.max(qmax, axis=0, keepdims=True), axis=1, keepdims=True)
    km = jnp.max(jnp.max(kmax, axis=0, keepdims=True), axis=1, keepdims=True)
    row = lax.broadcasted_iota(jnp.int32, (8, LANES), 0)
    stat_ref[0] = jnp.where(row == 0, qm, jnp.where(row == 1, km, 0.0))


def _inproj(x, mod8, w_bf, b, tables, *, rope, tm):
    s, d = x.shape
    cos, sa, sb = tables
    row = lambda n: pl.BlockSpec((tm, n), lambda i: (i, 0))
    outs = [(D_FOURIER, BF16), (D_CONV, BF16), (QK_COLS, BF16), (QK_COLS, BF16), (D_ATT, BF16)]
    return pl.pallas_call(
        functools.partial(_inproj_body, rope=rope),
        grid=(s // tm,),
        in_specs=[row(d), _resident((8, d)), _resident((d, D_IN_PROJ)), _resident((1, D_IN_PROJ)),
                  row(LANES), row(LANES), row(LANES)],
        out_specs=[row(n) for n, _ in outs] + [pl.BlockSpec((1, 8, LANES), lambda i: (i, 0, 0))],
        out_shape=[jax.ShapeDtypeStruct((s, n), dt) for n, dt in outs]
        + [jax.ShapeDtypeStruct((s // tm, 8, LANES), F32)],
        compiler_params=_cparams("arbitrary"),
    )(x, mod8, w_bf, b.reshape(1, -1), cos, sa, sb)


def _rope_tables(n_tokens):
    rows = n_tokens // GRID_W
    row = jnp.repeat(jnp.arange(rows), GRID_W).astype(F32)
    col = jnp.tile(jnp.arange(GRID_W), rows).astype(F32)
    inv = 1.0 / (ROPE_BASE ** (jnp.arange(ROPE_PAIRS_PER_AXIS, dtype=F32) / ROPE_PAIRS_PER_AXIS))
    ang = jnp.concatenate([row[:, None] * inv, col[:, None] * inv], -1)
    cos, sin = jnp.cos(ang), jnp.sin(ang)
    zero = jnp.zeros_like(sin)
    cos_t = jnp.tile(cos, (1, 4))
    sa = jnp.tile(jnp.concatenate([-sin, zero], -1), (1, 2))
    sb = jnp.tile(jnp.concatenate([zero, sin], -1), (1, 2))
    return cos_t, sa, sb


def _dft_mats(n):
    k = np.arange(n)
    ang = 2.0 * np.pi * ((k[:, None] * k[None, :]) % n) / n
    return np.cos(ang), np.sin(ang)


def _channel_dft(scale):
    c, s = _dft_mats(FOURIER_GROUP)
    eye = np.eye(N_FOURIER_GROUPS)
    return (jnp.asarray(np.kron(eye, c) * scale, BF16), jnp.asarray(np.kron(eye, s) * scale, BF16))


def _fourier1_body(m_ref, x_ref, o_ref):
    o_ref[...] = _dot(m_ref[...], x_ref[...]).astype(BF16)


def _fourier2_body(hr_ref, hi_ref, tc_ref, ts_ref, m2_ref, cc_ref, sc_ref, wf_ref, o_ref, *, n2):
    tc_all, ts_all = tc_ref[0], ts_ref[0]
    for j in range(8):
        hr = hr_ref[j * n2:(j + 1) * n2, :].astype(F32)
        hi = hi_ref[j * n2:(j + 1) * n2, :].astype(F32)
        tc, ts = tc_all[:, j:j + 1], ts_all[:, j:j + 1]
        gr = hr * tc + hi * ts
        gi = hi * tc - hr * ts
        g = jnp.concatenate([gr, gi], axis=0).astype(BF16)
        y = _dot(m2_ref[...], g)
        z = _dot(y[:n2].astype(BF16), cc_ref[...]) + _dot(y[n2:].astype(BF16), sc_ref[...])
        o_ref[:, j, :] = _dot(z.astype(BF16), wf_ref[...]).astype(BF16)


def _fourier_fourstep(f, wf_bf):
    l, ch = f.shape
    n2 = l // DFT1
    c1, s1 = _dft_mats(DFT1)
    m1 = jnp.asarray(np.concatenate([c1, -s1], 0), BF16)
    tn = min(8192, n2 * ch)
    h = pl.pallas_call(
        _fourier1_body,
        grid=(n2 * ch // tn,),
        in_specs=[_resident((2 * DFT1, DFT1)), pl.BlockSpec((DFT1, tn), lambda j: (0, j))],
        out_specs=pl.BlockSpec((2 * DFT1, tn), lambda j: (0, j)),
        out_shape=jax.ShapeDtypeStruct((2 * DFT1, n2 * ch), BF16),
        compiler_params=_cparams("arbitrary"),
    )(m1, f.reshape(DFT1, n2 * ch))
    h = h.reshape(2 * DFT1 * n2, ch)

    k1 = np.arange(DFT1)
    nn = np.arange(n2)
    ang = 2.0 * np.pi * ((nn[:, None] * k1[None, :]) % l) / l
    tc = jnp.asarray(np.cos(ang).reshape(n2, DFT1 // 8, 8).transpose(1, 0, 2), F32)
    ts = jnp.asarray(np.sin(ang).reshape(n2, DFT1 // 8, 8).transpose(1, 0, 2), F32)
    c2, s2 = _dft_mats(n2)
    m2 = jnp.asarray(np.block([[c2, s2], [-s2, c2]]), BF16)
    cc, sc = _channel_dft(1.0 / math.sqrt(l * FOURIER_GROUP))
    nb = DFT1 // 8
    out = pl.pallas_call(
        functools.partial(_fourier2_body, n2=n2),
        grid=(nb,),
        in_specs=[pl.BlockSpec((8 * n2, ch), lambda b: (b, 0)),
                  pl.BlockSpec((8 * n2, ch), lambda b: (b + nb, 0)),
                  pl.BlockSpec((1, n2, 8), lambda b: (b, 0, 0)),
                  pl.BlockSpec((1, n2, 8), lambda b: (b, 0, 0)),
                  _resident((2 * n2, 2 * n2)), _resident((ch, ch)), _resident((ch, ch)), _resident((ch, ch))],
        out_specs=pl.BlockSpec((n2, 8, ch), lambda b: (0, b, 0)),
        out_shape=jax.ShapeDtypeStruct((n2, DFT1, ch), BF16),
        compiler_params=_cparams("arbitrary"),
    )(h, h, tc, ts, m2, cc, sc, wf_bf)
    return out.reshape(l, ch)


def _fourier_dense_body(f_ref, cc_ref, sc_ref, mp_ref, wf_ref, o_ref):
    f = f_ref[...]
    a = jnp.concatenate([_dot(f, cc_ref[...]), _dot(f, sc_ref[...])], axis=0).astype(BF16)
    z = _dot(mp_ref[...], a)
    o_ref[...] = _dot(z.astype(BF16), wf_ref[...]).astype(BF16)


def _fourier_dense(f, wf_bf):
    l, ch = f.shape
    cl, sl = _dft_mats(l)
    mp = jnp.asarray(np.concatenate([cl, -sl], 1), BF16)
    cc, sc = _channel_dft(1.0 / math.sqrt(l * FOURIER_GROUP))
    return pl.pallas_call(
        _fourier_dense_body,
        grid=(1,),
        in_specs=[_resident((l, ch)), _resident((ch, ch)), _resident((ch, ch)), _resident((l, 2 * l)),
                  _resident((ch, ch))],
        out_specs=pl.BlockSpec((l, ch), lambda i: (0, 0)),
        out_shape=jax.ShapeDtypeStruct((l, ch), BF16),
        compiler_params=_cparams("arbitrary"),
    )(f, cc, sc, mp, wf_bf)


CONV_HALO = 16
CONV_ROWS = 64


def _conv_body(prev_ref, cur_ref, next_ref, cw_ref, vec_ref, wpw_ref, o_ref, ubuf, *, tm, nt):
    t = pl.program_id(0)
    ubuf[0:CONV_HALO, :] = jnp.where(t > 0, prev_ref[...].astype(F32), 0.0)
    ubuf[CONV_HALO:CONV_HALO + tm, :] = cur_ref[...].astype(F32)
    ubuf[CONV_HALO + tm:2 * CONV_HALO + tm, :] = jnp.where(t < nt - 1, next_ref[...].astype(F32), 0.0)
    cb, lg, lb = vec_ref[0:1, :], vec_ref[1:2, :], vec_ref[2:3, :]
    first = CONV_HALO - CONV_PAD
    for c in range(tm // CONV_ROWS):
        base = c * CONV_ROWS + first
        acc = cw_ref[0:1, :] * ubuf[base:base + CONV_ROWS, :]
        for j in range(1, CONV_WIDTH):
            acc = acc + cw_ref[j:j + 1, :] * ubuf[base + j:base + j + CONV_ROWS, :]
        y = _ln(acc + cb) * lg + lb
        y = y * jax.nn.sigmoid(y)
        o_ref[c * CONV_ROWS:(c + 1) * CONV_ROWS, :] = _dot(y.astype(BF16), wpw_ref[...]).astype(BF16)


def _conv(u, conv_w, conv_b, ln_g, ln_b, wpw_bf, *, tm):
    s, ch = u.shape
    nt = s // tm
    hb = tm // CONV_HALO
    nh = s // CONV_HALO
    cw = jnp.zeros((32, ch), F32).at[:CONV_WIDTH].set(conv_w)
    vec = jnp.zeros((8, ch), F32).at[0].set(conv_b).at[1].set(ln_g).at[2].set(ln_b)
    return pl.pallas_call(
        functools.partial(_conv_body, tm=tm, nt=nt),
        grid=(nt,),
        in_specs=[pl.BlockSpec((CONV_HALO, ch), lambda t: (jnp.maximum(t * hb - 1, 0), 0)),
                  pl.BlockSpec((tm, ch), lambda t: (t, 0)),
                  pl.BlockSpec((CONV_HALO, ch), lambda t: (jnp.minimum((t + 1) * hb, nh - 1), 0)),
                  _resident((32, ch)), _resident((8, ch)), _resident((ch, ch))],
        out_specs=pl.BlockSpec((tm, ch), lambda t: (t, 0)),
        out_shape=jax.ShapeDtypeStruct((s, ch), BF16),
        scratch_shapes=[pltpu.VMEM((tm + 2 * CONV_HALO, ch), F32)],
        compiler_params=_cparams("arbitrary"),
    )(u, u, u, cw, vec, wpw_bf)


def _split8(x):
    hi = x.astype(F8).astype(F32)
    return hi, (x - hi).astype(F8).astype(F32)


def _prep_q_body(sc_ref, q_ref, o_ref):
    hi, lo = _split8(q_ref[...].astype(F32) * sc_ref[0])
    first = lax.broadcasted_iota(jnp.int32, hi.shape, 1) < ATT_QK_DIM
    hl = (jnp.where(first, hi, pltpu.roll(lo, ATT_QK_DIM, 1)),
          jnp.where(first, pltpu.roll(hi, ATT_QK_DIM, 1), lo))
    for c in range(2):
        t = hl[c].T.astype(F8)
        o_ref[0, c, 0:LANES, :] = t
        o_ref[0, c, LANES:2 * LANES, :] = t


def _prep_k_body(sc_ref, k_ref, o_ref):
    hi, lo = _split8(k_ref[...].astype(F32) * sc_ref[0])
    first = lax.broadcasted_iota(jnp.int32, hi.shape, 1) < ATT_QK_DIM
    his, los = pltpu.roll(hi, ATT_QK_DIM, 1), pltpu.roll(lo, ATT_QK_DIM, 1)
    o_ref[0, 0, :, 0:LANES] = jnp.where(first, hi, his).astype(F8)
    o_ref[0, 0, :, LANES:2 * LANES] = jnp.where(first, lo, los).astype(F8)
    o_ref[0, 1, :, 0:LANES] = jnp.where(first, his, hi).astype(F8)
    o_ref[0, 1, :, LANES:2 * LANES] = jnp.where(first, los, lo).astype(F8)


def _prep_v_body(v_ref, o_ref):
    o_ref[0, 0, 0:ATT_V_DIM, :] = v_ref[...].astype(F32).T.astype(BF16)
    row = lax.broadcasted_iota(jnp.int32, (ATT_V_ROWS - ATT_V_DIM, v_ref.shape[0]), 0)
    o_ref[0, 0, ATT_V_DIM:ATT_V_ROWS, :] = jnp.where(row == 0, 1.0, 0.0).astype(BF16)


def _pow2_scale(mx):
    _, e = jnp.frexp(mx)
    return jnp.ldexp(jnp.float32(1.0), 5 - e).astype(F32)


def _attn_prep(q, k_all, v_all, qmax, kmax, *, tk, tm):
    s, lk = q.shape[0], k_all.shape[0]
    nk = lk // tk
    aq, ak = _pow2_scale(qmax), _pow2_scale(kmax)
    smem = pl.BlockSpec(memory_space=pltpu.SMEM)
    head_rows = lambda t: pl.BlockSpec((t, LANES), lambda h, i: (i, h))
    qt8 = pl.pallas_call(
        _prep_q_body,
        grid=(N_ATT_HEADS, s // tm),
        in_specs=[smem, head_rows(tm)],
        out_specs=pl.BlockSpec((1, 2, 2 * LANES, tm), lambda h, i: (h, 0, 0, i)),
        out_shape=jax.ShapeDtypeStruct((N_ATT_HEADS, 2, 2 * LANES, s), F8),
        compiler_params=_cparams("arbitrary", "arbitrary"),
    )(aq.reshape(1), q)
    tmk = _pick(lk, (1280, 256))
    k8 = pl.pallas_call(
        _prep_k_body,
        grid=(N_ATT_HEADS, lk // tmk),
        in_specs=[smem, head_rows(tmk)],
        out_specs=pl.BlockSpec((1, 2, tmk, 2 * LANES), lambda h, i: (h, 0, i, 0)),
        out_shape=jax.ShapeDtypeStruct((N_ATT_HEADS, 2, lk, 2 * LANES), F8),
        compiler_params=_cparams("arbitrary", "arbitrary"),
    )(ak.reshape(1), k_all)
    vt = pl.pallas_call(
        _prep_v_body,
        grid=(N_ATT_HEADS, nk),
        in_specs=[head_rows(tk)],
        out_specs=pl.BlockSpec((1, 1, ATT_V_ROWS, tk), lambda h, i: (h, i, 0, 0)),
        out_shape=jax.ShapeDtypeStruct((N_ATT_HEADS, nk, ATT_V_ROWS, tk), BF16),
        compiler_params=_cparams("arbitrary", "arbitrary"),
    )(v_all)
    return qt8, k8.reshape(N_ATT_HEADS, 2, nk, tk, 2 * LANES), vt, (1.0 / (aq * ak)).reshape(1)


def _attn_body(c_ref, lam_ref, g_ref, qt_ref, k_ref, vt_ref, o_ref, acc1, acc2, s1_buf, s2_buf,
               *, nk, tq, lam_init):
    cs = c_ref[0]
    cs_bf = cs.astype(BF16)
    acc1[...] = jnp.zeros_like(acc1)
    acc2[...] = jnp.zeros_like(acc2)

    tk = k_ref.shape[3]
    bufs, qs, accs = (s1_buf, s2_buf), (qt_ref[0, 0], qt_ref[0, 1]), (acc1, acc2)

    def chunk_step(i, slot, mx, m, *, consume, produce):
        if consume:
            mn = [jnp.maximum(m[c], mx[c]) for c in range(2)]
        new_mx, pv = [None, None], [None, None]
        for r in range(tk // ATT_SUB):
            rows = slice(r * ATT_SUB, (r + 1) * ATT_SUB)
            if produce:
                for c in range(2):
                    s = _dot(k_ref[0, c, i + 1, rows, :], qs[c])
                    bufs[c][1 - slot, rows, :] = s
                    smax = jnp.max(s, axis=0, keepdims=True)
                    new_mx[c] = smax if new_mx[c] is None else jnp.maximum(new_mx[c], smax)
            if consume:
                vt = vt_ref[0, i, :, rows]
                for c in range(2):
                    p = jnp.exp2((bufs[c][slot, rows, :] - mn[c]).astype(BF16) * cs_bf)
                    d = _dot(vt, p)
                    pv[c] = d if pv[c] is None else pv[c] + d
        if consume:
            for c in range(2):
                accs[c][...] = jnp.exp2((m[c] - mn[c]) * cs) * accs[c][...] + pv[c]
            m = tuple(mn)
        return tuple(new_mx), m

    def step(i, carry):
        mx, m = carry
        return lax.cond(i % 2 == 0,
                        lambda: chunk_step(i, 0, mx, m, consume=True, produce=True),
                        lambda: chunk_step(i, 1, mx, m, consume=True, produce=True))

    neg = jnp.full((1, tq), -jnp.inf, F32)
    mx0, _ = chunk_step(-1, 1, None, None, consume=False, produce=True)
    mx, m = lax.fori_loop(0, nk - 1, step, (mx0, (neg, neg)))
    chunk_step(nk - 1, (nk - 1) % 2, mx, m, consume=True, produce=False)

    lp = lam_ref[...]
    lam = (jnp.exp(jnp.sum(lp[0:1] * lp[1:2], axis=-1, keepdims=True))
           - jnp.exp(jnp.sum(lp[2:3] * lp[3:4], axis=-1, keepdims=True)) + lam_init)
    dv = ATT_V_DIM
    o = acc1[0:dv, :] / acc1[dv:dv + 1, :] - lam * (acc2[0:dv, :] / acc2[dv:dv + 1, :])
    ms = jnp.mean(o * o, axis=0, keepdims=True)
    o = o * lax.rsqrt(ms + LN_EPS) * g_ref[...] * (1.0 - lam_init)
    o_ref[...] = o.T.astype(BF16)


def _attention(q, k_all, v_all, qmax, kmax, lam_p, g, *, lam_init, tq, tk):
    s = q.shape[0]
    nk = k_all.shape[0] // tk
    qt8, k8, vt, c = _attn_prep(q, k_all, v_all, qmax, kmax, tk=tk, tm=tq)
    per_head = lambda shape: pl.BlockSpec(shape, lambda h, i: (h,) + (0,) * (len(shape) - 1),
                                          pipeline_mode=pl.Buffered(1))
    return pl.pallas_call(
        functools.partial(_attn_body, nk=nk, tq=tq, lam_init=lam_init),
        grid=(N_ATT_HEADS, s // tq),
        in_specs=[pl.BlockSpec(memory_space=pltpu.SMEM),
                  pl.BlockSpec((8, LANES), lambda h, i: (0, 0)),
                  pl.BlockSpec((ATT_V_DIM, 1), lambda h, i: (0, 0)),
                  pl.BlockSpec((1, 2, 2 * LANES, tq), lambda h, i: (h, 0, 0, i)),
                  per_head((1, 2, nk, tk, 2 * LANES)),
                  per_head((1, nk, ATT_V_ROWS, tk))],
        out_specs=pl.BlockSpec((tq, ATT_V_DIM), lambda h, i: (i, h)),
        out_shape=jax.ShapeDtypeStruct((s, D_ATT), BF16),
        scratch_shapes=[pltpu.VMEM((ATT_V_ROWS, tq), F32), pltpu.VMEM((ATT_V_ROWS, tq), F32)]
        + [pltpu.VMEM((2, tk, tq), F32) for _ in range(2)],
        compiler_params=_cparams("arbitrary", "arbitrary"),
    )(c, lam_p, g.reshape(ATT_V_DIM, 1), qt8, k8, vt)


def _route(logits):
    lane = lax.broadcasted_iota(jnp.int32, logits.shape, 1)
    lane_f = lane.astype(F32)
    is_g = lane < N_GROUPS
    gl = jnp.where(is_g, logits, NEG_BIG)
    gmax = jnp.max(gl, axis=-1, keepdims=True)
    gidx = jnp.min(jnp.where(gl == gmax, lane_f, float(ROUTER_LANES)), axis=-1, keepdims=True)
    wg = 1.0 / jnp.sum(jnp.where(is_g, jnp.exp(gl - gmax), 0.0), axis=-1, keepdims=True)
    grp = ((lane - N_GROUPS) // EXPERTS_PER_GROUP).astype(F32)
    valid = (lane >= N_GROUPS) & (lane < N_GROUPS + N_EXPERTS) & (grp == gidx)
    el = jnp.where(valid, logits, NEG_BIG)
    m1 = jnp.max(el, axis=-1, keepdims=True)
    i1 = jnp.min(jnp.where(el == m1, lane_f, float(ROUTER_LANES)), axis=-1, keepdims=True)
    el2 = jnp.where(lane_f == i1, NEG_BIG, el)
    m2 = jnp.max(el2, axis=-1, keepdims=True)
    i2 = jnp.min(jnp.where(el2 == m2, lane_f, float(ROUTER_LANES)), axis=-1, keepdims=True)
    t = jnp.exp(m2 - m1)
    w1 = wg / (1.0 + t)
    w2 = wg * t / (1.0 + t)
    e1 = (i1 - N_GROUPS).astype(jnp.int32)
    e2 = (i2 - N_GROUPS).astype(jnp.int32)
    eidx = jnp.where(lane == 0, e1, jnp.where(lane == 1, e2, 0))
    wts = jnp.where(lane == 0, w1, jnp.where(lane == 1, w2, 0.0))
    return eidx, wts


def _outproj_body(x_ref, f_ref, c_ref, a_ref, w_ref, mod_ref, ln_ref, wr_ref, br_ref,
                  xo_ref, h2_ref, ei_ref, wt_ref):
    mix = (_dot(f_ref[...], w_ref[0:D_FOURIER, :])
           + _dot(c_ref[...], w_ref[D_FOURIER:D_FOURIER + D_CONV, :])
           + _dot(a_ref[...], w_ref[D_FOURIER + D_CONV:D_MIX, :]))
    r = ALPHA * x_ref[...] + mod_ref[2:3, :] * mix
    xn = _ln(r) * ln_ref[0:1, :] + ln_ref[1:2, :]
    xo_ref[...] = xn
    h2 = _ln(xn) * (1.0 + mod_ref[4:5, :]) + mod_ref[3:4, :]
    h2_ref[...] = h2
    logits = _dot(h2.astype(BF16), wr_ref[...]) + br_ref[...]
    eidx, wts = _route(logits)
    ei_ref[...] = eidx
    wt_ref[...] = wts


def _outproj(x, four, conv, att, w_out_bf, mod8, ln8, wr_bf, br, *, tm):
    s, d = x.shape
    row = lambda n: pl.BlockSpec((tm, n), lambda i: (i, 0))
    return pl.pallas_call(
        _outproj_body,
        grid=(s // tm,),
        in_specs=[row(d), row(D_FOURIER), row(D_CONV), row(D_ATT), _resident((D_MIX, d)), _resident((8, d)),
                  _resident((8, d)), _resident((d, ROUTER_LANES)), _resident((1, ROUTER_LANES))],
        out_specs=[row(d), row(d), row(ROUTER_LANES), row(ROUTER_LANES)],
        out_shape=[jax.ShapeDtypeStruct((s, d), F32), jax.ShapeDtypeStruct((s, d), F32),
                   jax.ShapeDtypeStruct((s, ROUTER_LANES), jnp.int32),
                   jax.ShapeDtypeStruct((s, ROUTER_LANES), F32)],
        compiler_params=_cparams("arbitrary"),
    )(x, four, conv, att, w_out_bf, mod8, ln8, wr_bf, br)


TOK_TILE = 256
COMBINE_GROUPS = 8


def _moe_plan(eidx2, tb):
    n = eidx2.shape[0]
    nb = -(-(n * TOP_K) // tb) + N_EXPERTS
    onehot = (eidx2[:, :, None] == jnp.arange(N_EXPERTS, dtype=jnp.int32)).astype(jnp.int32)
    per_tok = onehot.sum(axis=1)
    incl = jnp.cumsum(per_tok, axis=0)
    counts = incl[-1]
    padded = (counts + tb - 1) // tb * tb
    pad_end = jnp.cumsum(padded)
    rank_base = (incl - per_tok) + (pad_end - padded)[None, :]
    dest = jnp.sum(onehot * rank_base[:, None, :], axis=-1).astype(jnp.int32)
    block_row = jnp.arange(nb, dtype=jnp.int32) * tb
    block_e = jnp.minimum(jnp.sum(pad_end[None, :] <= block_row[:, None], axis=1), N_EXPERTS - 1).astype(jnp.int32)
    first = jnp.concatenate([jnp.ones((1,), jnp.int32), (block_e[1:] != block_e[:-1]).astype(jnp.int32)])
    nused = (pad_end[-1] // tb).astype(jnp.int32).reshape(1)
    zrow = jnp.maximum(pad_end - tb, 0).astype(jnp.int32)
    zflag = (counts > 0).astype(jnp.int32)
    return dest, block_e, first, nused, zrow, zflag, nb


def _dispatch_body(zrow_ref, zflag_ref, nused_ref, dest_ref, h_ref, xs_hbm, zbuf, stage, sem, zsem,
                   *, tb, nt, nb):
    i = pl.program_id(0)

    def zero_copy(e):
        return pltpu.make_async_copy(zbuf, xs_hbm.at[pl.ds(pl.multiple_of(zrow_ref[e], tb), tb)], zsem)

    @pl.when(i == 0)
    def _():
        zbuf[...] = jnp.zeros_like(zbuf)
        for e in range(N_EXPERTS):
            @pl.when(zflag_ref[e] == 1)
            def _():
                zero_copy(e).start()
        for e in range(N_EXPERTS):
            @pl.when(zflag_ref[e] == 1)
            def _():
                zero_copy(e).wait()

        def tail_copy(j):
            return pltpu.make_async_copy(zbuf, xs_hbm.at[pl.ds(pl.multiple_of(j * tb, tb), tb)], zsem)

        def start_tail(j, c):
            tail_copy(j).start()
            return c

        def wait_tail(j, c):
            tail_copy(j).wait()
            return c
        lax.fori_loop(nused_ref[0], nb, start_tail, 0)
        lax.fori_loop(nused_ref[0], nb, wait_tail, 0)

    slot = i % 2

    def wait_tile(sl):
        for _ in range(TOP_K):
            pltpu.make_async_copy(stage.at[sl], xs_hbm.at[pl.ds(0, TOK_TILE)], sem.at[sl]).wait()

    @pl.when(i >= 2)
    def _():
        wait_tile(slot)

    stage[slot] = h_ref[...]

    def body(r, c):
        src = stage.at[slot, pl.ds(r, 1)]
        for k in range(TOP_K):
            row = dest_ref[0, 0, TOP_K * r + k]
            pltpu.make_async_copy(src, xs_hbm.at[pl.ds(row, 1)], sem.at[slot]).start()
        return c
    lax.fori_loop(0, TOK_TILE, body, 0, unroll=8)

    @pl.when(i == nt - 1)
    def _():
        wait_tile(slot)
        if nt >= 2:
            wait_tile(1 - slot)


def _dispatch(h2, dest, zrow, zflag, nused, *, tb, nb):
    n, d = h2.shape
    nt = n // TOK_TILE
    grid_spec = pltpu.PrefetchScalarGridSpec(
        num_scalar_prefetch=3,
        grid=(nt,),
        in_specs=[pl.BlockSpec((1, 1, TOP_K * TOK_TILE), lambda i, *_: (i, 0, 0), memory_space=pltpu.SMEM),
                  pl.BlockSpec((TOK_TILE, d), lambda i, *_: (i, 0))],
        out_specs=pl.BlockSpec(memory_space=pl.ANY),
        scratch_shapes=[pltpu.VMEM((tb, d), F32), pltpu.VMEM((2, TOK_TILE, d), F32),
                        pltpu.SemaphoreType.DMA((2,)), pltpu.SemaphoreType.DMA(())],
    )
    return pl.pallas_call(
        functools.partial(_dispatch_body, tb=tb, nt=nt, nb=nb),
        grid_spec=grid_spec,
        out_shape=jax.ShapeDtypeStruct((nb * tb, d), F32),
        compiler_params=_cparams("arbitrary"),
    )(zrow, zflag, nused, dest.reshape(nt, 1, TOP_K * TOK_TILE), h2)


def _experts_body(be_ref, first_ref, nused_ref, x_ref, wg_ref, wu_ref, wd_ref, o_ref, wgb, wub, wdb):
    del be_ref
    i = pl.program_id(0)

    @pl.when(i < nused_ref[0])
    def _():
        @pl.when(first_ref[i] == 1)
        def _():
            wgb[...] = wg_ref[0, 0].astype(BF16)
            wub[...] = wu_ref[0, 0].astype(BF16)
            wdb[...] = wd_ref[0, 0].astype(BF16)

        x = x_ref[...].astype(BF16)
        g = _dot(x, wgb[...])
        u = _dot(x, wub[...])
        mid = (g * jax.nn.sigmoid(g) * u).astype(BF16)
        o_ref[...] = _dot(mid, wdb[...])

    @pl.when(i >= nused_ref[0])
    def _():
        o_ref[...] = jnp.zeros_like(o_ref)


def _experts(xs, block_e, first, nused, w_gate, w_up, w_down, layer, *, tb, nb):
    d = xs.shape[1]
    wspec = lambda shape: pl.BlockSpec((1,) + shape, lambda i, be, first, nused: (layer, be[i], 0, 0))
    xspec = pl.BlockSpec((tb, d), lambda i, be, first, nused: (jnp.minimum(i, nused[0] - 1), 0))
    grid_spec = pltpu.PrefetchScalarGridSpec(
        num_scalar_prefetch=3,
        grid=(nb,),
        in_specs=[xspec, wspec((1, d, D_EXPERT)), wspec((1, d, D_EXPERT)), wspec((1, D_EXPERT, d))],
        out_specs=pl.BlockSpec((tb, d), lambda i, *_: (i, 0)),
        scratch_shapes=[pltpu.VMEM((d, D_EXPERT), BF16), pltpu.VMEM((d, D_EXPERT), BF16),
                        pltpu.VMEM((D_EXPERT, d), BF16)],
    )
    return pl.pallas_call(
        _experts_body,
        grid_spec=grid_spec,
        out_shape=jax.ShapeDtypeStruct((nb * tb, d), F32),
        compiler_params=_cparams("arbitrary"),
    )(block_e, first, nused, xs, w_gate, w_up, w_down)


def _combine_body(dcur_ref, dnext_ref, x_ref, wt_ref, mod_ref, ln_ref, ys_hbm, o_ref, ybuf, sem, *, nt):
    i = pl.program_id(0)

    slot = i % 2

    def start_row(d_ref, sl, r):
        for k in range(TOP_K):
            row = d_ref[0, 0, TOP_K * r + k]
            pltpu.make_async_copy(ys_hbm.at[pl.ds(row, 1)], ybuf.at[sl, k, pl.ds(r, 1)], sem.at[sl]).start()

    def wait_tile(sl):
        for k in range(TOP_K):
            pltpu.make_async_copy(ys_hbm.at[pl.ds(0, TOK_TILE)], ybuf.at[sl, k], sem.at[sl]).wait()

    @pl.when(i == 0)
    def _():
        def body(r, c):
            start_row(dcur_ref, 0, r)
            return c
        lax.fori_loop(0, TOK_TILE, body, 0)

    wait_tile(slot)
    rows_g = TOK_TILE // COMBINE_GROUPS
    for g in range(COMBINE_GROUPS):
        rs = slice(g * rows_g, (g + 1) * rows_g)
        w = wt_ref[rs, :]
        y = w[:, 0:1] * ybuf[slot, 0, rs, :] + w[:, 1:2] * ybuf[slot, 1, rs, :]
        r = ALPHA * x_ref[rs, :] + mod_ref[5:6, :] * y
        o_ref[rs, :] = _ln(r) * ln_ref[0:1, :] + ln_ref[1:2, :]
        for rr in range(g * rows_g, (g + 1) * rows_g):
            start_row(dnext_ref, 1 - slot, rr)

    @pl.when(i == nt - 1)
    def _():
        wait_tile(1 - slot)


def _combine(x, ys, dest, wt, mod8, ln8, *, row0):
    s, d = x.shape
    nt = s // TOK_TILE
    t0 = row0 // TOK_TILE
    dest3 = dest.reshape(-1, 1, TOP_K * TOK_TILE)
    dspec = lambda off: pl.BlockSpec((1, 1, TOP_K * TOK_TILE),
                                     lambda i: (t0 + jnp.minimum(i + off, nt - 1), 0, 0), memory_space=pltpu.SMEM)
    return pl.pallas_call(
        functools.partial(_combine_body, nt=nt),
        grid=(nt,),
        in_specs=[dspec(0), dspec(1),
                  pl.BlockSpec((TOK_TILE, d), lambda i: (i, 0)),
                  pl.BlockSpec((TOK_TILE, ROUTER_LANES), lambda i: (i, 0)),
                  _resident((8, d)), _resident((8, d)),
                  pl.BlockSpec(memory_space=pl.ANY)],
        out_specs=pl.BlockSpec((TOK_TILE, d), lambda i: (i, 0)),
        out_shape=jax.ShapeDtypeStruct((s, d), F32),
        scratch_shapes=[pltpu.VMEM((2, TOP_K, TOK_TILE, d), F32), pltpu.SemaphoreType.DMA((2,))],
        compiler_params=_cparams("arbitrary"),
    )(dest3, dest3, x, wt, mod8, ln8, ys)


def _moe(h2, eidx2, w_gate, w_up, w_down, layer):
    tb = MOE_ROWS
    dest, block_e, first, nused, zrow, zflag, nb = _moe_plan(eidx2, tb)
    xs = _dispatch(h2, dest, zrow, zflag, nused, tb=tb, nb=nb)
    ys = _experts(xs, block_e, first, nused, w_gate, w_up, w_down, layer, tb=tb, nb=nb)
    return ys, dest


def _pick(n, prefs):
    for t in prefs:
        if n % t == 0:
            return t
    raise ValueError(f"no tile of {prefs} divides {n}")


def _rows8(*vecs):
    d = vecs[0].shape[-1]
    out = jnp.zeros((8, d), F32)
    for j, v in enumerate(vecs):
        out = out.at[j].set(v)
    return out


def kernel(x, c, ctx, c_ctx, w_mod, b_mod, w_in, b_in, w_fourier, conv_w, conv_b, conv_ln_g, conv_ln_b, w_pw,
           lam_q1, lam_k1, lam_q2, lam_k2, subln_g, w_out, ln_a_g, ln_a_b, w_rg, b_rg, w_re, b_re,
           w_gate, w_up, w_down, ln_f_g, ln_f_b):
    b, s, d = x.shape
    nc = ctx.shape[1]
    assert b == 1 and d == D_MODEL and c.shape[0] == 1 and ctx.shape[0] == 1
    assert s % (DFT1 * 8) == 0 and nc % TOK_TILE == 0
    xl, xc = x[0], ctx[0]
    depth = w_mod.shape[0]

    c2 = jnp.zeros((8, d), F32).at[0].set(c[0]).at[1].set(c_ctx)
    mod_all = _modulation(c2, w_mod, b_mod)
    tables_l = _rope_tables(s)
    tables_c = tuple(jnp.zeros((nc, LANES), F32) for _ in range(3))
    tm_l = _pick(s, (512, 256))
    tm_c = _pick(nc, (256,))
    tq = _pick(s, (512, 256))
    tk = _pick(s + nc, (3328, 1280, 256))

    for i in range(depth):
        last = i == depth - 1
        lam_init = 0.8 - 0.6 * math.exp(-0.3 * i)
        mod_l = _rows8(*jnp.split(mod_all[i, 0], N_MOD))
        mod_c = _rows8(*jnp.split(mod_all[i, 1], N_MOD))
        w_in_bf = w_in[i].astype(BF16)
        wf_bf = w_fourier[i].astype(BF16)
        wpw_bf = w_pw[i].astype(BF16)
        w_out_bf = w_out[i].astype(BF16)
        wr = jnp.zeros((d, ROUTER_LANES), F32).at[:, :N_GROUPS].set(w_rg[i])
        wr_bf = wr.at[:, N_GROUPS:N_GROUPS + N_EXPERTS].set(w_re[i]).astype(BF16)
        br = jnp.zeros((1, ROUTER_LANES), F32).at[0, :N_GROUPS].set(b_rg[i])
        br = br.at[0, N_GROUPS:N_GROUPS + N_EXPERTS].set(b_re[i])
        lam_p = jnp.zeros((8, LANES), F32).at[0, :ATT_QK_DIM].set(lam_q1[i]).at[1, :ATT_QK_DIM].set(lam_k1[i])
        lam_p = lam_p.at[2, :ATT_QK_DIM].set(lam_q2[i]).at[3, :ATT_QK_DIM].set(lam_k2[i])
        ln_a = _rows8(ln_a_g[i], ln_a_b[i])
        ln_f = _rows8(ln_f_g[i], ln_f_b[i])
        conv_args = (conv_w[i], conv_b[i], conv_ln_g[i], conv_ln_b[i], wpw_bf)

        f_l, u_l, q_l, k_l, v_l, st_l = _inproj(xl, mod_l, w_in_bf, b_in[i], tables_l, rope=True, tm=tm_l)
        f_c, u_c, q_c, k_c, v_c, st_c = _inproj(xc, mod_c, w_in_bf, b_in[i], tables_c, rope=False, tm=tm_c)
        qmax_l, qmax_c = jnp.max(st_l[:, 0, 0]), jnp.max(st_c[:, 0, 0])
        kmax_c = jnp.max(st_c[:, 1, 0])
        kmax_all = jnp.maximum(jnp.max(st_l[:, 1, 0]), kmax_c)
        k_all = jnp.concatenate([k_c, k_l], axis=0)
        v_all = jnp.concatenate([v_c, v_l], axis=0)
        att_l = _attention(q_l, k_all, v_all, qmax_l, kmax_all, lam_p, subln_g[i],
                           lam_init=lam_init, tq=tq, tk=tk)
        four_l = _fourier_fourstep(f_l, wf_bf)
        conv_l = _conv(u_l, *conv_args, tm=tm_l)
        xl, h2_l, ei_l, wt_l = _outproj(xl, four_l, conv_l, att_l, w_out_bf, mod_l, ln_a, wr_bf, br, tm=tm_l)
        if last:
            ys, dest = _moe(h2_l, ei_l[:, :TOP_K], w_gate, w_up, w_down, i)
            xl = _combine(xl, ys, dest, wt_l, mod_l, ln_f, row0=0)
        else:
            att_c = _attention(q_c, k_c, v_c, qmax_c, kmax_c, lam_p, subln_g[i],
                               lam_init=lam_init, tq=tm_c, tk=tm_c)
            four_c = _fourier_dense(f_c, wf_bf)
            conv_c = _conv(u_c, *conv_args, tm=tm_c)
            xc, h2_c, ei_c, wt_c = _outproj(xc, four_c, conv_c, att_c, w_out_bf, mod_c, ln_a, wr_bf, br, tm=tm_c)
            h2 = jnp.concatenate([h2_c, h2_l], axis=0)
            ei = jnp.concatenate([ei_c[:, :TOP_K], ei_l[:, :TOP_K]], axis=0)
            ys, dest = _moe(h2, ei, w_gate, w_up, w_down, i)
            xc = _combine(xc, ys, dest, wt_c, mod_c, ln_f, row0=0)
            xl = _combine(xl, ys, dest, wt_l, mod_l, ln_f, row0=nc)
    return xl[None]
```

```python
import functools
import math

import numpy as np
import jax
import jax.numpy as jnp
from jax import lax
from jax.experimental import pallas as pl
from jax.experimental.pallas import tpu as pltpu

F32 = jnp.float32
BF16 = jnp.bfloat16
F8 = jnp.float8_e4m3fn

D_MODEL = 2048
DEPTH = 2
GRID_W = 64
D_FOURIER = 512
N_FOURIER_GROUPS = 4
FOURIER_GROUP = 128
D_CONV = 512
CONV_WIDTH = 31
CONV_PAD = CONV_WIDTH // 2
N_ATT_HEADS = 8
ATT_QK_DIM = 64
ATT_V_DIM = 128
D_ATT = N_ATT_HEADS * ATT_V_DIM
D_MIX = D_FOURIER + D_CONV + D_ATT
QK_COLS = N_ATT_HEADS * 2 * ATT_QK_DIM
ATT_SCALE = 1.0 / math.sqrt(ATT_QK_DIM)
ROPE_BASE = 10000.0
ROPE_PAIRS_PER_AXIS = ATT_QK_DIM // 4
OFF_CONV = D_FOURIER
OFF_Q = OFF_CONV + 2 * D_CONV
OFF_K = OFF_Q + QK_COLS
OFF_V = OFF_K + QK_COLS
D_IN_PROJ = OFF_V + D_ATT
N_GROUPS = 4
EXPERTS_PER_GROUP = 8
N_EXPERTS = N_GROUPS * EXPERTS_PER_GROUP
TOP_K = 2
D_EXPERT = 512
N_MOD = 6
LN_EPS = 1e-6
ALPHA = (2.0 * DEPTH) ** 0.25

LANES = 128
VMEM_LIMIT = 56 * 1024 * 1024
LOG2E = 1.4426950408889634
Q_SCALE = ATT_SCALE * LOG2E
NEG_BIG = -1e30
DFT1 = 128
MOE_ROWS = 256
ROUTER_LANES = 128
ATT_V_ROWS = ATT_V_DIM + 16
ATT_SUB = 256
def _cparams(*sem):
    return pltpu.CompilerParams(dimension_semantics=tuple(sem), vmem_limit_bytes=VMEM_LIMIT)


def _resident(shape):
    nd = len(shape)
    return pl.BlockSpec(shape, lambda *_: (0,) * nd, pipeline_mode=pl.Buffered(1))


def _ln(x):
    mu = jnp.mean(x, axis=-1, keepdims=True)
    xc = x - mu
    var = jnp.mean(xc * xc, axis=-1, keepdims=True)
    return xc * lax.rsqrt(var + LN_EPS)


def _dot(a, b):
    return jnp.dot(a, b, preferred_element_type=F32)


def _mod_body(c_ref, w_ref, b_ref, o_ref):
    c = c_ref[...]
    sc = (c * jax.nn.sigmoid(c)).astype(BF16)
    o_ref[0] = _dot(sc, w_ref[0].astype(BF16)) + b_ref[0]


def _modulation(c2, w_mod, b_mod):
    nl, d, n = w_mod.shape
    tn = 1024
    return pl.pallas_call(
        _mod_body,
        grid=(nl, n // tn),
        in_specs=[pl.BlockSpec((8, d), lambda l, j: (0, 0)),
                  pl.BlockSpec((1, d, tn), lambda l, j: (l, 0, j)),
                  pl.BlockSpec((1, 1, tn), lambda l, j: (l, 0, j))],
        out_specs=pl.BlockSpec((1, 8, tn), lambda l, j: (l, 0, j)),
        out_shape=jax.ShapeDtypeStruct((nl, 8, n), F32),
        compiler_params=_cparams("arbitrary", "arbitrary"),
    )(c2, w_mod, b_mod.reshape(nl, 1, n))


def _inproj_body(x_ref, mod_ref, w_ref, b_ref, cos_ref, sa_ref, sb_ref,
                 f_ref, u_ref, q_ref, k_ref, v_ref, stat_ref, *, rope):
    h = _ln(x_ref[...]) * (1.0 + mod_ref[1:2, :]) + mod_ref[0:1, :]
    hb = h.astype(BF16)

    def proj(lo, hi):
        return _dot(hb, w_ref[:, lo:hi]) + b_ref[:, lo:hi]

    f_ref[...] = proj(0, OFF_CONV).astype(BF16)
    a = proj(OFF_CONV, OFF_CONV + D_CONV)
    g = proj(OFF_CONV + D_CONV, OFF_Q)
    u_ref[...] = (a * jax.nn.sigmoid(g)).astype(BF16)
    v_ref[...] = proj(OFF_V, D_IN_PROJ).astype(BF16)
    q = proj(OFF_Q, OFF_K)
    k = proj(OFF_K, OFF_V)
    if rope:
        cos, sa, sb = cos_ref[...], sa_ref[...], sb_ref[...]
    qmax = kmax = None
    for hd in range(N_ATT_HEADS):
        sl = slice(hd * LANES, (hd + 1) * LANES)
        qh, kh = q[:, sl], k[:, sl]
        if rope:
            qh = qh * cos + pltpu.roll(qh, LANES - 32, 1) * sa + pltpu.roll(qh, 32, 1) * sb
            kh = kh * cos + pltpu.roll(kh, LANES - 32, 1) * sa + pltpu.roll(kh, 32, 1) * sb
        qb, kb = (qh * Q_SCALE).astype(BF16), kh.astype(BF16)
        q_ref[:, sl] = qb
        k_ref[:, sl] = kb
        qa, ka = jnp.abs(qb.astype(F32)), jnp.abs(kb.astype(F32))
        qmax = qa if qmax is None else jnp.maximum(qmax, qa)
        kmax = ka if kmax is None else jnp.maximum(kmax, ka)
    qm = jnp.max(jnp.max(qmax, axis=0, keepdims=True), axis=1, keepdims=True)
    km = jnp.max(jnp.max(kmax, axis=0, keepdims=True), axis=1, keepdims=True)
    row = lax.broadcasted_iota(jnp.int32, (8, LANES), 0)
    stat_ref[0] = jnp.where(row == 0, qm, jnp.where(row == 1, km, 0.0))


def _inproj(x, mod8, w_bf, b, tables, *, rope, tm):
    s, d = x.shape
    cos, sa, sb = tables
    row = lambda n: pl.BlockSpec((tm, n), lambda i: (i, 0))
    outs = [(D_FOURIER, BF16), (D_CONV, BF16), (QK_COLS, BF16), (QK_COLS, BF16), (D_ATT, BF16)]
    return pl.pallas_call(
        functools.partial(_inproj_body, rope=rope),
        grid=(s // tm,),
        in_specs=[row(d), _resident((8, d)), _resident((d, D_IN_PROJ)), _resident((1, D_IN_PROJ)),
                  row(LANES), row(LANES), row(LANES)],
        out_specs=[row(n) for n, _ in outs] + [pl.BlockSpec((1, 8, LANES), lambda i: (i, 0, 0))],
        out_shape=[jax.ShapeDtypeStruct((s, n), dt) for n, dt in outs]
        + [jax.ShapeDtypeStruct((s // tm, 8, LANES), F32)],
        compiler_params=_cparams("arbitrary"),
    )(x, mod8, w_bf, b.reshape(1, -1), cos, sa, sb)


def _rope_tables(n_tokens):
    rows = n_tokens // GRID_W
    row = jnp.repeat(jnp.arange(rows), GRID_W).astype(F32)
    col = jnp.tile(jnp.arange(GRID_W), rows).astype(F32)
    inv = 1.0 / (ROPE_BASE ** (jnp.arange(ROPE_PAIRS_PER_AXIS, dtype=F32) / ROPE_PAIRS_PER_AXIS))
    ang = jnp.concatenate([row[:, None] * inv, col[:, None] * inv], -1)
    cos, sin = jnp.cos(ang), jnp.sin(ang)
    zero = jnp.zeros_like(sin)
    cos_t = jnp.tile(cos, (1, 4))
    sa = jnp.tile(jnp.concatenate([-sin, zero], -1), (1, 2))
    sb = jnp.tile(jnp.concatenate([zero, sin], -1), (1, 2))
    return cos_t, sa, sb


def _dft_mats(n):
    k = np.arange(n)
    ang = 2.0 * np.pi * ((k[:, None] * k[None, :]) % n) / n
    return np.cos(ang), np.sin(ang)


def _channel_dft(scale):
    c, s = _dft_mats(FOURIER_GROUP)
    eye = np.eye(N_FOURIER_GROUPS)
    return (jnp.asarray(np.kron(eye, c) * scale, BF16), jnp.asarray(np.kron(eye, s) * scale, BF16))


def _fourier1_body(m_ref, x_ref, o_ref):
    ch = o_ref.shape[2]
    res = _dot(m_ref[...], x_ref[...])
    for j in range(o_ref.shape[1]):
        o_ref[:, j, :] = res[:, j * ch:(j + 1) * ch].astype(BF16)


def _fourier2_body(hr_ref, hi_ref, tc_ref, ts_ref, m2_ref, cc_ref, sc_ref, wf_ref, o_ref, *, n2):
    tc_all, ts_all = tc_ref[0], ts_ref[0]
    for j in range(8):
        hr = hr_ref[j].astype(F32)
        hi = hi_ref[j].astype(F32)
        tc, ts = tc_all[:, j:j + 1], ts_all[:, j:j + 1]
        gr = hr * tc + hi * ts
        gi = hi * tc - hr * ts
        g = jnp.concatenate([gr, gi], axis=0).astype(BF16)
        y = _dot(m2_ref[...], g)
        z = _dot(y[:n2].astype(BF16), cc_ref[...]) + _dot(y[n2:].astype(BF16), sc_ref[...])
        o_ref[:, j, :] = _dot(z.astype(BF16), wf_ref[...]).astype(BF16)


def _fourier_fourstep(f, wf_bf):
    l, ch = f.shape
    n2 = l // DFT1
    c1, s1 = _dft_mats(DFT1)
    m1 = jnp.asarray(np.concatenate([c1, -s1], 0), BF16)
    tn = min(8192, n2 * ch)
    h = pl.pallas_call(
        _fourier1_body,
        grid=(n2 * ch // tn,),
        in_specs=[_resident((2 * DFT1, DFT1)), pl.BlockSpec((DFT1, tn), lambda j: (0, j))],
        out_specs=pl.BlockSpec((2 * DFT1, tn // ch, ch), lambda j: (0, j, 0)),
        out_shape=jax.ShapeDtypeStruct((2 * DFT1, n2, ch), BF16),
        compiler_params=_cparams("arbitrary"),
    )(m1, f.reshape(DFT1, n2 * ch))

    k1 = np.arange(DFT1)
    nn = np.arange(n2)
    ang = 2.0 * np.pi * ((nn[:, None] * k1[None, :]) % l) / l
    tc = jnp.asarray(np.cos(ang).reshape(n2, DFT1 // 8, 8).transpose(1, 0, 2), F32)
    ts = jnp.asarray(np.sin(ang).reshape(n2, DFT1 // 8, 8).transpose(1, 0, 2), F32)
    c2, s2 = _dft_mats(n2)
    m2 = jnp.asarray(np.block([[c2, s2], [-s2, c2]]), BF16)
    cc, sc = _channel_dft(1.0 / math.sqrt(l * FOURIER_GROUP))
    nb = DFT1 // 8
    out = pl.pallas_call(
        functools.partial(_fourier2_body, n2=n2),
        grid=(nb,),
        in_specs=[pl.BlockSpec((8, n2, ch), lambda b: (b, 0, 0)),
                  pl.BlockSpec((8, n2, ch), lambda b: (b + nb, 0, 0)),
                  pl.BlockSpec((1, n2, 8), lambda b: (b, 0, 0)),
                  pl.BlockSpec((1, n2, 8), lambda b: (b, 0, 0)),
                  _resident((2 * n2, 2 * n2)), _resident((ch, ch)), _resident((ch, ch)), _resident((ch, ch))],
        out_specs=pl.BlockSpec((n2, 8, ch), lambda b: (0, b, 0)),
        out_shape=jax.ShapeDtypeStruct((n2, DFT1, ch), BF16),
        compiler_params=_cparams("arbitrary"),
    )(h, h, tc, ts, m2, cc, sc, wf_bf)
    return out.reshape(l, ch)


def _fourier_dense_body(f_ref, cc_ref, sc_ref, mp_ref, wf_ref, o_ref):
    f = f_ref[...]
    a = jnp.concatenate([_dot(f, cc_ref[...]), _dot(f, sc_ref[...])], axis=0).astype(BF16)
    z = _dot(mp_ref[...], a)
    o_ref[...] = _dot(z.astype(BF16), wf_ref[...]).astype(BF16)


def _fourier_dense(f, wf_bf):
    l, ch = f.shape
    cl, sl = _dft_mats(l)
    mp = jnp.asarray(np.concatenate([cl, -sl], 1), BF16)
    cc, sc = _channel_dft(1.0 / math.sqrt(l * FOURIER_GROUP))
    return pl.pallas_call(
        _fourier_dense_body,
        grid=(1,),
        in_specs=[_resident((l, ch)), _resident((ch, ch)), _resident((ch, ch)), _resident((l, 2 * l)),
                  _resident((ch, ch))],
        out_specs=pl.BlockSpec((l, ch), lambda i: (0, 0)),
        out_shape=jax.ShapeDtypeStruct((l, ch), BF16),
        compiler_params=_cparams("arbitrary"),
    )(f, cc, sc, mp, wf_bf)


CONV_HALO = 16
CONV_ROWS = 64
SUBLANES = 8
CONV_SPAN_PAD = (CONV_HALO - CONV_PAD + CONV_WIDTH - 1) // SUBLANES * SUBLANES


def _conv_body(prev_ref, cur_ref, next_ref, cw_ref, vec_ref, wpw_ref, o_ref, ubuf, shifted, *, tm, nt):
    t = pl.program_id(0)
    ubuf[0:CONV_HALO, :] = jnp.where(t > 0, prev_ref[...].astype(F32), 0.0)
    ubuf[CONV_HALO:CONV_HALO + tm, :] = cur_ref[...].astype(F32)
    ubuf[CONV_HALO + tm:2 * CONV_HALO + tm, :] = jnp.where(t < nt - 1, next_ref[...].astype(F32), 0.0)
    cb, lg, lb = vec_ref[0:1, :], vec_ref[1:2, :], vec_ref[2:3, :]
    first = CONV_HALO - CONV_PAD
    span = tm + CONV_SPAN_PAD
    for rho in range(SUBLANES):
        shifted[rho, 0:span, :] = ubuf[rho:rho + span, :]
    for c in range(tm // CONV_ROWS):
        acc = None
        for j in range(CONV_WIDTH):
            rho, q = (first + j) % SUBLANES, (first + j) // SUBLANES
            row0 = c * CONV_ROWS + SUBLANES * q
            term = cw_ref[j:j + 1, :] * shifted[rho, row0:row0 + CONV_ROWS, :]
            acc = term if acc is None else acc + term
        y = _ln(acc + cb) * lg + lb
        y = y * jax.nn.sigmoid(y)
        o_ref[c * CONV_ROWS:(c + 1) * CONV_ROWS, :] = _dot(y.astype(BF16), wpw_ref[...]).astype(BF16)


def _conv(u, conv_w, conv_b, ln_g, ln_b, wpw_bf, *, tm):
    s, ch = u.shape
    nt = s // tm
    hb = tm // CONV_HALO
    nh = s // CONV_HALO
    cw = jnp.zeros((32, ch), F32).at[:CONV_WIDTH].set(conv_w)
    vec = jnp.zeros((8, ch), F32).at[0].set(conv_b).at[1].set(ln_g).at[2].set(ln_b)
    return pl.pallas_call(
        functools.partial(_conv_body, tm=tm, nt=nt),
        grid=(nt,),
        in_specs=[pl.BlockSpec((CONV_HALO, ch), lambda t: (jnp.maximum(t * hb - 1, 0), 0)),
                  pl.BlockSpec((tm, ch), lambda t: (t, 0)),
                  pl.BlockSpec((CONV_HALO, ch), lambda t: (jnp.minimum((t + 1) * hb, nh - 1), 0)),
                  _resident((32, ch)), _resident((8, ch)), _resident((ch, ch))],
        out_specs=pl.BlockSpec((tm, ch), lambda t: (t, 0)),
        out_shape=jax.ShapeDtypeStruct((s, ch), BF16),
        scratch_shapes=[pltpu.VMEM((tm + 2 * CONV_HALO, ch), F32),
                        pltpu.VMEM((SUBLANES, tm + CONV_SPAN_PAD, ch), F32)],
        compiler_params=_cparams("arbitrary"),
    )(u, u, u, cw, vec, wpw_bf)


def _split8(x):
    hi = x.astype(F8).astype(F32)
    return hi, (x - hi).astype(F8).astype(F32)


def _prep_q_body(sc_ref, q_ref, o_ref):
    hi, lo = _split8(q_ref[...].astype(F32) * sc_ref[0])
    first = lax.broadcasted_iota(jnp.int32, hi.shape, 1) < ATT_QK_DIM
    hl = (jnp.where(first, hi, pltpu.roll(lo, ATT_QK_DIM, 1)),
          jnp.where(first, pltpu.roll(hi, ATT_QK_DIM, 1), lo))
    for c in range(2):
        t = hl[c].T.astype(F8)
        o_ref[0, c, 0:LANES, :] = t
        o_ref[0, c, LANES:2 * LANES, :] = t


def _prep_k_body(sc_ref, k_ref, o_ref):
    hi, lo = _split8(k_ref[...].astype(F32) * sc_ref[0])
    first = lax.broadcasted_iota(jnp.int32, hi.shape, 1) < ATT_QK_DIM
    his, los = pltpu.roll(hi, ATT_QK_DIM, 1), pltpu.roll(lo, ATT_QK_DIM, 1)
    o_ref[0, 0, :, 0:LANES] = jnp.where(first, hi, his).astype(F8)
    o_ref[0, 0, :, LANES:2 * LANES] = jnp.where(first, lo, los).astype(F8)
    o_ref[0, 1, :, 0:LANES] = jnp.where(first, his, hi).astype(F8)
    o_ref[0, 1, :, LANES:2 * LANES] = jnp.where(first, los, lo).astype(F8)


def _prep_v_body(v_ref, o_ref):
    o_ref[0, 0, 0:ATT_V_DIM, :] = v_ref[...].astype(F32).T.astype(BF16)
    row = lax.broadcasted_iota(jnp.int32, (ATT_V_ROWS - ATT_V_DIM, v_ref.shape[0]), 0)
    o_ref[0, 0, ATT_V_DIM:ATT_V_ROWS, :] = jnp.where(row == 0, 1.0, 0.0).astype(BF16)


def _pow2_scale(mx):
    _, e = jnp.frexp(mx)
    return jnp.ldexp(jnp.float32(1.0), 5 - e).astype(F32)


def _attn_prep(q, k_all, v_all, qmax, kmax, *, tk, tm):
    s, lk = q.shape[0], k_all.shape[0]
    nk = lk // tk
    aq, ak = _pow2_scale(qmax), _pow2_scale(kmax)
    smem = pl.BlockSpec(memory_space=pltpu.SMEM)
    head_rows = lambda t: pl.BlockSpec((t, LANES), lambda h, i: (i, h))
    qt8 = pl.pallas_call(
        _prep_q_body,
        grid=(N_ATT_HEADS, s // tm),
        in_specs=[smem, head_rows(tm)],
        out_specs=pl.BlockSpec((1, 2, 2 * LANES, tm), lambda h, i: (h, 0, 0, i)),
        out_shape=jax.ShapeDtypeStruct((N_ATT_HEADS, 2, 2 * LANES, s), F8),
        compiler_params=_cparams("arbitrary", "arbitrary"),
    )(aq.reshape(1), q)
    tmk = _pick(lk, (1280, 256))
    k8 = pl.pallas_call(
        _prep_k_body,
        grid=(N_ATT_HEADS, lk // tmk),
        in_specs=[smem, head_rows(tmk)],
        out_specs=pl.BlockSpec((1, 2, tmk, 2 * LANES), lambda h, i: (h, 0, i, 0)),
        out_shape=jax.ShapeDtypeStruct((N_ATT_HEADS, 2, lk, 2 * LANES), F8),
        compiler_params=_cparams("arbitrary", "arbitrary"),
    )(ak.reshape(1), k_all)
    vt = pl.pallas_call(
        _prep_v_body,
        grid=(N_ATT_HEADS, nk),
        in_specs=[head_rows(tk)],
        out_specs=pl.BlockSpec((1, 1, ATT_V_ROWS, tk), lambda h, i: (h, i, 0, 0)),
        out_shape=jax.ShapeDtypeStruct((N_ATT_HEADS, nk, ATT_V_ROWS, tk), BF16),
        compiler_params=_cparams("arbitrary", "arbitrary"),
    )(v_all)
    return qt8, k8.reshape(N_ATT_HEADS, 2, nk, tk, 2 * LANES), vt, (1.0 / (aq * ak)).reshape(1)


def _attn_body(c_ref, lam_ref, g_ref, qt_ref, k_ref, vt_ref, o_ref, acc1, acc2, s1_buf, s2_buf,
               *, nk, tq, lam_init):
    cs = c_ref[0]
    cs_bf = cs.astype(BF16)
    acc1[...] = jnp.zeros_like(acc1)
    acc2[...] = jnp.zeros_like(acc2)

    tk = k_ref.shape[3]
    bufs, qs, accs = (s1_buf, s2_buf), (qt_ref[0, 0], qt_ref[0, 1]), (acc1, acc2)

    def chunk_step(i, slot, mx, m, *, consume, produce):
        if consume:
            mn = [jnp.maximum(m[c], mx[c]) for c in range(2)]
        new_mx, pv = [None, None], [None, None]
        for r in range(tk // ATT_SUB):
            rows = slice(r * ATT_SUB, (r + 1) * ATT_SUB)
            if produce:
                for c in range(2):
                    s = _dot(k_ref[0, c, i + 1, rows, :], qs[c])
                    bufs[c][1 - slot, rows, :] = s
                    smax = jnp.max(s, axis=0, keepdims=True)
                    new_mx[c] = smax if new_mx[c] is None else jnp.maximum(new_mx[c], smax)
            if consume:
                vt = vt_ref[0, i, :, rows]
                for c in range(2):
                    p = jnp.exp2((bufs[c][slot, rows, :] - mn[c]).astype(BF16) * cs_bf)
                    d = _dot(vt, p)
                    pv[c] = d if pv[c] is None else pv[c] + d
        if consume:
            for c in range(2):
                accs[c][...] = jnp.exp2((m[c] - mn[c]) * cs) * accs[c][...] + pv[c]
            m = tuple(mn)
        return tuple(new_mx), m

    def step(i, carry):
        mx, m = carry
        return lax.cond(i % 2 == 0,
                        lambda: chunk_step(i, 0, mx, m, consume=True, produce=True),
                        lambda: chunk_step(i, 1, mx, m, consume=True, produce=True))

    neg = jnp.full((1, tq), -jnp.inf, F32)
    mx0, _ = chunk_step(-1, 1, None, None, consume=False, produce=True)
    mx, m = lax.fori_loop(0, nk - 1, step, (mx0, (neg, neg)))
    chunk_step(nk - 1, (nk - 1) % 2, mx, m, consume=True, produce=False)

    lp = lam_ref[...]
    lam = (jnp.exp(jnp.sum(lp[0:1] * lp[1:2], axis=-1, keepdims=True))
           - jnp.exp(jnp.sum(lp[2:3] * lp[3:4], axis=-1, keepdims=True)) + lam_init)
    dv = ATT_V_DIM
    o = acc1[0:dv, :] / acc1[dv:dv + 1, :] - lam * (acc2[0:dv, :] / acc2[dv:dv + 1, :])
    ms = jnp.mean(o * o, axis=0, keepdims=True)
    o = o * lax.rsqrt(ms + LN_EPS) * g_ref[...] * (1.0 - lam_init)
    o_ref[...] = o.T.astype(BF16)


def _attention(q, k_all, v_all, qmax, kmax, lam_p, g, *, lam_init, tq, tk):
    s = q.shape[0]
    nk = k_all.shape[0] // tk
    qt8, k8, vt, c = _attn_prep(q, k_all, v_all, qmax, kmax, tk=tk, tm=tq)
    per_head = lambda shape: pl.BlockSpec(shape, lambda h, i: (h,) + (0,) * (len(shape) - 1),
                                          pipeline_mode=pl.Buffered(1))
    return pl.pallas_call(
        functools.partial(_attn_body, nk=nk, tq=tq, lam_init=lam_init),
        grid=(N_ATT_HEADS, s // tq),
        in_specs=[pl.BlockSpec(memory_space=pltpu.SMEM),
                  pl.BlockSpec((8, LANES), lambda h, i: (0, 0)),
                  pl.BlockSpec((ATT_V_DIM, 1), lambda h, i: (0, 0)),
                  pl.BlockSpec((1, 2, 2 * LANES, tq), lambda h, i: (h, 0, 0, i)),
                  per_head((1, 2, nk, tk, 2 * LANES)),
                  per_head((1, nk, ATT_V_ROWS, tk))],
        out_specs=pl.BlockSpec((tq, ATT_V_DIM), lambda h, i: (i, h)),
        out_shape=jax.ShapeDtypeStruct((s, D_ATT), BF16),
        scratch_shapes=[pltpu.VMEM((ATT_V_ROWS, tq), F32), pltpu.VMEM((ATT_V_ROWS, tq), F32)]
        + [pltpu.VMEM((2, tk, tq), F32) for _ in range(2)],
        compiler_params=_cparams("arbitrary", "arbitrary"),
    )(c, lam_p, g.reshape(ATT_V_DIM, 1), qt8, k8, vt)


def _route(logits):
    lane = lax.broadcasted_iota(jnp.int32, logits.shape, 1)
    lane_f = lane.astype(F32)
    is_g = lane < N_GROUPS
    gl = jnp.where(is_g, logits, NEG_BIG)
    gmax = jnp.max(gl, axis=-1, keepdims=True)
    gidx = jnp.min(jnp.where(gl == gmax, lane_f, float(ROUTER_LANES)), axis=-1, keepdims=True)
    wg = 1.0 / jnp.sum(jnp.where(is_g, jnp.exp(gl - gmax), 0.0), axis=-1, keepdims=True)
    grp = ((lane - N_GROUPS) // EXPERTS_PER_GROUP).astype(F32)
    valid = (lane >= N_GROUPS) & (lane < N_GROUPS + N_EXPERTS) & (grp == gidx)
    el = jnp.where(valid, logits, NEG_BIG)
    m1 = jnp.max(el, axis=-1, keepdims=True)
    i1 = jnp.min(jnp.where(el == m1, lane_f, float(ROUTER_LANES)), axis=-1, keepdims=True)
    el2 = jnp.where(lane_f == i1, NEG_BIG, el)
    m2 = jnp.max(el2, axis=-1, keepdims=True)
    i2 = jnp.min(jnp.where(el2 == m2, lane_f, float(ROUTER_LANES)), axis=-1, keepdims=True)
    t = jnp.exp(m2 - m1)
    w1 = wg / (1.0 + t)
    w2 = wg * t / (1.0 + t)
    e1 = (i1 - N_GROUPS).astype(jnp.int32)
    e2 = (i2 - N_GROUPS).astype(jnp.int32)
    eidx = jnp.where(lane == 0, e1, jnp.where(lane == 1, e2, 0))
    wts = jnp.where(lane == 0, w1, jnp.where(lane == 1, w2, 0.0))
    return eidx, wts


def _outproj_body(x_ref, f_ref, c_ref, a_ref, w_ref, mod_ref, ln_ref, wr_ref, br_ref,
                  xo_ref, h2_ref, ei_ref, wt_ref):
    mix = (_dot(f_ref[...], w_ref[0:D_FOURIER, :])
           + _dot(c_ref[...], w_ref[D_FOURIER:D_FOURIER + D_CONV, :])
           + _dot(a_ref[...], w_ref[D_FOURIER + D_CONV:D_MIX, :]))
    r = ALPHA * x_ref[...] + mod_ref[2:3, :] * mix
    xn = _ln(r) * ln_ref[0:1, :] + ln_ref[1:2, :]
    xo_ref[...] = xn
    h2 = _ln(xn) * (1.0 + mod_ref[4:5, :]) + mod_ref[3:4, :]
    h2_ref[...] = h2
    logits = _dot(h2.astype(BF16), wr_ref[...]) + br_ref[...]
    eidx, wts = _route(logits)
    ei_ref[...] = eidx
    wt_ref[...] = wts


def _outproj(x, four, conv, att, w_out_bf, mod8, ln8, wr_bf, br, *, tm):
    s, d = x.shape
    row = lambda n: pl.BlockSpec((tm, n), lambda i: (i, 0))
    return pl.pallas_call(
        _outproj_body,
        grid=(s // tm,),
        in_specs=[row(d), row(D_FOURIER), row(D_CONV), row(D_ATT), _resident((D_MIX, d)), _resident((8, d)),
                  _resident((8, d)), _resident((d, ROUTER_LANES)), _resident((1, ROUTER_LANES))],
        out_specs=[row(d), row(d), row(ROUTER_LANES), row(ROUTER_LANES)],
        out_shape=[jax.ShapeDtypeStruct((s, d), F32), jax.ShapeDtypeStruct((s, d), F32),
                   jax.ShapeDtypeStruct((s, ROUTER_LANES), jnp.int32),
                   jax.ShapeDtypeStruct((s, ROUTER_LANES), F32)],
        compiler_params=_cparams("arbitrary"),
    )(x, four, conv, att, w_out_bf, mod8, ln8, wr_bf, br)


TOK_TILE = 256
COMBINE_GROUPS = 8


def _moe_plan(eidx2, tb):
    n = eidx2.shape[0]
    nb = -(-(n * TOP_K) // tb) + N_EXPERTS
    onehot = (eidx2[:, :, None] == jnp.arange(N_EXPERTS, dtype=jnp.int32)).astype(jnp.int32)
    per_tok = onehot.sum(axis=1)
    incl = jnp.cumsum(per_tok, axis=0)
    counts = incl[-1]
    padded = (counts + tb - 1) // tb * tb
    pad_end = jnp.cumsum(padded)
    rank_base = (incl - per_tok) + (pad_end - padded)[None, :]
    dest = jnp.sum(onehot * rank_base[:, None, :], axis=-1).astype(jnp.int32)
    block_row = jnp.arange(nb, dtype=jnp.int32) * tb
    block_e = jnp.minimum(jnp.sum(pad_end[None, :] <= block_row[:, None], axis=1), N_EXPERTS - 1).astype(jnp.int32)
    first = jnp.concatenate([jnp.ones((1,), jnp.int32), (block_e[1:] != block_e[:-1]).astype(jnp.int32)])
    nused = (pad_end[-1] // tb).astype(jnp.int32).reshape(1)
    zrow = jnp.maximum(pad_end - tb, 0).astype(jnp.int32)
    zflag = (counts > 0).astype(jnp.int32)
    return dest, block_e, first, nused, zrow, zflag, nb


def _dispatch_body(zrow_ref, zflag_ref, nused_ref, dest_ref, h_ref, xs_hbm, zbuf, stage, sem, zsem,
                   *, tb, nt, nb):
    i = pl.program_id(0)

    def zero_copy(e):
        return pltpu.make_async_copy(zbuf, xs_hbm.at[pl.ds(pl.multiple_of(zrow_ref[e], tb), tb)], zsem)

    @pl.when(i == 0)
    def _():
        zbuf[...] = jnp.zeros_like(zbuf)
        for e in range(N_EXPERTS):
            @pl.when(zflag_ref[e] == 1)
            def _():
                zero_copy(e).start()
        for e in range(N_EXPERTS):
            @pl.when(zflag_ref[e] == 1)
            def _():
                zero_copy(e).wait()

        def tail_copy(j):
            return pltpu.make_async_copy(zbuf, xs_hbm.at[pl.ds(pl.multiple_of(j * tb, tb), tb)], zsem)

        def start_tail(j, c):
            tail_copy(j).start()
            return c

        def wait_tail(j, c):
            tail_copy(j).wait()
            return c
        lax.fori_loop(nused_ref[0], nb, start_tail, 0)
        lax.fori_loop(nused_ref[0], nb, wait_tail, 0)

    slot = i % 2

    def wait_tile(sl):
        for _ in range(TOP_K):
            pltpu.make_async_copy(stage.at[sl], xs_hbm.at[pl.ds(0, TOK_TILE)], sem.at[sl]).wait()

    @pl.when(i >= 2)
    def _():
        wait_tile(slot)

    stage[slot] = h_ref[...]

    def body(r, c):
        src = stage.at[slot, pl.ds(r, 1)]
        for k in range(TOP_K):
            row = dest_ref[0, 0, TOP_K * r + k]
            pltpu.make_async_copy(src, xs_hbm.at[pl.ds(row, 1)], sem.at[slot]).start()
        return c
    lax.fori_loop(0, TOK_TILE, body, 0, unroll=8)

    @pl.when(i == nt - 1)
    def _():
        wait_tile(slot)
        if nt >= 2:
            wait_tile(1 - slot)


def _dispatch(h2, dest, zrow, zflag, nused, *, tb, nb):
    n, d = h2.shape
    nt = n // TOK_TILE
    grid_spec = pltpu.PrefetchScalarGridSpec(
        num_scalar_prefetch=3,
        grid=(nt,),
        in_specs=[pl.BlockSpec((1, 1, TOP_K * TOK_TILE), lambda i, *_: (i, 0, 0), memory_space=pltpu.SMEM),
                  pl.BlockSpec((TOK_TILE, d), lambda i, *_: (i, 0))],
        out_specs=pl.BlockSpec(memory_space=pl.ANY),
        scratch_shapes=[pltpu.VMEM((tb, d), F32), pltpu.VMEM((2, TOK_TILE, d), F32),
                        pltpu.SemaphoreType.DMA((2,)), pltpu.SemaphoreType.DMA(())],
    )
    return pl.pallas_call(
        functools.partial(_dispatch_body, tb=tb, nt=nt, nb=nb),
        grid_spec=grid_spec,
        out_shape=jax.ShapeDtypeStruct((nb * tb, d), F32),
        compiler_params=_cparams("arbitrary"),
    )(zrow, zflag, nused, dest.reshape(nt, 1, TOP_K * TOK_TILE), h2)


def _experts_body(be_ref, first_ref, nused_ref, x_ref, wg_ref, wu_ref, wd_ref, o_ref, wgb, wub, wdb):
    del be_ref
    i = pl.program_id(0)

    @pl.when(i < nused_ref[0])
    def _():
        @pl.when(first_ref[i] == 1)
        def _():
            wgb[...] = wg_ref[0, 0].astype(BF16)
            wub[...] = wu_ref[0, 0].astype(BF16)
            wdb[...] = wd_ref[0, 0].astype(BF16)

        x = x_ref[...].astype(BF16)
        g = _dot(x, wgb[...])
        u = _dot(x, wub[...])
        mid = (g * jax.nn.sigmoid(g) * u).astype(BF16)
        o_ref[...] = _dot(mid, wdb[...])

    @pl.when(i >= nused_ref[0])
    def _():
        o_ref[...] = jnp.zeros_like(o_ref)


def _experts(xs, block_e, first, nused, w_gate, w_up, w_down, layer, *, tb, nb):
    d = xs.shape[1]
    wspec = lambda shape: pl.BlockSpec((1,) + shape, lambda i, be, first, nused: (layer, be[i], 0, 0))
    xspec = pl.BlockSpec((tb, d), lambda i, be, first, nused: (jnp.minimum(i, nused[0] - 1), 0))
    grid_spec = pltpu.PrefetchScalarGridSpec(
        num_scalar_prefetch=3,
        grid=(nb,),
        in_specs=[xspec, wspec((1, d, D_EXPERT)), wspec((1, d, D_EXPERT)), wspec((1, D_EXPERT, d))],
        out_specs=pl.BlockSpec((tb, d), lambda i, *_: (i, 0)),
        scratch_shapes=[pltpu.VMEM((d, D_EXPERT), BF16), pltpu.VMEM((d, D_EXPERT), BF16),
                        pltpu.VMEM((D_EXPERT, d), BF16)],
    )
    return pl.pallas_call(
        _experts_body,
        grid_spec=grid_spec,
        out_shape=jax.ShapeDtypeStruct((nb * tb, d), F32),
        compiler_params=_cparams("arbitrary"),
    )(block_e, first, nused, xs, w_gate, w_up, w_down)


def _combine_body(dcur_ref, dnext_ref, x_ref, wt_ref, mod_ref, ln_ref, ys_hbm, o_ref, ybuf, sem, *, nt):
    i = pl.program_id(0)

    slot = i % 2

    def start_row(d_ref, sl, r):
        for k in range(TOP_K):
            row = d_ref[0, 0, TOP_K * r + k]
            pltpu.make_async_copy(ys_hbm.at[pl.ds(row, 1)], ybuf.at[sl, k, pl.ds(r, 1)], sem.at[sl]).start()

    def wait_tile(sl):
        for k in range(TOP_K):
            pltpu.make_async_copy(ys_hbm.at[pl.ds(0, TOK_TILE)], ybuf.at[sl, k], sem.at[sl]).wait()

    @pl.when(i == 0)
    def _():
        def body(r, c):
            start_row(dcur_ref, 0, r)
            return c
        lax.fori_loop(0, TOK_TILE, body, 0)

    wait_tile(slot)
    rows_g = TOK_TILE // COMBINE_GROUPS
    for g in range(COMBINE_GROUPS):
        rs = slice(g * rows_g, (g + 1) * rows_g)
        w = wt_ref[rs, :]
        y = w[:, 0:1] * ybuf[slot, 0, rs, :] + w[:, 1:2] * ybuf[slot, 1, rs, :]
        r = ALPHA * x_ref[rs, :] + mod_ref[5:6, :] * y
        o_ref[rs, :] = _ln(r) * ln_ref[0:1, :] + ln_ref[1:2, :]
        for rr in range(g * rows_g, (g + 1) * rows_g):
            start_row(dnext_ref, 1 - slot, rr)

    @pl.when(i == nt - 1)
    def _():
        wait_tile(1 - slot)


def _combine(x, ys, dest, wt, mod8, ln8, *, row0):
    s, d = x.shape
    nt = s // TOK_TILE
    t0 = row0 // TOK_TILE
    dest3 = dest.reshape(-1, 1, TOP_K * TOK_TILE)
    dspec = lambda off: pl.BlockSpec((1, 1, TOP_K * TOK_TILE),
                                     lambda i: (t0 + jnp.minimum(i + off, nt - 1), 0, 0), memory_space=pltpu.SMEM)
    return pl.pallas_call(
        functools.partial(_combine_body, nt=nt),
        grid=(nt,),
        in_specs=[dspec(0), dspec(1),
                  pl.BlockSpec((TOK_TILE, d), lambda i: (i, 0)),
                  pl.BlockSpec((TOK_TILE, ROUTER_LANES), lambda i: (i, 0)),
                  _resident((8, d)), _resident((8, d)),
                  pl.BlockSpec(memory_space=pl.ANY)],
        out_specs=pl.BlockSpec((TOK_TILE, d), lambda i: (i, 0)),
        out_shape=jax.ShapeDtypeStruct((s, d), F32),
        scratch_shapes=[pltpu.VMEM((2, TOP_K, TOK_TILE, d), F32), pltpu.SemaphoreType.DMA((2,))],
        compiler_params=_cparams("arbitrary"),
    )(dest3, dest3, x, wt, mod8, ln8, ys)


def _moe(h2, eidx2, w_gate, w_up, w_down, layer):
    tb = MOE_ROWS
    dest, block_e, first, nused, zrow, zflag, nb = _moe_plan(eidx2, tb)
    xs = _dispatch(h2, dest, zrow, zflag, nused, tb=tb, nb=nb)
    ys = _experts(xs, block_e, first, nused, w_gate, w_up, w_down, layer, tb=tb, nb=nb)
    return ys, dest


def _pick(n, prefs):
    for t in prefs:
        if n % t == 0:
            return t
    raise ValueError(f"no tile of {prefs} divides {n}")


def _rows8(*vecs):
    d = vecs[0].shape[-1]
    out = jnp.zeros((8, d), F32)
    for j, v in enumerate(vecs):
        out = out.at[j].set(v)
    return out


def kernel(x, c, ctx, c_ctx, w_mod, b_mod, w_in, b_in, w_fourier, conv_w, conv_b, conv_ln_g, conv_ln_b, w_pw,
           lam_q1, lam_k1, lam_q2, lam_k2, subln_g, w_out, ln_a_g, ln_a_b, w_rg, b_rg, w_re, b_re,
           w_gate, w_up, w_down, ln_f_g, ln_f_b):
    b, s, d = x.shape
    nc = ctx.shape[1]
    assert b == 1 and d == D_MODEL and c.shape[0] == 1 and ctx.shape[0] == 1
    assert s % (DFT1 * 8) == 0 and nc % TOK_TILE == 0
    xl, xc = x[0], ctx[0]
    depth = w_mod.shape[0]

    c2 = jnp.zeros((8, d), F32).at[0].set(c[0]).at[1].set(c_ctx)
    mod_all = _modulation(c2, w_mod, b_mod)
    tables_l = _rope_tables(s)
    tables_c = tuple(jnp.zeros((nc, LANES), F32) for _ in range(3))
    tm_l = _pick(s, (512, 256))
    tm_c = _pick(nc, (256,))
    tq = _pick(s, (512, 256))
    tk = _pick(s + nc, (3328, 1280, 256))

    for i in range(depth):
        last = i == depth - 1
        lam_init = 0.8 - 0.6 * math.exp(-0.3 * i)
        mod_l = _rows8(*jnp.split(mod_all[i, 0], N_MOD))
        mod_c = _rows8(*jnp.split(mod_all[i, 1], N_MOD))
        w_in_bf = w_in[i].astype(BF16)
        wf_bf = w_fourier[i].astype(BF16)
        wpw_bf = w_pw[i].astype(BF16)
        w_out_bf = w_out[i].astype(BF16)
        wr = jnp.zeros((d, ROUTER_LANES), F32).at[:, :N_GROUPS].set(w_rg[i])
        wr_bf = wr.at[:, N_GROUPS:N_GROUPS + N_EXPERTS].set(w_re[i]).astype(BF16)
        br = jnp.zeros((1, ROUTER_LANES), F32).at[0, :N_GROUPS].set(b_rg[i])
        br = br.at[0, N_GROUPS:N_GROUPS + N_EXPERTS].set(b_re[i])
        lam_p = jnp.zeros((8, LANES), F32).at[0, :ATT_QK_DIM].set(lam_q1[i]).at[1, :ATT_QK_DIM].set(lam_k1[i])
        lam_p = lam_p.at[2, :ATT_QK_DIM].set(lam_q2[i]).at[3, :ATT_QK_DIM].set(lam_k2[i])
        ln_a = _rows8(ln_a_g[i], ln_a_b[i])
        ln_f = _rows8(ln_f_g[i], ln_f_b[i])
        conv_args = (conv_w[i], conv_b[i], conv_ln_g[i], conv_ln_b[i], wpw_bf)

        f_l, u_l, q_l, k_l, v_l, st_l = _inproj(xl, mod_l, w_in_bf, b_in[i], tables_l, rope=True, tm=tm_l)
        f_c, u_c, q_c, k_c, v_c, st_c = _inproj(xc, mod_c, w_in_bf, b_in[i], tables_c, rope=False, tm=tm_c)
        qmax_l, qmax_c = jnp.max(st_l[:, 0, 0]), jnp.max(st_c[:, 0, 0])
        kmax_c = jnp.max(st_c[:, 1, 0])
        kmax_all = jnp.maximum(jnp.max(st_l[:, 1, 0]), kmax_c)
        k_all = jnp.concatenate([k_c, k_l], axis=0)
        v_all = jnp.concatenate([v_c, v_l], axis=0)
        att_l = _attention(q_l, k_all, v_all, qmax_l, kmax_all, lam_p, subln_g[i],
                           lam_init=lam_init, tq=tq, tk=tk)
        four_l = _fourier_fourstep(f_l, wf_bf)
        conv_l = _conv(u_l, *conv_args, tm=tm_l)
        xl, h2_l, ei_l, wt_l = _outproj(xl, four_l, conv_l, att_l, w_out_bf, mod_l, ln_a, wr_bf, br, tm=tm_l)
        if last:
            ys, dest = _moe(h2_l, ei_l[:, :TOP_K], w_gate, w_up, w_down, i)
            xl = _combine(xl, ys, dest, wt_l, mod_l, ln_f, row0=0)
        else:
            att_c = _attention(q_c, k_c, v_c, qmax_c, kmax_c, lam_p, subln_g[i],
                               lam_init=lam_init, tq=tm_c, tk=tm_c)
            four_c = _fourier_dense(f_c, wf_bf)
            conv_c = _conv(u_c, *conv_args, tm=tm_c)
            xc, h2_c, ei_c, wt_c = _outproj(xc, four_c, conv_c, att_c, w_out_bf, mod_c, ln_a, wr_bf, br, tm=tm_c)
            h2 = jnp.concatenate([h2_c, h2_l], axis=0)
            ei = jnp.concatenate([ei_c[:, :TOP_K], ei_l[:, :TOP_K]], axis=0)
            ys, dest = _moe(h2, ei, w_gate, w_up, w_down, i)
            xc = _combine(xc, ys, dest, wt_c, mod_c, ln_f, row0=0)
            xl = _combine(xl, ys, dest, wt_l, mod_l, ln_f, row0=nc)
    return xl[None]
```

```python
import functools
import math

import numpy as np
import jax
import jax.numpy as jnp
from jax import lax
from jax.experimental import pallas as pl
from jax.experimental.pallas import tpu as pltpu

F32 = jnp.float32
BF16 = jnp.bfloat16
F8 = jnp.float8_e4m3fn

D_MODEL = 2048
DEPTH = 2
GRID_W = 64
D_FOURIER = 512
N_FOURIER_GROUPS = 4
FOURIER_GROUP = 128
D_CONV = 512
CONV_WIDTH = 31
CONV_PAD = CONV_WIDTH // 2
N_ATT_HEADS = 8
ATT_QK_DIM = 64
ATT_V_DIM = 128
D_ATT = N_ATT_HEADS * ATT_V_DIM
D_MIX = D_FOURIER + D_CONV + D_ATT
QK_COLS = N_ATT_HEADS * 2 * ATT_QK_DIM
ATT_SCALE = 1.0 / math.sqrt(ATT_QK_DIM)
ROPE_BASE = 10000.0
ROPE_PAIRS_PER_AXIS = ATT_QK_DIM // 4
OFF_CONV = D_FOURIER
OFF_Q = OFF_CONV + 2 * D_CONV
OFF_K = OFF_Q + QK_COLS
OFF_V = OFF_K + QK_COLS
D_IN_PROJ = OFF_V + D_ATT
N_GROUPS = 4
EXPERTS_PER_GROUP = 8
N_EXPERTS = N_GROUPS * EXPERTS_PER_GROUP
TOP_K = 2
D_EXPERT = 512
N_MOD = 6
LN_EPS = 1e-6
ALPHA = (2.0 * DEPTH) ** 0.25

LANES = 128
VMEM_LIMIT = 56 * 1024 * 1024
LOG2E = 1.4426950408889634
Q_SCALE = ATT_SCALE * LOG2E
NEG_BIG = -1e30
DFT1 = 128
MOE_ROWS = 256
ROUTER_LANES = 128
ATT_V_ROWS = ATT_V_DIM + 16
ATT_SUB = 256


def _cparams(*sem):
    return pltpu.CompilerParams(dimension_semantics=tuple(sem), vmem_limit_bytes=VMEM_LIMIT)


def _resident(shape):
    nd = len(shape)
    return pl.BlockSpec(shape, lambda *_: (0,) * nd, pipeline_mode=pl.Buffered(1))


def _ln(x):
    mu = jnp.mean(x, axis=-1, keepdims=True)
    xc = x - mu
    var = jnp.mean(xc * xc, axis=-1, keepdims=True)
    return xc * lax.rsqrt(var + LN_EPS)


def _dot(a, b):
    return jnp.dot(a, b, preferred_element_type=F32)


def _mod_body(c_ref, w_ref, b_ref, o_ref):
    c = c_ref[...]
    sc = (c * jax.nn.sigmoid(c)).astype(BF16)
    o_ref[0] = _dot(sc, w_ref[0].astype(BF16)) + b_ref[0]


def _modulation(c2, w_mod, b_mod):
    nl, d, n = w_mod.shape
    tn = 1024
    return pl.pallas_call(
        _mod_body,
        grid=(nl, n // tn),
        in_specs=[pl.BlockSpec((8, d), lambda l, j: (0, 0)),
                  pl.BlockSpec((1, d, tn), lambda l, j: (l, 0, j)),
                  pl.BlockSpec((1, 1, tn), lambda l, j: (l, 0, j))],
        out_specs=pl.BlockSpec((1, 8, tn), lambda l, j: (l, 0, j)),
        out_shape=jax.ShapeDtypeStruct((nl, 8, n), F32),
        compiler_params=_cparams("arbitrary", "arbitrary"),
    )(c2, w_mod, b_mod.reshape(nl, 1, n))


def _inproj_body(x_ref, mod_ref, w_ref, b_ref, cos_ref, sa_ref, sb_ref,
                 f_ref, u_ref, q_ref, k_ref, v_ref, stat_ref, *, rope):
    h = _ln(x_ref[...]) * (1.0 + mod_ref[1:2, :]) + mod_ref[0:1, :]
    hb = h.astype(BF16)

    def proj(lo, hi):
        return _dot(hb, w_ref[:, lo:hi]) + b_ref[:, lo:hi]

    f_ref[...] = proj(0, OFF_CONV).astype(BF16)
    a = proj(OFF_CONV, OFF_CONV + D_CONV)
    g = proj(OFF_CONV + D_CONV, OFF_Q)
    u_ref[...] = (a * jax.nn.sigmoid(g)).astype(BF16)
    v_ref[...] = proj(OFF_V, D_IN_PROJ).astype(BF16)
    q = proj(OFF_Q, OFF_K)
    k = proj(OFF_K, OFF_V)
    if rope:
        cos, sa, sb = cos_ref[...], sa_ref[...], sb_ref[...]
    qmax = kmax = None
    for hd in range(N_ATT_HEADS):
        sl = slice(hd * LANES, (hd + 1) * LANES)
        qh, kh = q[:, sl], k[:, sl]
        if rope:
            qh = qh * cos + pltpu.roll(qh, LANES - 32, 1) * sa + pltpu.roll(qh, 32, 1) * sb
            kh = kh * cos + pltpu.roll(kh, LANES - 32, 1) * sa + pltpu.roll(kh, 32, 1) * sb
        qb, kb = (qh * Q_SCALE).astype(BF16), kh.astype(BF16)
        q_ref[:, sl] = qb
        k_ref[:, sl] = kb
        qa, ka = jnp.abs(qb.astype(F32)), jnp.abs(kb.astype(F32))
        qmax = qa if qmax is None else jnp.maximum(qmax, qa)
        kmax = ka if kmax is None else jnp.maximum(kmax, ka)
    qm = jnp.max(jnp.max(qmax, axis=0, keepdims=True), axis=1, keepdims=True)
    km = jnp.max(jnp.max(kmax, axis=0, keepdims=True), axis=1, keepdims=True)
    row = lax.broadcasted_iota(jnp.int32, (8, LANES), 0)
    stat_ref[0] = jnp.where(row == 0, qm, jnp.where(row == 1, km, 0.0))


def _inproj(x, mod8, w_bf, b, tables, *, rope, tm):
    s, d = x.shape
    cos, sa, sb = tables
    row = lambda n: pl.BlockSpec((tm, n), lambda i: (i, 0))
    outs = [(D_FOURIER, BF16), (D_CONV, BF16), (QK_COLS, BF16), (QK_COLS, BF16), (D_ATT, BF16)]
    return pl.pallas_call(
        functools.partial(_inproj_body, rope=rope),
        grid=(s // tm,),
        in_specs=[row(d), _resident((8, d)), _resident((d, D_IN_PROJ)), _resident((1, D_IN_PROJ)),
                  row(LANES), row(LANES), row(LANES)],
        out_specs=[row(n) for n, _ in outs] + [pl.BlockSpec((1, 8, LANES), lambda i: (i, 0, 0))],
        out_shape=[jax.ShapeDtypeStruct((s, n), dt) for n, dt in outs]
        + [jax.ShapeDtypeStruct((s // tm, 8, LANES), F32)],
        compiler_params=_cparams("arbitrary"),
    )(x, mod8, w_bf, b.reshape(1, -1), cos, sa, sb)


def _rope_tables(n_tokens):
    rows = n_tokens // GRID_W
    row = jnp.repeat(jnp.arange(rows), GRID_W).astype(F32)
    col = jnp.tile(jnp.arange(GRID_W), rows).astype(F32)
    inv = 1.0 / (ROPE_BASE ** (jnp.arange(ROPE_PAIRS_PER_AXIS, dtype=F32) / ROPE_PAIRS_PER_AXIS))
    ang = jnp.concatenate([row[:, None] * inv, col[:, None] * inv], -1)
    cos, sin = jnp.cos(ang), jnp.sin(ang)
    zero = jnp.zeros_like(sin)
    cos_t = jnp.tile(cos, (1, 4))
    sa = jnp.tile(jnp.concatenate([-sin, zero], -1), (1, 2))
    sb = jnp.tile(jnp.concatenate([zero, sin], -1), (1, 2))
    return cos_t, sa, sb


def _dft_mats(n):
    k = np.arange(n)
    ang = 2.0 * np.pi * ((k[:, None] * k[None, :]) % n) / n
    return np.cos(ang), np.sin(ang)


def _channel_dft(scale):
    c, s = _dft_mats(FOURIER_GROUP)
    eye = np.eye(N_FOURIER_GROUPS)
    return (jnp.asarray(np.kron(eye, c) * scale, BF16), jnp.asarray(np.kron(eye, s) * scale, BF16))


def _fourier1_body(m_ref, x_ref, o_ref):
    ch = o_ref.shape[2]
    res = _dot(m_ref[...], x_ref[...])
    for j in range(o_ref.shape[1]):
        o_ref[:, j, :] = res[:, j * ch:(j + 1) * ch].astype(BF16)


def _fourier2_body(hr_ref, hi_ref, tc_ref, ts_ref, m2_ref, cc_ref, sc_ref, wf_ref, o_ref, *, n2):
    tc_all, ts_all = tc_ref[0], ts_ref[0]
    for j in range(8):
        hr = hr_ref[j].astype(F32)
        hi = hi_ref[j].astype(F32)
        tc, ts = tc_all[:, j:j + 1], ts_all[:, j:j + 1]
        gr = hr * tc + hi * ts
        gi = hi * tc - hr * ts
        g = jnp.concatenate([gr, gi], axis=0).astype(BF16)
        y = _dot(m2_ref[...], g)
        z = _dot(y[:n2].astype(BF16), cc_ref[...]) + _dot(y[n2:].astype(BF16), sc_ref[...])
        o_ref[:, j, :] = _dot(z.astype(BF16), wf_ref[...]).astype(BF16)


def _fourier_fourstep(f, wf_bf):
    l, ch = f.shape
    n2 = l // DFT1
    c1, s1 = _dft_mats(DFT1)
    m1 = jnp.asarray(np.concatenate([c1, -s1], 0), BF16)
    tn = min(8192, n2 * ch)
    h = pl.pallas_call(
        _fourier1_body,
        grid=(n2 * ch // tn,),
        in_specs=[_resident((2 * DFT1, DFT1)), pl.BlockSpec((DFT1, tn), lambda j: (0, j))],
        out_specs=pl.BlockSpec((2 * DFT1, tn // ch, ch), lambda j: (0, j, 0)),
        out_shape=jax.ShapeDtypeStruct((2 * DFT1, n2, ch), BF16),
        compiler_params=_cparams("arbitrary"),
    )(m1, f.reshape(DFT1, n2 * ch))

    k1 = np.arange(DFT1)
    nn = np.arange(n2)
    ang = 2.0 * np.pi * ((nn[:, None] * k1[None, :]) % l) / l
    tc = jnp.asarray(np.cos(ang).reshape(n2, DFT1 // 8, 8).transpose(1, 0, 2), F32)
    ts = jnp.asarray(np.sin(ang).reshape(n2, DFT1 // 8, 8).transpose(1, 0, 2), F32)
    c2, s2 = _dft_mats(n2)
    m2 = jnp.asarray(np.block([[c2, s2], [-s2, c2]]), BF16)
    cc, sc = _channel_dft(1.0 / math.sqrt(l * FOURIER_GROUP))
    nb = DFT1 // 8
    out = pl.pallas_call(
        functools.partial(_fourier2_body, n2=n2),
        grid=(nb,),
        in_specs=[pl.BlockSpec((8, n2, ch), lambda b: (b, 0, 0)),
                  pl.BlockSpec((8, n2, ch), lambda b: (b + nb, 0, 0)),
                  pl.BlockSpec((1, n2, 8), lambda b: (b, 0, 0)),
                  pl.BlockSpec((1, n2, 8), lambda b: (b, 0, 0)),
                  _resident((2 * n2, 2 * n2)), _resident((ch, ch)), _resident((ch, ch)), _resident((ch, ch))],
        out_specs=pl.BlockSpec((n2, 8, ch), lambda b: (0, b, 0)),
        out_shape=jax.ShapeDtypeStruct((n2, DFT1, ch), BF16),
        compiler_params=_cparams("arbitrary"),
    )(h, h, tc, ts, m2, cc, sc, wf_bf)
    return out.reshape(l, ch)


def _fourier_dense_body(f_ref, cc_ref, sc_ref, mp_ref, wf_ref, o_ref):
    f = f_ref[...]
    a = jnp.concatenate([_dot(f, cc_ref[...]), _dot(f, sc_ref[...])], axis=0).astype(BF16)
    z = _dot(mp_ref[...], a)
    o_ref[...] = _dot(z.astype(BF16), wf_ref[...]).astype(BF16)


def _fourier_dense(f, wf_bf):
    l, ch = f.shape
    cl, sl = _dft_mats(l)
    mp = jnp.asarray(np.concatenate([cl, -sl], 1), BF16)
    cc, sc = _channel_dft(1.0 / math.sqrt(l * FOURIER_GROUP))
    return pl.pallas_call(
        _fourier_dense_body,
        grid=(1,),
        in_specs=[_resident((l, ch)), _resident((ch, ch)), _resident((ch, ch)), _resident((l, 2 * l)),
                  _resident((ch, ch))],
        out_specs=pl.BlockSpec((l, ch), lambda i: (0, 0)),
        out_shape=jax.ShapeDtypeStruct((l, ch), BF16),
        compiler_params=_cparams("arbitrary"),
    )(f, cc, sc, mp, wf_bf)


CONV_HALO = 16
CONV_ROWS = 64
SUBLANES = 8
CONV_SPAN_PAD = (CONV_HALO - CONV_PAD + CONV_WIDTH - 1) // SUBLANES * SUBLANES


def _conv_body(prev_ref, cur_ref, next_ref, cw_ref, vec_ref, wpw_ref, o_ref, ubuf, shifted, *, tm, nt):
    t = pl.program_id(0)
    ubuf[0:CONV_HALO, :] = jnp.where(t > 0, prev_ref[...].astype(F32), 0.0)
    ubuf[CONV_HALO:CONV_HALO + tm, :] = cur_ref[...].astype(F32)
    ubuf[CONV_HALO + tm:2 * CONV_HALO + tm, :] = jnp.where(t < nt - 1, next_ref[...].astype(F32), 0.0)
    cb, lg, lb = vec_ref[0:1, :], vec_ref[1:2, :], vec_ref[2:3, :]
    first = CONV_HALO - CONV_PAD
    span = tm + CONV_SPAN_PAD
    for rho in range(SUBLANES):
        shifted[rho, 0:span, :] = ubuf[rho:rho + span, :]
    for c in range(tm // CONV_ROWS):
        acc = None
        for j in range(CONV_WIDTH):
            rho, q = (first + j) % SUBLANES, (first + j) // SUBLANES
            row0 = c * CONV_ROWS + SUBLANES * q
            term = cw_ref[j:j + 1, :] * shifted[rho, row0:row0 + CONV_ROWS, :]
            acc = term if acc is None else acc + term
        y = _ln(acc + cb) * lg + lb
        y = y * jax.nn.sigmoid(y)
        o_ref[c * CONV_ROWS:(c + 1) * CONV_ROWS, :] = _dot(y.astype(BF16), wpw_ref[...]).astype(BF16)


def _conv(u, conv_w, conv_b, ln_g, ln_b, wpw_bf, *, tm):
    s, ch = u.shape
    nt = s // tm
    hb = tm // CONV_HALO
    nh = s // CONV_HALO
    cw = jnp.zeros((32, ch), F32).at[:CONV_WIDTH].set(conv_w)
    vec = jnp.zeros((8, ch), F32).at[0].set(conv_b).at[1].set(ln_g).at[2].set(ln_b)
    return pl.pallas_call(
        functools.partial(_conv_body, tm=tm, nt=nt),
        grid=(nt,),
        in_specs=[pl.BlockSpec((CONV_HALO, ch), lambda t: (jnp.maximum(t * hb - 1, 0), 0)),
                  pl.BlockSpec((tm, ch), lambda t: (t, 0)),
                  pl.BlockSpec((CONV_HALO, ch), lambda t: (jnp.minimum((t + 1) * hb, nh - 1), 0)),
                  _resident((32, ch)), _resident((8, ch)), _resident((ch, ch))],
        out_specs=pl.BlockSpec((tm, ch), lambda t: (t, 0)),
        out_shape=jax.ShapeDtypeStruct((s, ch), BF16),
        scratch_shapes=[pltpu.VMEM((tm + 2 * CONV_HALO, ch), F32),
                        pltpu.VMEM((SUBLANES, tm + CONV_SPAN_PAD, ch), F32)],
        compiler_params=_cparams("arbitrary"),
    )(u, u, u, cw, vec, wpw_bf)


def _split8(x):
    hi = x.astype(F8).astype(F32)
    return hi, (x - hi).astype(F8).astype(F32)


def _prep_q_body(sc_ref, q_ref, o_ref):
    hi, lo = _split8(q_ref[...].astype(F32) * sc_ref[0])
    first = lax.broadcasted_iota(jnp.int32, hi.shape, 1) < ATT_QK_DIM
    hl = (jnp.where(first, hi, pltpu.roll(lo, ATT_QK_DIM, 1)),
          jnp.where(first, pltpu.roll(hi, ATT_QK_DIM, 1), lo))
    for c in range(2):
        t = hl[c].T.astype(F8)
        o_ref[0, c, 0:LANES, :] = t
        o_ref[0, c, LANES:2 * LANES, :] = t


def _prep_k_body(sc_ref, k_ref, o_ref):
    hi, lo = _split8(k_ref[...].astype(F32) * sc_ref[0])
    first = lax.broadcasted_iota(jnp.int32, hi.shape, 1) < ATT_QK_DIM
    his, los = pltpu.roll(hi, ATT_QK_DIM, 1), pltpu.roll(lo, ATT_QK_DIM, 1)
    o_ref[0, 0, :, 0:LANES] = jnp.where(first, hi, his).astype(F8)
    o_ref[0, 0, :, LANES:2 * LANES] = jnp.where(first, lo, los).astype(F8)
    o_ref[0, 1, :, 0:LANES] = jnp.where(first, his, hi).astype(F8)
    o_ref[0, 1, :, LANES:2 * LANES] = jnp.where(first, los, lo).astype(F8)


def _prep_v_body(v_ref, o_ref):
    o_ref[0, 0, 0:ATT_V_DIM, :] = v_ref[...].astype(F32).T.astype(BF16)
    row = lax.broadcasted_iota(jnp.int32, (ATT_V_ROWS - ATT_V_DIM, v_ref.shape[0]), 0)
    o_ref[0, 0, ATT_V_DIM:ATT_V_ROWS, :] = jnp.where(row == 0, 1.0, 0.0).astype(BF16)


def _pow2_scale(mx):
    _, e = jnp.frexp(mx)
    return jnp.ldexp(jnp.float32(1.0), 5 - e).astype(F32)


def _attn_prep(q, k_all, v_all, qmax, kmax, *, tk, tm):
    s, lk = q.shape[0], k_all.shape[0]
    nk = lk // tk
    aq, ak = _pow2_scale(qmax), _pow2_scale(kmax)
    smem = pl.BlockSpec(memory_space=pltpu.SMEM)
    head_rows = lambda t: pl.BlockSpec((t, LANES), lambda h, i: (i, h))
    qt8 = pl.pallas_call(
        _prep_q_body,
        grid=(N_ATT_HEADS, s // tm),
        in_specs=[smem, head_rows(tm)],
        out_specs=pl.BlockSpec((1, 2, 2 * LANES, tm), lambda h, i: (h, 0, 0, i)),
        out_shape=jax.ShapeDtypeStruct((N_ATT_HEADS, 2, 2 * LANES, s), F8),
        compiler_params=_cparams("arbitrary", "arbitrary"),
    )(aq.reshape(1), q)
    tmk = _pick(lk, (3328, 1280, 256))
    k8 = pl.pallas_call(
        _prep_k_body,
        grid=(N_ATT_HEADS, lk // tmk),
        in_specs=[smem, head_rows(tmk)],
        out_specs=pl.BlockSpec((1, 2, tmk, 2 * LANES), lambda h, i: (h, 0, i, 0)),
        out_shape=jax.ShapeDtypeStruct((N_ATT_HEADS, 2, lk, 2 * LANES), F8),
        compiler_params=_cparams("arbitrary", "arbitrary"),
    )(ak.reshape(1), k_all)
    vt = pl.pallas_call(
        _prep_v_body,
        grid=(N_ATT_HEADS, nk),
        in_specs=[head_rows(tk)],
        out_specs=pl.BlockSpec((1, 1, ATT_V_ROWS, tk), lambda h, i: (h, i, 0, 0)),
        out_shape=jax.ShapeDtypeStruct((N_ATT_HEADS, nk, ATT_V_ROWS, tk), BF16),
        compiler_params=_cparams("arbitrary", "arbitrary"),
    )(v_all)
    return qt8, k8.reshape(N_ATT_HEADS, 2, nk, tk, 2 * LANES), vt, (1.0 / (aq * ak)).reshape(1)


def _attn_body(c_ref, lam_ref, g_ref, qt_ref, k_ref, vt_ref, o_ref, acc1, acc2, s1_buf, s2_buf,
               *, nk, tq, lam_init):
    cs = c_ref[0]
    cs_bf = cs.astype(BF16)
    acc1[...] = jnp.zeros_like(acc1)
    acc2[...] = jnp.zeros_like(acc2)

    tk = k_ref.shape[3]
    bufs, qs, accs = (s1_buf, s2_buf), (qt_ref[0, 0], qt_ref[0, 1]), (acc1, acc2)

    def chunk_step(i, slot, mx, m, *, consume, produce):
        if consume:
            mn = [jnp.maximum(m[c], mx[c]) for c in range(2)]
        new_mx, pv = [None, None], [None, None]
        for r in range(tk // ATT_SUB):
            rows = slice(r * ATT_SUB, (r + 1) * ATT_SUB)
            if consume:
                vt = vt_ref[0, i, :, rows]
            for c in range(2):
                if produce:
                    s = _dot(k_ref[0, c, i + 1, rows, :], qs[c])
                    bufs[c][1 - slot, rows, :] = s
                    smax = jnp.max(s, axis=0, keepdims=True)
                    new_mx[c] = smax if new_mx[c] is None else jnp.maximum(new_mx[c], smax)
                if consume:
                    p = jnp.exp2((bufs[c][slot, rows, :] - mn[c]).astype(BF16) * cs_bf)
                    d = _dot(vt, p)
                    pv[c] = d if pv[c] is None else pv[c] + d
        if consume:
            for c in range(2):
                accs[c][...] = jnp.exp2((m[c] - mn[c]) * cs) * accs[c][...] + pv[c]
            m = tuple(mn)
        return tuple(new_mx), m

    def step(i, carry):
        mx, m = carry
        return lax.cond(i % 2 == 0,
                        lambda: chunk_step(i, 0, mx, m, consume=True, produce=True),
                        lambda: chunk_step(i, 1, mx, m, consume=True, produce=True))

    neg = jnp.full((1, tq), -jnp.inf, F32)
    mx0, _ = chunk_step(-1, 1, None, None, consume=False, produce=True)
    mx, m = lax.fori_loop(0, nk - 1, step, (mx0, (neg, neg)))
    chunk_step(nk - 1, (nk - 1) % 2, mx, m, consume=True, produce=False)

    lp = lam_ref[...]
    lam = (jnp.exp(jnp.sum(lp[0:1] * lp[1:2], axis=-1, keepdims=True))
           - jnp.exp(jnp.sum(lp[2:3] * lp[3:4], axis=-1, keepdims=True)) + lam_init)
    dv = ATT_V_DIM
    o = acc1[0:dv, :] / acc1[dv:dv + 1, :] - lam * (acc2[0:dv, :] / acc2[dv:dv + 1, :])
    ms = jnp.mean(o * o, axis=0, keepdims=True)
    o = o * lax.rsqrt(ms + LN_EPS) * g_ref[...] * (1.0 - lam_init)
    o_ref[...] = o.T.astype(BF16)


def _attention(q, k_all, v_all, qmax, kmax, lam_p, g, *, lam_init, tq, tk):
    s = q.shape[0]
    nk = k_all.shape[0] // tk
    qt8, k8, vt, c = _attn_prep(q, k_all, v_all, qmax, kmax, tk=tk, tm=_pick(s, (2048, 512, 256)))
    per_head = lambda shape: pl.BlockSpec(shape, lambda h, i: (h,) + (0,) * (len(shape) - 1),
                                          pipeline_mode=pl.Buffered(1))
    return pl.pallas_call(
        functools.partial(_attn_body, nk=nk, tq=tq, lam_init=lam_init),
        grid=(N_ATT_HEADS, s // tq),
        in_specs=[pl.BlockSpec(memory_space=pltpu.SMEM),
                  pl.BlockSpec((8, LANES), lambda h, i: (0, 0)),
                  pl.BlockSpec((ATT_V_DIM, 1), lambda h, i: (0, 0)),
                  pl.BlockSpec((1, 2, 2 * LANES, tq), lambda h, i: (h, 0, 0, i)),
                  per_head((1, 2, nk, tk, 2 * LANES)),
                  per_head((1, nk, ATT_V_ROWS, tk))],
        out_specs=pl.BlockSpec((tq, ATT_V_DIM), lambda h, i: (i, h)),
        out_shape=jax.ShapeDtypeStruct((s, D_ATT), BF16),
        scratch_shapes=[pltpu.VMEM((ATT_V_ROWS, tq), F32), pltpu.VMEM((ATT_V_ROWS, tq), F32)]
        + [pltpu.VMEM((2, tk, tq), F32) for _ in range(2)],
        compiler_params=_cparams("arbitrary", "arbitrary"),
    )(c, lam_p, g.reshape(ATT_V_DIM, 1), qt8, k8, vt)


def _route(logits):
    lane = lax.broadcasted_iota(jnp.int32, logits.shape, 1)
    lane_f = lane.astype(F32)
    is_g = lane < N_GROUPS
    gl = jnp.where(is_g, logits, NEG_BIG)
    gmax = jnp.max(gl, axis=-1, keepdims=True)
    gidx = jnp.min(jnp.where(gl == gmax, lane_f, float(ROUTER_LANES)), axis=-1, keepdims=True)
    wg = 1.0 / jnp.sum(jnp.where(is_g, jnp.exp(gl - gmax), 0.0), axis=-1, keepdims=True)
    grp = ((lane - N_GROUPS) // EXPERTS_PER_GROUP).astype(F32)
    valid = (lane >= N_GROUPS) & (lane < N_GROUPS + N_EXPERTS) & (grp == gidx)
    el = jnp.where(valid, logits, NEG_BIG)
    m1 = jnp.max(el, axis=-1, keepdims=True)
    i1 = jnp.min(jnp.where(el == m1, lane_f, float(ROUTER_LANES)), axis=-1, keepdims=True)
    el2 = jnp.where(lane_f == i1, NEG_BIG, el)
    m2 = jnp.max(el2, axis=-1, keepdims=True)
    i2 = jnp.min(jnp.where(el2 == m2, lane_f, float(ROUTER_LANES)), axis=-1, keepdims=True)
    t = jnp.exp(m2 - m1)
    w1 = wg / (1.0 + t)
    w2 = wg * t / (1.0 + t)
    e1 = (i1 - N_GROUPS).astype(jnp.int32)
    e2 = (i2 - N_GROUPS).astype(jnp.int32)
    eidx = jnp.where(lane == 0, e1, jnp.where(lane == 1, e2, 0))
    wts = jnp.where(lane == 0, w1, jnp.where(lane == 1, w2, 0.0))
    return eidx, wts


def _outproj_body(x_ref, f_ref, c_ref, a_ref, w_ref, mod_ref, ln_ref, wr_ref, br_ref,
                  xo_ref, h2_ref, ei_ref, wt_ref):
    def project(rs):
        return (_dot(f_ref[rs, :], w_ref[0:D_FOURIER, :])
                + _dot(c_ref[rs, :], w_ref[D_FOURIER:D_FOURIER + D_CONV, :])
                + _dot(a_ref[rs, :], w_ref[D_FOURIER + D_CONV:D_MIX, :]))

    def finish(rs, mix):
        r = ALPHA * x_ref[rs, :] + mod_ref[2:3, :] * mix
        xn = _ln(r) * ln_ref[0:1, :] + ln_ref[1:2, :]
        xo_ref[rs, :] = xn
        h2 = _ln(xn) * (1.0 + mod_ref[4:5, :]) + mod_ref[3:4, :]
        h2_ref[rs, :] = h2
        logits = _dot(h2.astype(BF16), wr_ref[...]) + br_ref[...]
        eidx, wts = _route(logits)
        ei_ref[rs, :] = eidx
        wt_ref[rs, :] = wts

    tm = x_ref.shape[0]
    rows_g = tm // OUTPROJ_GROUPS
    pending = None
    for g in range(OUTPROJ_GROUPS):
        rs = slice(g * rows_g, (g + 1) * rows_g)
        mix = project(rs)
        if pending is not None:
            finish(*pending)
        pending = (rs, mix)
    finish(*pending)


def _outproj(x, four, conv, att, w_out_bf, mod8, ln8, wr_bf, br, *, tm):
    s, d = x.shape
    row = lambda n: pl.BlockSpec((tm, n), lambda i: (i, 0))
    return pl.pallas_call(
        _outproj_body,
        grid=(s // tm,),
        in_specs=[row(d), row(D_FOURIER), row(D_CONV), row(D_ATT), _resident((D_MIX, d)), _resident((8, d)),
                  _resident((8, d)), _resident((d, ROUTER_LANES)), _resident((1, ROUTER_LANES))],
        out_specs=[row(d), row(d), row(ROUTER_LANES), row(ROUTER_LANES)],
        out_shape=[jax.ShapeDtypeStruct((s, d), F32), jax.ShapeDtypeStruct((s, d), F32),
                   jax.ShapeDtypeStruct((s, ROUTER_LANES), jnp.int32),
                   jax.ShapeDtypeStruct((s, ROUTER_LANES), F32)],
        compiler_params=_cparams("arbitrary"),
    )(x, four, conv, att, w_out_bf, mod8, ln8, wr_bf, br)


OUTPROJ_GROUPS = 2
TOK_TILE = 256
COMBINE_GROUPS = 8


def _moe_plan(eidx2, tb):
    n = eidx2.shape[0]
    nb = -(-(n * TOP_K) // tb) + N_EXPERTS
    onehot = (eidx2[:, :, None] == jnp.arange(N_EXPERTS, dtype=jnp.int32)).astype(jnp.int32)
    per_tok = onehot.sum(axis=1)
    incl = jnp.cumsum(per_tok, axis=0)
    counts = incl[-1]
    padded = (counts + tb - 1) // tb * tb
    pad_end = jnp.cumsum(padded)
    rank_base = (incl - per_tok) + (pad_end - padded)[None, :]
    dest = jnp.sum(onehot * rank_base[:, None, :], axis=-1).astype(jnp.int32)
    block_row = jnp.arange(nb, dtype=jnp.int32) * tb
    block_e = jnp.minimum(jnp.sum(pad_end[None, :] <= block_row[:, None], axis=1), N_EXPERTS - 1).astype(jnp.int32)
    first = jnp.concatenate([jnp.ones((1,), jnp.int32), (block_e[1:] != block_e[:-1]).astype(jnp.int32)])
    nused = (pad_end[-1] // tb).astype(jnp.int32).reshape(1)
    zrow = jnp.maximum(pad_end - tb, 0).astype(jnp.int32)
    zflag = (counts > 0).astype(jnp.int32)
    return dest, block_e, first, nused, zrow, zflag, nb


def _dispatch_body(zrow_ref, zflag_ref, nused_ref, dest_ref, h_ref, xs_hbm, zbuf, stage, sem, zsem,
                   *, tb, nt, nb):
    i = pl.program_id(0)

    def zero_copy(e):
        return pltpu.make_async_copy(zbuf, xs_hbm.at[pl.ds(pl.multiple_of(zrow_ref[e], tb), tb)], zsem)

    @pl.when(i == 0)
    def _():
        zbuf[...] = jnp.zeros_like(zbuf)
        for e in range(N_EXPERTS):
            @pl.when(zflag_ref[e] == 1)
            def _():
                zero_copy(e).start()
        for e in range(N_EXPERTS):
            @pl.when(zflag_ref[e] == 1)
            def _():
                zero_copy(e).wait()

        def tail_copy(j):
            return pltpu.make_async_copy(zbuf, xs_hbm.at[pl.ds(pl.multiple_of(j * tb, tb), tb)], zsem)

        def start_tail(j, c):
            tail_copy(j).start()
            return c

        def wait_tail(j, c):
            tail_copy(j).wait()
            return c
        lax.fori_loop(nused_ref[0], nb, start_tail, 0)
        lax.fori_loop(nused_ref[0], nb, wait_tail, 0)

    slot = i % 2

    def wait_tile(sl):
        for _ in range(TOP_K):
            pltpu.make_async_copy(stage.at[sl], xs_hbm.at[pl.ds(0, TOK_TILE)], sem.at[sl]).wait()

    @pl.when(i >= 2)
    def _():
        wait_tile(slot)

    stage[slot] = h_ref[...]

    def body(r, c):
        src = stage.at[slot, pl.ds(r, 1)]
        for k in range(TOP_K):
            row = dest_ref[0, 0, TOP_K * r + k]
            pltpu.make_async_copy(src, xs_hbm.at[pl.ds(row, 1)], sem.at[slot]).start()
        return c
    lax.fori_loop(0, TOK_TILE, body, 0, unroll=8)

    @pl.when(i == nt - 1)
    def _():
        wait_tile(slot)
        if nt >= 2:
            wait_tile(1 - slot)


def _dispatch(h2, dest, zrow, zflag, nused, *, tb, nb):
    n, d = h2.shape
    nt = n // TOK_TILE
    grid_spec = pltpu.PrefetchScalarGridSpec(
        num_scalar_prefetch=3,
        grid=(nt,),
        in_specs=[pl.BlockSpec((1, 1, TOP_K * TOK_TILE), lambda i, *_: (i, 0, 0), memory_space=pltpu.SMEM),
                  pl.BlockSpec((TOK_TILE, d), lambda i, *_: (i, 0))],
        out_specs=pl.BlockSpec(memory_space=pl.ANY),
        scratch_shapes=[pltpu.VMEM((tb, d), F32), pltpu.VMEM((2, TOK_TILE, d), F32),
                        pltpu.SemaphoreType.DMA((2,)), pltpu.SemaphoreType.DMA(())],
    )
    return pl.pallas_call(
        functools.partial(_dispatch_body, tb=tb, nt=nt, nb=nb),
        grid_spec=grid_spec,
        out_shape=jax.ShapeDtypeStruct((nb * tb, d), F32),
        compiler_params=_cparams("arbitrary"),
    )(zrow, zflag, nused, dest.reshape(nt, 1, TOP_K * TOK_TILE), h2)


def _experts_body(be_ref, first_ref, nused_ref, x_ref, wg_ref, wu_ref, wd_ref, o_ref, wgb, wub, wdb):
    del be_ref
    i = pl.program_id(0)

    @pl.when(i < nused_ref[0])
    def _():
        @pl.when(first_ref[i] == 1)
        def _():
            wgb[...] = wg_ref[0, 0].astype(BF16)
            wub[...] = wu_ref[0, 0].astype(BF16)
            wdb[...] = wd_ref[0, 0].astype(BF16)

        x = x_ref[...].astype(BF16)
        g = _dot(x, wgb[...])
        u = _dot(x, wub[...])
        mid = (g * jax.nn.sigmoid(g) * u).astype(BF16)
        o_ref[...] = _dot(mid, wdb[...])

    @pl.when(i >= nused_ref[0])
    def _():
        o_ref[...] = jnp.zeros_like(o_ref)


def _experts(xs, block_e, first, nused, w_gate, w_up, w_down, layer, *, tb, nb):
    d = xs.shape[1]
    wspec = lambda shape: pl.BlockSpec((1,) + shape, lambda i, be, first, nused: (layer, be[i], 0, 0))
    xspec = pl.BlockSpec((tb, d), lambda i, be, first, nused: (jnp.minimum(i, nused[0] - 1), 0))
    grid_spec = pltpu.PrefetchScalarGridSpec(
        num_scalar_prefetch=3,
        grid=(nb,),
        in_specs=[xspec, wspec((1, d, D_EXPERT)), wspec((1, d, D_EXPERT)), wspec((1, D_EXPERT, d))],
        out_specs=pl.BlockSpec((tb, d), lambda i, *_: (i, 0)),
        scratch_shapes=[pltpu.VMEM((d, D_EXPERT), BF16), pltpu.VMEM((d, D_EXPERT), BF16),
                        pltpu.VMEM((D_EXPERT, d), BF16)],
    )
    return pl.pallas_call(
        _experts_body,
        grid_spec=grid_spec,
        out_shape=jax.ShapeDtypeStruct((nb * tb, d), F32),
        compiler_params=_cparams("arbitrary"),
    )(block_e, first, nused, xs, w_gate, w_up, w_down)


def _combine_body(dcur_ref, dnext_ref, x_ref, wt_ref, mod_ref, ln_ref, ys_hbm, o_ref, ybuf, sem, *, nt):
    i = pl.program_id(0)

    slot = i % 2

    def start_row(d_ref, sl, r):
        for k in range(TOP_K):
            row = d_ref[0, 0, TOP_K * r + k]
            pltpu.make_async_copy(ys_hbm.at[pl.ds(row, 1)], ybuf.at[sl, k, pl.ds(r, 1)], sem.at[sl]).start()

    def wait_tile(sl):
        for k in range(TOP_K):
            pltpu.make_async_copy(ys_hbm.at[pl.ds(0, TOK_TILE)], ybuf.at[sl, k], sem.at[sl]).wait()

    @pl.when(i == 0)
    def _():
        def body(r, c):
            start_row(dcur_ref, 0, r)
            return c
        lax.fori_loop(0, TOK_TILE, body, 0)

    wait_tile(slot)
    rows_g = TOK_TILE // COMBINE_GROUPS
    for g in range(COMBINE_GROUPS):
        rs = slice(g * rows_g, (g + 1) * rows_g)
        w = wt_ref[rs, :]
        y = w[:, 0:1] * ybuf[slot, 0, rs, :] + w[:, 1:2] * ybuf[slot, 1, rs, :]
        r = ALPHA * x_ref[rs, :] + mod_ref[5:6, :] * y
        o_ref[rs, :] = _ln(r) * ln_ref[0:1, :] + ln_ref[1:2, :]
        for rr in range(g * rows_g, (g + 1) * rows_g):
            start_row(dnext_ref, 1 - slot, rr)

    @pl.when(i == nt - 1)
    def _():
        wait_tile(1 - slot)


def _combine(x, ys, dest, wt, mod8, ln8, *, row0):
    s, d = x.shape
    nt = s // TOK_TILE
    t0 = row0 // TOK_TILE
    dest3 = dest.reshape(-1, 1, TOP_K * TOK_TILE)
    dspec = lambda off: pl.BlockSpec((1, 1, TOP_K * TOK_TILE),
                                     lambda i: (t0 + jnp.minimum(i + off, nt - 1), 0, 0), memory_space=pltpu.SMEM)
    return pl.pallas_call(
        functools.partial(_combine_body, nt=nt),
        grid=(nt,),
        in_specs=[dspec(0), dspec(1),
                  pl.BlockSpec((TOK_TILE, d), lambda i: (i, 0)),
                  pl.BlockSpec((TOK_TILE, ROUTER_LANES), lambda i: (i, 0)),
                  _resident((8, d)), _resident((8, d)),
                  pl.BlockSpec(memory_space=pl.ANY)],
        out_specs=pl.BlockSpec((TOK_TILE, d), lambda i: (i, 0)),
        out_shape=jax.ShapeDtypeStruct((s, d), F32),
        scratch_shapes=[pltpu.VMEM((2, TOP_K, TOK_TILE, d), F32), pltpu.SemaphoreType.DMA((2,))],
        compiler_params=_cparams("arbitrary"),
    )(dest3, dest3, x, wt, mod8, ln8, ys)


def _moe(h2, eidx2, w_gate, w_up, w_down, layer):
    tb = MOE_ROWS
    dest, block_e, first, nused, zrow, zflag, nb = _moe_plan(eidx2, tb)
    xs = _dispatch(h2, dest, zrow, zflag, nused, tb=tb, nb=nb)
    ys = _experts(xs, block_e, first, nused, w_gate, w_up, w_down, layer, tb=tb, nb=nb)
    return ys, dest


def _pick(n, prefs):
    for t in prefs:
        if n % t == 0:
            return t
    raise ValueError(f"no tile of {prefs} divides {n}")


def _rows8(*vecs):
    d = vecs[0].shape[-1]
    out = jnp.zeros((8, d), F32)
    for j, v in enumerate(vecs):
        out = out.at[j].set(v)
    return out


def kernel(x, c, ctx, c_ctx, w_mod, b_mod, w_in, b_in, w_fourier, conv_w, conv_b, conv_ln_g, conv_ln_b, w_pw,
           lam_q1, lam_k1, lam_q2, lam_k2, subln_g, w_out, ln_a_g, ln_a_b, w_rg, b_rg, w_re, b_re,
           w_gate, w_up, w_down, ln_f_g, ln_f_b):
    b, s, d = x.shape
    nc = ctx.shape[1]
    assert b == 1 and d == D_MODEL and c.shape[0] == 1 and ctx.shape[0] == 1
    assert s % (DFT1 * 8) == 0 and nc % TOK_TILE == 0
    xl, xc = x[0], ctx[0]
    depth = w_mod.shape[0]

    c2 = jnp.zeros((8, d), F32).at[0].set(c[0]).at[1].set(c_ctx)
    mod_all = _modulation(c2, w_mod, b_mod)
    tables_l = _rope_tables(s)
    tables_c = tuple(jnp.zeros((nc, LANES), F32) for _ in range(3))
    tm_l = _pick(s, (512, 256))
    tm_c = _pick(nc, (256,))
    tq = _pick(s, (512, 256))
    tk = _pick(s + nc, (3328, 1280, 256))

    for i in range(depth):
        last = i == depth - 1
        lam_init = 0.8 - 0.6 * math.exp(-0.3 * i)
        mod_l = _rows8(*jnp.split(mod_all[i, 0], N_MOD))
        mod_c = _rows8(*jnp.split(mod_all[i, 1], N_MOD))
        w_in_bf = w_in[i].astype(BF16)
        wf_bf = w_fourier[i].astype(BF16)
        wpw_bf = w_pw[i].astype(BF16)
        w_out_bf = w_out[i].astype(BF16)
        wr = jnp.zeros((d, ROUTER_LANES), F32).at[:, :N_GROUPS].set(w_rg[i])
        wr_bf = wr.at[:, N_GROUPS:N_GROUPS + N_EXPERTS].set(w_re[i]).astype(BF16)
        br = jnp.zeros((1, ROUTER_LANES), F32).at[0, :N_GROUPS].set(b_rg[i])
        br = br.at[0, N_GROUPS:N_GROUPS + N_EXPERTS].set(b_re[i])
        lam_p = jnp.zeros((8, LANES), F32).at[0, :ATT_QK_DIM].set(lam_q1[i]).at[1, :ATT_QK_DIM].set(lam_k1[i])
        lam_p = lam_p.at[2, :ATT_QK_DIM].set(lam_q2[i]).at[3, :ATT_QK_DIM].set(lam_k2[i])
        ln_a = _rows8(ln_a_g[i], ln_a_b[i])
        ln_f = _rows8(ln_f_g[i], ln_f_b[i])
        conv_args = (conv_w[i], conv_b[i], conv_ln_g[i], conv_ln_b[i], wpw_bf)

        f_l, u_l, q_l, k_l, v_l, st_l = _inproj(xl, mod_l, w_in_bf, b_in[i], tables_l, rope=True, tm=tm_l)
        f_c, u_c, q_c, k_c, v_c, st_c = _inproj(xc, mod_c, w_in_bf, b_in[i], tables_c, rope=False, tm=tm_c)
        qmax_l, qmax_c = jnp.max(st_l[:, 0, 0]), jnp.max(st_c[:, 0, 0])
        kmax_c = jnp.max(st_c[:, 1, 0])
        kmax_all = jnp.maximum(jnp.max(st_l[:, 1, 0]), kmax_c)
        k_all = jnp.concatenate([k_c, k_l], axis=0)
        v_all = jnp.concatenate([v_c, v_l], axis=0)
        att_l = _attention(q_l, k_all, v_all, qmax_l, kmax_all, lam_p, subln_g[i],
                           lam_init=lam_init, tq=tq, tk=tk)
        four_l = _fourier_fourstep(f_l, wf_bf)
        conv_l = _conv(u_l, *conv_args, tm=tm_l)
        xl, h2_l, ei_l, wt_l = _outproj(xl, four_l, conv_l, att_l, w_out_bf, mod_l, ln_a, wr_bf, br, tm=tm_l)
        if last:
            ys, dest = _moe(h2_l, ei_l[:, :TOP_K], w_gate, w_up, w_down, i)
            xl = _combine(xl, ys, dest, wt_l, mod_l, ln_f, row0=0)
        else:
            att_c = _attention(q_c, k_c, v_c, qmax_c, kmax_c, lam_p, subln_g[i],
                               lam_init=lam_init, tq=tm_c, tk=tm_c)
            four_c = _fourier_dense(f_c, wf_bf)
            conv_c = _conv(u_c, *conv_args, tm=tm_c)
            xc, h2_c, ei_c, wt_c = _outproj(xc, four_c, conv_c, att_c, w_out_bf, mod_c, ln_a, wr_bf, br, tm=tm_c)
            h2 = jnp.concatenate([h2_c, h2_l], axis=0)
            ei = jnp.concatenate([ei_c[:, :TOP_K], ei_l[:, :TOP_K]], axis=0)
            ys, dest = _moe(h2, ei, w_gate, w_up, w_down, i)
            xc = _combine(xc, ys, dest, wt_c, mod_c, ln_f, row0=0)
            xl = _combine(xl, ys, dest, wt_l, mod_l, ln_f, row0=nc)
    return xl[None]
```

```python
import functools
import math

import numpy as np
import jax
import jax.numpy as jnp
from jax import lax
from jax.experimental import pallas as pl
from jax.experimental.pallas import tpu as pltpu

F32 = jnp.float32
BF16 = jnp.bfloat16
F8 = jnp.float8_e4m3fn

D_MODEL = 2048
DEPTH = 2
GRID_W = 64
D_FOURIER = 512
N_FOURIER_GROUPS = 4
FOURIER_GROUP = 128
D_CONV = 512
CONV_WIDTH = 31
CONV_PAD = CONV_WIDTH // 2
N_ATT_HEADS = 8
ATT_QK_DIM = 64
ATT_V_DIM = 128
D_ATT = N_ATT_HEADS * ATT_V_DIM
D_MIX = D_FOURIER + D_CONV + D_ATT
QK_COLS = N_ATT_HEADS * 2 * ATT_QK_DIM
ATT_SCALE = 1.0 / math.sqrt(ATT_QK_DIM)
ROPE_BASE = 10000.0
ROPE_PAIRS_PER_AXIS = ATT_QK_DIM // 4
OFF_CONV = D_FOURIER
OFF_Q = OFF_CONV + 2 * D_CONV
OFF_K = OFF_Q + QK_COLS
OFF_V = OFF_K + QK_COLS
D_IN_PROJ = OFF_V + D_ATT
N_GROUPS = 4
EXPERTS_PER_GROUP = 8
N_EXPERTS = N_GROUPS * EXPERTS_PER_GROUP
TOP_K = 2
D_EXPERT = 512
N_MOD = 6
LN_EPS = 1e-6
ALPHA = (2.0 * DEPTH) ** 0.25

LANES = 128
VMEM_LIMIT = 56 * 1024 * 1024
LOG2E = 1.4426950408889634
Q_SCALE = ATT_SCALE * LOG2E
NEG_BIG = -1e30
DFT1 = 128
MOE_ROWS = 256
ROUTER_LANES = 128
ATT_V_ROWS = ATT_V_DIM + 16
ATT_SUB = 256


def _cparams(*sem):
    return pltpu.CompilerParams(dimension_semantics=tuple(sem), vmem_limit_bytes=VMEM_LIMIT)


def _resident(shape):
    nd = len(shape)
    return pl.BlockSpec(shape, lambda *_: (0,) * nd, pipeline_mode=pl.Buffered(1))


def _ln(x):
    mu = jnp.mean(x, axis=-1, keepdims=True)
    xc = x - mu
    var = jnp.mean(xc * xc, axis=-1, keepdims=True)
    return xc * lax.rsqrt(var + LN_EPS)


def _dot(a, b):
    return jnp.dot(a, b, preferred_element_type=F32)


def _mod_body(c_ref, w_ref, b_ref, o_ref):
    c = c_ref[...]
    sc = (c * jax.nn.sigmoid(c)).astype(BF16)
    o_ref[0] = _dot(sc, w_ref[0].astype(BF16)) + b_ref[0]


def _modulation(c2, w_mod, b_mod):
    nl, d, n = w_mod.shape
    tn = 1024
    return pl.pallas_call(
        _mod_body,
        grid=(nl, n // tn),
        in_specs=[pl.BlockSpec((8, d), lambda l, j: (0, 0)),
                  pl.BlockSpec((1, d, tn), lambda l, j: (l, 0, j)),
                  pl.BlockSpec((1, 1, tn), lambda l, j: (l, 0, j))],
        out_specs=pl.BlockSpec((1, 8, tn), lambda l, j: (l, 0, j)),
        out_shape=jax.ShapeDtypeStruct((nl, 8, n), F32),
        compiler_params=_cparams("arbitrary", "arbitrary"),
    )(c2, w_mod, b_mod.reshape(nl, 1, n))


def _inproj_body(x_ref, mod_ref, w_ref, b_ref, cos_ref, sa_ref, sb_ref,
                 f_ref, u_ref, q_ref, k_ref, v_ref, stat_ref, *, rope):
    h = _ln(x_ref[...]) * (1.0 + mod_ref[1:2, :]) + mod_ref[0:1, :]
    hb = h.astype(BF16)

    def proj(lo, hi):
        return _dot(hb, w_ref[:, lo:hi]) + b_ref[:, lo:hi]

    f_ref[...] = proj(0, OFF_CONV).astype(BF16)
    a = proj(OFF_CONV, OFF_CONV + D_CONV)
    g = proj(OFF_CONV + D_CONV, OFF_Q)
    u_ref[...] = (a * jax.nn.sigmoid(g)).astype(BF16)
    v_ref[...] = proj(OFF_V, D_IN_PROJ).astype(BF16)
    q = proj(OFF_Q, OFF_K)
    k = proj(OFF_K, OFF_V)
    if rope:
        cos, sa, sb = cos_ref[...], sa_ref[...], sb_ref[...]
    qmax = kmax = None
    for hd in range(N_ATT_HEADS):
        sl = slice(hd * LANES, (hd + 1) * LANES)
        qh, kh = q[:, sl], k[:, sl]
        if rope:
            qh = qh * cos + pltpu.roll(qh, LANES - 32, 1) * sa + pltpu.roll(qh, 32, 1) * sb
            kh = kh * cos + pltpu.roll(kh, LANES - 32, 1) * sa + pltpu.roll(kh, 32, 1) * sb
        qb, kb = (qh * Q_SCALE).astype(BF16), kh.astype(BF16)
        q_ref[:, sl] = qb
        k_ref[:, sl] = kb
        qa, ka = jnp.abs(qb.astype(F32)), jnp.abs(kb.astype(F32))
        qmax = qa if qmax is None else jnp.maximum(qmax, qa)
        kmax = ka if kmax is None else jnp.maximum(kmax, ka)
    qm = jnp.max(jnp.max(qmax, axis=0, keepdims=True), axis=1, keepdims=True)
    km = jnp.max(jnp.max(kmax, axis=0, keepdims=True), axis=1, keepdims=True)
    row = lax.broadcasted_iota(jnp.int32, (8, LANES), 0)
    stat_ref[0] = jnp.where(row == 0, qm, jnp.where(row == 1, km, 0.0))


def _inproj(x, mod8, w_bf, b, tables, *, rope, tm):
    s, d = x.shape
    cos, sa, sb = tables
    row = lambda n: pl.BlockSpec((tm, n), lambda i: (i, 0))
    outs = [(D_FOURIER, BF16), (D_CONV, BF16), (QK_COLS, BF16), (QK_COLS, BF16), (D_ATT, BF16)]
    return pl.pallas_call(
        functools.partial(_inproj_body, rope=rope),
        grid=(s // tm,),
        in_specs=[row(d), _resident((8, d)), _resident((d, D_IN_PROJ)), _resident((1, D_IN_PROJ)),
                  row(LANES), row(LANES), row(LANES)],
        out_specs=[row(n) for n, _ in outs] + [pl.BlockSpec((1, 8, LANES), lambda i: (i, 0, 0))],
        out_shape=[jax.ShapeDtypeStruct((s, n), dt) for n, dt in outs]
        + [jax.ShapeDtypeStruct((s // tm, 8, LANES), F32)],
        compiler_params=_cparams("arbitrary"),
    )(x, mod8, w_bf, b.reshape(1, -1), cos, sa, sb)


def _rope_tables(n_tokens):
    rows = n_tokens // GRID_W
    row = jnp.repeat(jnp.arange(rows), GRID_W).astype(F32)
    col = jnp.tile(jnp.arange(GRID_W), rows).astype(F32)
    inv = 1.0 / (ROPE_BASE ** (jnp.arange(ROPE_PAIRS_PER_AXIS, dtype=F32) / ROPE_PAIRS_PER_AXIS))
    ang = jnp.concatenate([row[:, None] * inv, col[:, None] * inv], -1)
    cos, sin = jnp.cos(ang), jnp.sin(ang)
    zero = jnp.zeros_like(sin)
    cos_t = jnp.tile(cos, (1, 4))
    sa = jnp.tile(jnp.concatenate([-sin, zero], -1), (1, 2))
    sb = jnp.tile(jnp.concatenate([zero, sin], -1), (1, 2))
    return cos_t, sa, sb


def _dft_mats(n):
    k = np.arange(n)
    ang = 2.0 * np.pi * ((k[:, None] * k[None, :]) % n) / n
    return np.cos(ang), np.sin(ang)


def _channel_dft(scale):
    c, s = _dft_mats(FOURIER_GROUP)
    eye = np.eye(N_FOURIER_GROUPS)
    return (jnp.asarray(np.kron(eye, c) * scale, BF16), jnp.asarray(np.kron(eye, s) * scale, BF16))


def _fourier1_body(m_ref, x_ref, o_ref):
    ch = o_ref.shape[2]
    res = _dot(m_ref[...], x_ref[...])
    for j in range(o_ref.shape[1]):
        o_ref[:, j, :] = res[:, j * ch:(j + 1) * ch].astype(BF16)


def _fourier2_body(hr_ref, hi_ref, tc_ref, ts_ref, m2_ref, cc_ref, sc_ref, wf_ref, o_ref, *, n2):
    tc_all, ts_all = tc_ref[0], ts_ref[0]
    for j in range(8):
        hr = hr_ref[j].astype(F32)
        hi = hi_ref[j].astype(F32)
        tc, ts = tc_all[:, j:j + 1], ts_all[:, j:j + 1]
        gr = hr * tc + hi * ts
        gi = hi * tc - hr * ts
        g = jnp.concatenate([gr, gi], axis=0).astype(BF16)
        y = _dot(m2_ref[...], g)
        z = _dot(y[:n2].astype(BF16), cc_ref[...]) + _dot(y[n2:].astype(BF16), sc_ref[...])
        o_ref[:, j, :] = _dot(z.astype(BF16), wf_ref[...]).astype(BF16)


def _fourier_fourstep(f, wf_bf):
    l, ch = f.shape
    n2 = l // DFT1
    c1, s1 = _dft_mats(DFT1)
    m1 = jnp.asarray(np.concatenate([c1, -s1], 0), BF16)
    tn = min(8192, n2 * ch)
    h = pl.pallas_call(
        _fourier1_body,
        grid=(n2 * ch // tn,),
        in_specs=[_resident((2 * DFT1, DFT1)), pl.BlockSpec((DFT1, tn), lambda j: (0, j))],
        out_specs=pl.BlockSpec((2 * DFT1, tn // ch, ch), lambda j: (0, j, 0)),
        out_shape=jax.ShapeDtypeStruct((2 * DFT1, n2, ch), BF16),
        compiler_params=_cparams("arbitrary"),
    )(m1, f.reshape(DFT1, n2 * ch))

    k1 = np.arange(DFT1)
    nn = np.arange(n2)
    ang = 2.0 * np.pi * ((nn[:, None] * k1[None, :]) % l) / l
    tc = jnp.asarray(np.cos(ang).reshape(n2, DFT1 // 8, 8).transpose(1, 0, 2), F32)
    ts = jnp.asarray(np.sin(ang).reshape(n2, DFT1 // 8, 8).transpose(1, 0, 2), F32)
    c2, s2 = _dft_mats(n2)
    m2 = jnp.asarray(np.block([[c2, s2], [-s2, c2]]), BF16)
    cc, sc = _channel_dft(1.0 / math.sqrt(l * FOURIER_GROUP))
    nb = DFT1 // 8
    out = pl.pallas_call(
        functools.partial(_fourier2_body, n2=n2),
        grid=(nb,),
        in_specs=[pl.BlockSpec((8, n2, ch), lambda b: (b, 0, 0)),
                  pl.BlockSpec((8, n2, ch), lambda b: (b + nb, 0, 0)),
                  pl.BlockSpec((1, n2, 8), lambda b: (b, 0, 0)),
                  pl.BlockSpec((1, n2, 8), lambda b: (b, 0, 0)),
                  _resident((2 * n2, 2 * n2)), _resident((ch, ch)), _resident((ch, ch)), _resident((ch, ch))],
        out_specs=pl.BlockSpec((n2, 8, ch), lambda b: (0, b, 0)),
        out_shape=jax.ShapeDtypeStruct((n2, DFT1, ch), BF16),
        compiler_params=_cparams("arbitrary"),
    )(h, h, tc, ts, m2, cc, sc, wf_bf)
    return out.reshape(l, ch)


def _fourier_dense_body(f_ref, cc_ref, sc_ref, mp_ref, wf_ref, o_ref):
    f = f_ref[...]
    a = jnp.concatenate([_dot(f, cc_ref[...]), _dot(f, sc_ref[...])], axis=0).astype(BF16)
    z = _dot(mp_ref[...], a)
    o_ref[...] = _dot(z.astype(BF16), wf_ref[...]).astype(BF16)


def _fourier_dense(f, wf_bf):
    l, ch = f.shape
    cl, sl = _dft_mats(l)
    mp = jnp.asarray(np.concatenate([cl, -sl], 1), BF16)
    cc, sc = _channel_dft(1.0 / math.sqrt(l * FOURIER_GROUP))
    return pl.pallas_call(
        _fourier_dense_body,
        grid=(1,),
        in_specs=[_resident((l, ch)), _resident((ch, ch)), _resident((ch, ch)), _resident((l, 2 * l)),
                  _resident((ch, ch))],
        out_specs=pl.BlockSpec((l, ch), lambda i: (0, 0)),
        out_shape=jax.ShapeDtypeStruct((l, ch), BF16),
        compiler_params=_cparams("arbitrary"),
    )(f, cc, sc, mp, wf_bf)


CONV_HALO = 16
CONV_ROWS = 64
SUBLANES = 8
CONV_SPAN_PAD = (CONV_HALO - CONV_PAD + CONV_WIDTH - 1) // SUBLANES * SUBLANES


def _conv_body(prev_ref, cur_ref, next_ref, cw_ref, vec_ref, wpw_ref, o_ref, ubuf, shifted, *, tm, nt):
    t = pl.program_id(0)
    ubuf[0:CONV_HALO, :] = jnp.where(t > 0, prev_ref[...].astype(F32), 0.0)
    ubuf[CONV_HALO:CONV_HALO + tm, :] = cur_ref[...].astype(F32)
    ubuf[CONV_HALO + tm:2 * CONV_HALO + tm, :] = jnp.where(t < nt - 1, next_ref[...].astype(F32), 0.0)
    cb, lg, lb = vec_ref[0:1, :], vec_ref[1:2, :], vec_ref[2:3, :]
    first = CONV_HALO - CONV_PAD
    span = tm + CONV_SPAN_PAD
    for rho in range(SUBLANES):
        shifted[rho, 0:span, :] = ubuf[rho:rho + span, :]
    for c in range(tm // CONV_ROWS):
        acc = None
        for j in range(CONV_WIDTH):
            rho, q = (first + j) % SUBLANES, (first + j) // SUBLANES
            row0 = c * CONV_ROWS + SUBLANES * q
            term = cw_ref[j:j + 1, :] * shifted[rho, row0:row0 + CONV_ROWS, :]
            acc = term if acc is None else acc + term
        y = _ln(acc + cb) * lg + lb
        y = y * jax.nn.sigmoid(y)
        o_ref[c * CONV_ROWS:(c + 1) * CONV_ROWS, :] = _dot(y.astype(BF16), wpw_ref[...]).astype(BF16)


def _conv(u, conv_w, conv_b, ln_g, ln_b, wpw_bf, *, tm):
    s, ch = u.shape
    nt = s // tm
    hb = tm // CONV_HALO
    nh = s // CONV_HALO
    cw = jnp.zeros((32, ch), F32).at[:CONV_WIDTH].set(conv_w)
    vec = jnp.zeros((8, ch), F32).at[0].set(conv_b).at[1].set(ln_g).at[2].set(ln_b)
    return pl.pallas_call(
        functools.partial(_conv_body, tm=tm, nt=nt),
        grid=(nt,),
        in_specs=[pl.BlockSpec((CONV_HALO, ch), lambda t: (jnp.maximum(t * hb - 1, 0), 0)),
                  pl.BlockSpec((tm, ch), lambda t: (t, 0)),
                  pl.BlockSpec((CONV_HALO, ch), lambda t: (jnp.minimum((t + 1) * hb, nh - 1), 0)),
                  _resident((32, ch)), _resident((8, ch)), _resident((ch, ch))],
        out_specs=pl.BlockSpec((tm, ch), lambda t: (t, 0)),
        out_shape=jax.ShapeDtypeStruct((s, ch), BF16),
        scratch_shapes=[pltpu.VMEM((tm + 2 * CONV_HALO, ch), F32),
                        pltpu.VMEM((SUBLANES, tm + CONV_SPAN_PAD, ch), F32)],
        compiler_params=_cparams("arbitrary"),
    )(u, u, u, cw, vec, wpw_bf)


def _split8(x):
    hi = x.astype(F8).astype(F32)
    return hi, (x - hi).astype(F8).astype(F32)


def _prep_q_body(sc_ref, q_ref, o_ref):
    hi, lo = _split8(q_ref[...].astype(F32) * sc_ref[0])
    first = lax.broadcasted_iota(jnp.int32, hi.shape, 1) < ATT_QK_DIM
    hl = (jnp.where(first, hi, pltpu.roll(lo, ATT_QK_DIM, 1)),
          jnp.where(first, pltpu.roll(hi, ATT_QK_DIM, 1), lo))
    for c in range(2):
        t = hl[c].T.astype(F8)
        o_ref[0, c, 0:LANES, :] = t
        o_ref[0, c, LANES:2 * LANES, :] = t


def _prep_k_body(sc_ref, k_ref, o_ref):
    hi, lo = _split8(k_ref[...].astype(F32) * sc_ref[0])
    first = lax.broadcasted_iota(jnp.int32, hi.shape, 1) < ATT_QK_DIM
    his, los = pltpu.roll(hi, ATT_QK_DIM, 1), pltpu.roll(lo, ATT_QK_DIM, 1)
    o_ref[0, 0, :, 0:LANES] = jnp.where(first, hi, his).astype(F8)
    o_ref[0, 0, :, LANES:2 * LANES] = jnp.where(first, lo, los).astype(F8)
    o_ref[0, 1, :, 0:LANES] = jnp.where(first, his, hi).astype(F8)
    o_ref[0, 1, :, LANES:2 * LANES] = jnp.where(first, los, lo).astype(F8)


def _prep_v_body(v_ref, o_ref):
    o_ref[0, 0, 0:ATT_V_DIM, :] = v_ref[...].astype(F32).T.astype(BF16)
    row = lax.broadcasted_iota(jnp.int32, (ATT_V_ROWS - ATT_V_DIM, v_ref.shape[0]), 0)
    o_ref[0, 0, ATT_V_DIM:ATT_V_ROWS, :] = jnp.where(row == 0, 1.0, 0.0).astype(BF16)


def _pow2_scale(mx):
    _, e = jnp.frexp(mx)
    return jnp.ldexp(jnp.float32(1.0), 5 - e).astype(F32)


def _attn_prep(q, k_all, v_all, qmax, kmax, *, tk, tm):
    s, lk = q.shape[0], k_all.shape[0]
    nk = lk // tk
    aq, ak = _pow2_scale(qmax), _pow2_scale(kmax)
    smem = pl.BlockSpec(memory_space=pltpu.SMEM)
    head_rows = lambda t: pl.BlockSpec((t, LANES), lambda h, i: (i, h))
    qt8 = pl.pallas_call(
        _prep_q_body,
        grid=(N_ATT_HEADS, s // tm),
        in_specs=[smem, head_rows(tm)],
        out_specs=pl.BlockSpec((1, 2, 2 * LANES, tm), lambda h, i: (h, 0, 0, i)),
        out_shape=jax.ShapeDtypeStruct((N_ATT_HEADS, 2, 2 * LANES, s), F8),
        compiler_params=_cparams("arbitrary", "arbitrary"),
    )(aq.reshape(1), q)
    tmk = _pick(lk, (3328, 1280, 256))
    k8 = pl.pallas_call(
        _prep_k_body,
        grid=(N_ATT_HEADS, lk // tmk),
        in_specs=[smem, head_rows(tmk)],
        out_specs=pl.BlockSpec((1, 2, tmk, 2 * LANES), lambda h, i: (h, 0, i, 0)),
        out_shape=jax.ShapeDtypeStruct((N_ATT_HEADS, 2, lk, 2 * LANES), F8),
        compiler_params=_cparams("arbitrary", "arbitrary"),
    )(ak.reshape(1), k_all)
    vt = pl.pallas_call(
        _prep_v_body,
        grid=(N_ATT_HEADS, nk),
        in_specs=[head_rows(tk)],
        out_specs=pl.BlockSpec((1, 1, ATT_V_ROWS, tk), lambda h, i: (h, i, 0, 0)),
        out_shape=jax.ShapeDtypeStruct((N_ATT_HEADS, nk, ATT_V_ROWS, tk), BF16),
        compiler_params=_cparams("arbitrary", "arbitrary"),
    )(v_all)
    return qt8, k8.reshape(N_ATT_HEADS, 2, nk, tk, 2 * LANES), vt, (1.0 / (aq * ak)).reshape(1)


def _attn_body(c_ref, lam_ref, g_ref, qt_ref, k_ref, vt_ref, o_ref, acc1, acc2, s1_buf, s2_buf,
               *, nk, tq, lam_init):
    cs = c_ref[0]
    cs_bf = cs.astype(BF16)
    acc1[...] = jnp.zeros_like(acc1)
    acc2[...] = jnp.zeros_like(acc2)

    tk = k_ref.shape[3]
    bufs, qs, accs = (s1_buf, s2_buf), (qt_ref[0, 0], qt_ref[0, 1]), (acc1, acc2)

    def chunk_step(i, slot, mx, m, *, consume, produce):
        if consume:
            mn = [jnp.maximum(m[c], mx[c]) for c in range(2)]
        new_mx, pv = [None, None], [None, None]
        for r in range(tk // ATT_SUB):
            rows = slice(r * ATT_SUB, (r + 1) * ATT_SUB)
            if consume:
                vt = vt_ref[0, i, :, rows]
            for c in range(2):
                if produce:
                    s = _dot(k_ref[0, c, i + 1, rows, :], qs[c])
                    bufs[c][1 - slot, rows, :] = s
                    smax = jnp.max(s, axis=0, keepdims=True)
                    new_mx[c] = smax if new_mx[c] is None else jnp.maximum(new_mx[c], smax)
                if consume:
                    p = jnp.exp2((bufs[c][slot, rows, :] - mn[c]).astype(BF16) * cs_bf)
                    d = _dot(vt, p)
                    pv[c] = d if pv[c] is None else pv[c] + d
        if consume:
            for c in range(2):
                accs[c][...] = jnp.exp2((m[c] - mn[c]) * cs) * accs[c][...] + pv[c]
            m = tuple(mn)
        return tuple(new_mx), m

    def step(i, carry):
        mx, m = carry
        return lax.cond(i % 2 == 0,
                        lambda: chunk_step(i, 0, mx, m, consume=True, produce=True),
                        lambda: chunk_step(i, 1, mx, m, consume=True, produce=True))

    neg = jnp.full((1, tq), -jnp.inf, F32)
    mx0, _ = chunk_step(-1, 1, None, None, consume=False, produce=True)
    mx, m = lax.fori_loop(0, nk - 1, step, (mx0, (neg, neg)))
    chunk_step(nk - 1, (nk - 1) % 2, mx, m, consume=True, produce=False)

    lp = lam_ref[...]
    lam = (jnp.exp(jnp.sum(lp[0:1] * lp[1:2], axis=-1, keepdims=True))
           - jnp.exp(jnp.sum(lp[2:3] * lp[3:4], axis=-1, keepdims=True)) + lam_init)
    dv = ATT_V_DIM
    o = acc1[0:dv, :] / acc1[dv:dv + 1, :] - lam * (acc2[0:dv, :] / acc2[dv:dv + 1, :])
    ms = jnp.mean(o * o, axis=0, keepdims=True)
    o = o * lax.rsqrt(ms + LN_EPS) * g_ref[...] * (1.0 - lam_init)
    o_ref[...] = o.T.astype(BF16)


def _attention(q, k_all, v_all, qmax, kmax, lam_p, g, *, lam_init, tq, tk):
    s = q.shape[0]
    nk = k_all.shape[0] // tk
    qt8, k8, vt, c = _attn_prep(q, k_all, v_all, qmax, kmax, tk=tk, tm=_pick(s, (2048, 512, 256)))
    per_head = lambda shape: pl.BlockSpec(shape, lambda h, i: (h,) + (0,) * (len(shape) - 1),
                                          pipeline_mode=pl.Buffered(1))
    return pl.pallas_call(
        functools.partial(_attn_body, nk=nk, tq=tq, lam_init=lam_init),
        grid=(N_ATT_HEADS, s // tq),
        in_specs=[pl.BlockSpec(memory_space=pltpu.SMEM),
                  pl.BlockSpec((8, LANES), lambda h, i: (0, 0)),
                  pl.BlockSpec((ATT_V_DIM, 1), lambda h, i: (0, 0)),
                  pl.BlockSpec((1, 2, 2 * LANES, tq), lambda h, i: (h, 0, 0, i)),
                  per_head((1, 2, nk, tk, 2 * LANES)),
                  per_head((1, nk, ATT_V_ROWS, tk))],
        out_specs=pl.BlockSpec((tq, ATT_V_DIM), lambda h, i: (i, h)),
        out_shape=jax.ShapeDtypeStruct((s, D_ATT), BF16),
        scratch_shapes=[pltpu.VMEM((ATT_V_ROWS, tq), F32), pltpu.VMEM((ATT_V_ROWS, tq), F32)]
        + [pltpu.VMEM((2, tk, tq), F32) for _ in range(2)],
        compiler_params=_cparams("arbitrary", "arbitrary"),
    )(c, lam_p, g.reshape(ATT_V_DIM, 1), qt8, k8, vt)


def _route(logits):
    lane = lax.broadcasted_iota(jnp.int32, logits.shape, 1)
    lane_f = lane.astype(F32)
    is_g = lane < N_GROUPS
    gl = jnp.where(is_g, logits, NEG_BIG)
    gmax = jnp.max(gl, axis=-1, keepdims=True)
    gidx = jnp.min(jnp.where(gl == gmax, lane_f, float(ROUTER_LANES)), axis=-1, keepdims=True)
    wg = 1.0 / jnp.sum(jnp.where(is_g, jnp.exp(gl - gmax), 0.0), axis=-1, keepdims=True)
    grp = ((lane - N_GROUPS) // EXPERTS_PER_GROUP).astype(F32)
    valid = (lane >= N_GROUPS) & (lane < N_GROUPS + N_EXPERTS) & (grp == gidx)
    el = jnp.where(valid, logits, NEG_BIG)
    m1 = jnp.max(el, axis=-1, keepdims=True)
    i1 = jnp.min(jnp.where(el == m1, lane_f, float(ROUTER_LANES)), axis=-1, keepdims=True)
    el2 = jnp.where(lane_f == i1, NEG_BIG, el)
    m2 = jnp.max(el2, axis=-1, keepdims=True)
    i2 = jnp.min(jnp.where(el2 == m2, lane_f, float(ROUTER_LANES)), axis=-1, keepdims=True)
    t = jnp.exp(m2 - m1)
    w1 = wg / (1.0 + t)
    w2 = wg * t / (1.0 + t)
    e1 = (i1 - N_GROUPS).astype(jnp.int32)
    e2 = (i2 - N_GROUPS).astype(jnp.int32)
    eidx = jnp.where(lane == 0, e1, jnp.where(lane == 1, e2, 0))
    wts = jnp.where(lane == 0, w1, jnp.where(lane == 1, w2, 0.0))
    return eidx, wts


def _outproj_body(x_ref, f_ref, c_ref, a_ref, w_ref, mod_ref, ln_ref, wr_ref, br_ref,
                  xo_ref, h2_ref, ei_ref, wt_ref):
    def project(rs):
        return (_dot(f_ref[rs, :], w_ref[0:D_FOURIER, :])
                + _dot(c_ref[rs, :], w_ref[D_FOURIER:D_FOURIER + D_CONV, :])
                + _dot(a_ref[rs, :], w_ref[D_FOURIER + D_CONV:D_MIX, :]))

    def finish(rs, mix):
        r = ALPHA * x_ref[rs, :] + mod_ref[2:3, :] * mix
        xn = _ln(r) * ln_ref[0:1, :] + ln_ref[1:2, :]
        xo_ref[rs, :] = xn
        h2 = _ln(xn) * (1.0 + mod_ref[4:5, :]) + mod_ref[3:4, :]
        h2_ref[rs, :] = h2
        logits = _dot(h2.astype(BF16), wr_ref[...]) + br_ref[...]
        eidx, wts = _route(logits)
        ei_ref[rs, :] = eidx
        wt_ref[rs, :] = wts

    tm = x_ref.shape[0]
    rows_g = tm // OUTPROJ_GROUPS
    pending = None
    for g in range(OUTPROJ_GROUPS):
        rs = slice(g * rows_g, (g + 1) * rows_g)
        mix = project(rs)
        if pending is not None:
            finish(*pending)
        pending = (rs, mix)
    finish(*pending)


def _outproj(x, four, conv, att, w_out_bf, mod8, ln8, wr_bf, br, *, tm):
    s, d = x.shape
    row = lambda n: pl.BlockSpec((tm, n), lambda i: (i, 0))
    return pl.pallas_call(
        _outproj_body,
        grid=(s // tm,),
        in_specs=[row(d), row(D_FOURIER), row(D_CONV), row(D_ATT), _resident((D_MIX, d)), _resident((8, d)),
                  _resident((8, d)), _resident((d, ROUTER_LANES)), _resident((1, ROUTER_LANES))],
        out_specs=[row(d), row(d), row(ROUTER_LANES), row(ROUTER_LANES)],
        out_shape=[jax.ShapeDtypeStruct((s, d), F32), jax.ShapeDtypeStruct((s, d), F32),
                   jax.ShapeDtypeStruct((s, ROUTER_LANES), jnp.int32),
                   jax.ShapeDtypeStruct((s, ROUTER_LANES), F32)],
        compiler_params=_cparams("arbitrary"),
    )(x, four, conv, att, w_out_bf, mod8, ln8, wr_bf, br)


OUTPROJ_GROUPS = 2
TOK_TILE = 256
COMBINE_GROUPS = 8


def _moe_plan(eidx2, tb):
    n = eidx2.shape[0]
    nb = -(-(n * TOP_K) // tb) + N_EXPERTS
    onehot = (eidx2[:, :, None] == jnp.arange(N_EXPERTS, dtype=jnp.int32)).astype(jnp.int32)
    per_tok = onehot.sum(axis=1)
    incl = jnp.cumsum(per_tok, axis=0)
    counts = incl[-1]
    padded = (counts + tb - 1) // tb * tb
    pad_end = jnp.cumsum(padded)
    rank_base = (incl - per_tok) + (pad_end - padded)[None, :]
    dest = jnp.sum(onehot * rank_base[:, None, :], axis=-1).astype(jnp.int32)
    block_row = jnp.arange(nb, dtype=jnp.int32) * tb
    block_e = jnp.minimum(jnp.sum(pad_end[None, :] <= block_row[:, None], axis=1), N_EXPERTS - 1).astype(jnp.int32)
    first = jnp.concatenate([jnp.ones((1,), jnp.int32), (block_e[1:] != block_e[:-1]).astype(jnp.int32)])
    nused = (pad_end[-1] // tb).astype(jnp.int32).reshape(1)
    zrow = jnp.maximum(pad_end - tb, 0).astype(jnp.int32)
    zflag = (counts > 0).astype(jnp.int32)
    return dest, block_e, first, nused, zrow, zflag, nb


def _dispatch_body(zrow_ref, zflag_ref, nused_ref, dest_ref, h_ref, xs_hbm, zbuf, stage, sem, zsem,
                   *, tb, nt, nb):
    i = pl.program_id(0)

    def zero_copy(e):
        return pltpu.make_async_copy(zbuf, xs_hbm.at[pl.ds(pl.multiple_of(zrow_ref[e], tb), tb)], zsem)

    @pl.when(i == 0)
    def _():
        zbuf[...] = jnp.zeros_like(zbuf)
        for e in range(N_EXPERTS):
            @pl.when(zflag_ref[e] == 1)
            def _():
                zero_copy(e).start()
        for e in range(N_EXPERTS):
            @pl.when(zflag_ref[e] == 1)
            def _():
                zero_copy(e).wait()

        def tail_copy(j):
            return pltpu.make_async_copy(zbuf, xs_hbm.at[pl.ds(pl.multiple_of(j * tb, tb), tb)], zsem)

        def start_tail(j, c):
            tail_copy(j).start()
            return c

        def wait_tail(j, c):
            tail_copy(j).wait()
            return c
        lax.fori_loop(nused_ref[0], nb, start_tail, 0)
        lax.fori_loop(nused_ref[0], nb, wait_tail, 0)

    slot = i % 2

    def wait_tile(sl):
        for _ in range(TOP_K):
            pltpu.make_async_copy(stage.at[sl], xs_hbm.at[pl.ds(0, TOK_TILE)], sem.at[sl]).wait()

    @pl.when(i >= 2)
    def _():
        wait_tile(slot)

    stage[slot] = h_ref[...]

    for r in range(TOK_TILE):
        src = stage.at[slot, pl.ds(r, 1)]
        for k in range(TOP_K):
            row = dest_ref[0, 0, TOP_K * r + k]
            pltpu.make_async_copy(src, xs_hbm.at[pl.ds(row, 1)], sem.at[slot]).start()

    @pl.when(i == nt - 1)
    def _():
        wait_tile(slot)
        if nt >= 2:
            wait_tile(1 - slot)


def _dispatch(h2, dest, zrow, zflag, nused, *, tb, nb):
    n, d = h2.shape
    nt = n // TOK_TILE
    grid_spec = pltpu.PrefetchScalarGridSpec(
        num_scalar_prefetch=3,
        grid=(nt,),
        in_specs=[pl.BlockSpec((1, 1, TOP_K * TOK_TILE), lambda i, *_: (i, 0, 0), memory_space=pltpu.SMEM),
                  pl.BlockSpec((TOK_TILE, d), lambda i, *_: (i, 0))],
        out_specs=pl.BlockSpec(memory_space=pl.ANY),
        scratch_shapes=[pltpu.VMEM((tb, d), F32), pltpu.VMEM((2, TOK_TILE, d), F32),
                        pltpu.SemaphoreType.DMA((2,)), pltpu.SemaphoreType.DMA(())],
    )
    return pl.pallas_call(
        functools.partial(_dispatch_body, tb=tb, nt=nt, nb=nb),
        grid_spec=grid_spec,
        out_shape=jax.ShapeDtypeStruct((nb * tb, d), F32),
        compiler_params=_cparams("arbitrary"),
    )(zrow, zflag, nused, dest.reshape(nt, 1, TOP_K * TOK_TILE), h2)


def _experts_body(be_ref, first_ref, nused_ref, x_ref, wg_ref, wu_ref, wd_ref, o_ref, wgb, wub, wdb):
    del be_ref
    i = pl.program_id(0)

    @pl.when(i < nused_ref[0])
    def _():
        @pl.when(first_ref[i] == 1)
        def _():
            wgb[...] = wg_ref[0, 0].astype(BF16)
            wub[...] = wu_ref[0, 0].astype(BF16)
            wdb[...] = wd_ref[0, 0].astype(BF16)

        x = x_ref[...].astype(BF16)
        g = _dot(x, wgb[...])
        u = _dot(x, wub[...])
        mid = (g * jax.nn.sigmoid(g) * u).astype(BF16)
        o_ref[...] = _dot(mid, wdb[...])

    @pl.when(i >= nused_ref[0])
    def _():
        o_ref[...] = jnp.zeros_like(o_ref)


def _experts(xs, block_e, first, nused, w_gate, w_up, w_down, layer, *, tb, nb):
    d = xs.shape[1]
    wspec = lambda shape: pl.BlockSpec((1,) + shape, lambda i, be, first, nused: (layer, be[i], 0, 0))
    xspec = pl.BlockSpec((tb, d), lambda i, be, first, nused: (jnp.minimum(i, nused[0] - 1), 0))
    grid_spec = pltpu.PrefetchScalarGridSpec(
        num_scalar_prefetch=3,
        grid=(nb,),
        in_specs=[xspec, wspec((1, d, D_EXPERT)), wspec((1, d, D_EXPERT)), wspec((1, D_EXPERT, d))],
        out_specs=pl.BlockSpec((tb, d), lambda i, *_: (i, 0)),
        scratch_shapes=[pltpu.VMEM((d, D_EXPERT), BF16), pltpu.VMEM((d, D_EXPERT), BF16),
                        pltpu.VMEM((D_EXPERT, d), BF16)],
    )
    return pl.pallas_call(
        _experts_body,
        grid_spec=grid_spec,
        out_shape=jax.ShapeDtypeStruct((nb * tb, d), F32),
        compiler_params=_cparams("arbitrary"),
    )(block_e, first, nused, xs, w_gate, w_up, w_down)


def _combine_body(dcur_ref, dnext_ref, x_ref, wt_ref, mod_ref, ln_ref, ys_hbm, o_ref, ybuf, sem, *, nt):
    i = pl.program_id(0)

    slot = i % 2

    def start_row(d_ref, sl, r):
        for k in range(TOP_K):
            row = d_ref[0, 0, TOP_K * r + k]
            pltpu.make_async_copy(ys_hbm.at[pl.ds(row, 1)], ybuf.at[sl, k, pl.ds(r, 1)], sem.at[sl]).start()

    def wait_tile(sl):
        for k in range(TOP_K):
            pltpu.make_async_copy(ys_hbm.at[pl.ds(0, TOK_TILE)], ybuf.at[sl, k], sem.at[sl]).wait()

    @pl.when(i == 0)
    def _():
        def body(r, c):
            start_row(dcur_ref, 0, r)
            return c
        lax.fori_loop(0, TOK_TILE, body, 0)

    wait_tile(slot)
    rows_g = TOK_TILE // COMBINE_GROUPS
    for g in range(COMBINE_GROUPS):
        rs = slice(g * rows_g, (g + 1) * rows_g)
        w = wt_ref[rs, :]
        y = w[:, 0:1] * ybuf[slot, 0, rs, :] + w[:, 1:2] * ybuf[slot, 1, rs, :]
        r = ALPHA * x_ref[rs, :] + mod_ref[5:6, :] * y
        o_ref[rs, :] = _ln(r) * ln_ref[0:1, :] + ln_ref[1:2, :]
        for rr in range(g * rows_g, (g + 1) * rows_g):
            start_row(dnext_ref, 1 - slot, rr)

    @pl.when(i == nt - 1)
    def _():
        wait_tile(1 - slot)


def _combine(x, ys, dest, wt, mod8, ln8, *, row0):
    s, d = x.shape
    nt = s // TOK_TILE
    t0 = row0 // TOK_TILE
    dest3 = dest.reshape(-1, 1, TOP_K * TOK_TILE)
    dspec = lambda off: pl.BlockSpec((1, 1, TOP_K * TOK_TILE),
                                     lambda i: (t0 + jnp.minimum(i + off, nt - 1), 0, 0), memory_space=pltpu.SMEM)
    return pl.pallas_call(
        functools.partial(_combine_body, nt=nt),
        grid=(nt,),
        in_specs=[dspec(0), dspec(1),
                  pl.BlockSpec((TOK_TILE, d), lambda i: (i, 0)),
                  pl.BlockSpec((TOK_TILE, ROUTER_LANES), lambda i: (i, 0)),
                  _resident((8, d)), _resident((8, d)),
                  pl.BlockSpec(memory_space=pl.ANY)],
        out_specs=pl.BlockSpec((TOK_TILE, d), lambda i: (i, 0)),
        out_shape=jax.ShapeDtypeStruct((s, d), F32),
        scratch_shapes=[pltpu.VMEM((2, TOP_K, TOK_TILE, d), F32), pltpu.SemaphoreType.DMA((2,))],
        compiler_params=_cparams("arbitrary"),
    )(dest3, dest3, x, wt, mod8, ln8, ys)


def _moe(h2, eidx2, w_gate, w_up, w_down, layer):
    tb = MOE_ROWS
    dest, block_e, first, nused, zrow, zflag, nb = _moe_plan(eidx2, tb)
    xs = _dispatch(h2, dest, zrow, zflag, nused, tb=tb, nb=nb)
    ys = _experts(xs, block_e, first, nused, w_gate, w_up, w_down, layer, tb=tb, nb=nb)
    return ys, dest


def _pick(n, prefs):
    for t in prefs:
        if n % t == 0:
            return t
    raise ValueError(f"no tile of {prefs} divides {n}")


def _rows8(*vecs):
    d = vecs[0].shape[-1]
    out = jnp.zeros((8, d), F32)
    for j, v in enumerate(vecs):
        out = out.at[j].set(v)
    return out


def kernel(x, c, ctx, c_ctx, w_mod, b_mod, w_in, b_in, w_fourier, conv_w, conv_b, conv_ln_g, conv_ln_b, w_pw,
           lam_q1, lam_k1, lam_q2, lam_k2, subln_g, w_out, ln_a_g, ln_a_b, w_rg, b_rg, w_re, b_re,
           w_gate, w_up, w_down, ln_f_g, ln_f_b):
    b, s, d = x.shape
    nc = ctx.shape[1]
    assert b == 1 and d == D_MODEL and c.shape[0] == 1 and ctx.shape[0] == 1
    assert s % (DFT1 * 8) == 0 and nc % TOK_TILE == 0
    xl, xc = x[0], ctx[0]
    depth = w_mod.shape[0]

    c2 = jnp.zeros((8, d), F32).at[0].set(c[0]).at[1].set(c_ctx)
    mod_all = _modulation(c2, w_mod, b_mod)
    tables_l = _rope_tables(s)
    tables_c = tuple(jnp.zeros((nc, LANES), F32) for _ in range(3))
    tm_l = _pick(s, (512, 256))
    tm_c = _pick(nc, (256,))
    tq = _pick(s, (512, 256))
    tk = _pick(s + nc, (3328, 1280, 256))

    for i in range(depth):
        last = i == depth - 1
        lam_init = 0.8 - 0.6 * math.exp(-0.3 * i)
        mod_l = _rows8(*jnp.split(mod_all[i, 0], N_MOD))
        mod_c = _rows8(*jnp.split(mod_all[i, 1], N_MOD))
        w_in_bf = w_in[i].astype(BF16)
        wf_bf = w_fourier[i].astype(BF16)
        wpw_bf = w_pw[i].astype(BF16)
        w_out_bf = w_out[i].astype(BF16)
        wr = jnp.zeros((d, ROUTER_LANES), F32).at[:, :N_GROUPS].set(w_rg[i])
        wr_bf = wr.at[:, N_GROUPS:N_GROUPS + N_EXPERTS].set(w_re[i]).astype(BF16)
        br = jnp.zeros((1, ROUTER_LANES), F32).at[0, :N_GROUPS].set(b_rg[i])
        br = br.at[0, N_GROUPS:N_GROUPS + N_EXPERTS].set(b_re[i])
        lam_p = jnp.zeros((8, LANES), F32).at[0, :ATT_QK_DIM].set(lam_q1[i]).at[1, :ATT_QK_DIM].set(lam_k1[i])
        lam_p = lam_p.at[2, :ATT_QK_DIM].set(lam_q2[i]).at[3, :ATT_QK_DIM].set(lam_k2[i])
        ln_a = _rows8(ln_a_g[i], ln_a_b[i])
        ln_f = _rows8(ln_f_g[i], ln_f_b[i])
        conv_args = (conv_w[i], conv_b[i], conv_ln_g[i], conv_ln_b[i], wpw_bf)

        f_l, u_l, q_l, k_l, v_l, st_l = _inproj(xl, mod_l, w_in_bf, b_in[i], tables_l, rope=True, tm=tm_l)
        f_c, u_c, q_c, k_c, v_c, st_c = _inproj(xc, mod_c, w_in_bf, b_in[i], tables_c, rope=False, tm=tm_c)
        qmax_l, qmax_c = jnp.max(st_l[:, 0, 0]), jnp.max(st_c[:, 0, 0])
        kmax_c = jnp.max(st_c[:, 1, 0])
        kmax_all = jnp.maximum(jnp.max(st_l[:, 1, 0]), kmax_c)
        k_all = jnp.concatenate([k_c, k_l], axis=0)
        v_all = jnp.concatenate([v_c, v_l], axis=0)
        att_l = _attention(q_l, k_all, v_all, qmax_l, kmax_all, lam_p, subln_g[i],
                           lam_init=lam_init, tq=tq, tk=tk)
        four_l = _fourier_fourstep(f_l, wf_bf)
        conv_l = _conv(u_l, *conv_args, tm=tm_l)
        xl, h2_l, ei_l, wt_l = _outproj(xl, four_l, conv_l, att_l, w_out_bf, mod_l, ln_a, wr_bf, br, tm=tm_l)
        if last:
            ys, dest = _moe(h2_l, ei_l[:, :TOP_K], w_gate, w_up, w_down, i)
            xl = _combine(xl, ys, dest, wt_l, mod_l, ln_f, row0=0)
        else:
            att_c = _attention(q_c, k_c, v_c, qmax_c, kmax_c, lam_p, subln_g[i],
                               lam_init=lam_init, tq=tm_c, tk=tm_c)
            four_c = _fourier_dense(f_c, wf_bf)
            conv_c = _conv(u_c, *conv_args, tm=tm_c)
            xc, h2_c, ei_c, wt_c = _outproj(xc, four_c, conv_c, att_c, w_out_bf, mod_c, ln_a, wr_bf, br, tm=tm_c)
            h2 = jnp.concatenate([h2_c, h2_l], axis=0)
            ei = jnp.concatenate([ei_c[:, :TOP_K], ei_l[:, :TOP_K]], axis=0)
            ys, dest = _moe(h2, ei, w_gate, w_up, w_down, i)
            xc = _combine(xc, ys, dest, wt_c, mod_c, ln_f, row0=0)
            xl = _combine(xl, ys, dest, wt_l, mod_l, ln_f, row0=nc)
    return xl[None]
```
